```python
import jax, jax.numpy as jnp
from jax import lax
import numpy as np

D_MODEL = 1024
BATCH = 8
SEQ = 4096
DEPTH = 1

HEAD_DIM = 64
NA_HEADS = 8
GQA_HEADS = 8
GQA_KV_HEADS = 2
GQA_GROUP = GQA_HEADS // GQA_KV_HEADS
NA_WIDTH = NA_HEADS * HEAD_DIM
GQA_WIDTH = GQA_HEADS * HEAD_DIM
KV_WIDTH = GQA_KV_HEADS * HEAD_DIM
MIX_WIDTH = NA_WIDTH + GQA_WIDTH
IN_WIDTH = 3 * NA_WIDTH + GQA_WIDTH + 2 * KV_WIDTH
IN_SPLITS = (NA_WIDTH, 2 * NA_WIDTH, 3 * NA_WIDTH,
             3 * NA_WIDTH + GQA_WIDTH, 3 * NA_WIDTH + GQA_WIDTH + KV_WIDTH)
GRID_W = 64
NA_WIN_H = 8
NA_WIN_W = 16
Q_BLOCK = 128
ROPE_THETA = 10000.0
ROPE_ROW_DIMS = HEAD_DIM // 2
ROPE_COL_DIMS = HEAD_DIM - ROPE_ROW_DIMS
D_FF = -(-8 * D_MODEL // (3 * 256)) * 256
N_MOD = 6
EPS = 1e-6

kernel_name = "hybrid_natten_gqa_axialrope_adaln_block"


def rms_norm(x, gain):
    xf = x.astype(jnp.float32)
    y = xf * lax.rsqrt(jnp.mean(xf * xf, axis=-1, keepdims=True) + EPS)
    return (y * gain.astype(jnp.float32)).astype(x.dtype)


def axial_rope_tables(n_tokens):
    t = jnp.arange(n_tokens)
    row = (t // GRID_W).astype(jnp.float32)
    col = (t % GRID_W).astype(jnp.float32)

    def angles(pos, dims):
        inv = ROPE_THETA ** (-jnp.arange(0, dims, 2, dtype=jnp.float32) / dims)
        return pos[:, None] * inv[None, :]

    ang = jnp.concatenate([angles(row, ROPE_ROW_DIMS), angles(col, ROPE_COL_DIMS)], axis=-1)
    return jnp.cos(ang), jnp.sin(ang)


def apply_axial_rope(x, cos, sin):
    xf = x.astype(jnp.float32)
    x1, x2 = jnp.split(xf, 2, axis=-1)
    out = jnp.concatenate([x1 * cos - x2 * sin, x1 * sin + x2 * cos], axis=-1)
    return out.astype(x.dtype)


def neighborhood_attention(q, k, v, rpb):
    B, H, T, D = q.shape
    rows = T // GRID_W
    kh = min(NA_WIN_H, rows)
    kw = NA_WIN_W
    q = q.reshape(B, H, rows, GRID_W, D)
    k = k.reshape(B, H, rows, GRID_W, D)
    v = v.reshape(B, H, rows, GRID_W, D)
    cols = jnp.arange(GRID_W)
    col_start = jnp.clip(cols - kw // 2, 0, GRID_W - kw)
    col_idx = col_start[:, None] + jnp.arange(kw)[None, :]
    col_off = col_idx - cols[:, None] + (NA_WIN_W - 1)
    scale = D ** -0.5

    def row_block(r):
        rs = jnp.clip(r - kh // 2, 0, rows - kh)
        q_r = lax.dynamic_index_in_dim(q, r, axis=2, keepdims=False)
        k_band = lax.dynamic_slice_in_dim(k, rs, kh, axis=2)
        v_band = lax.dynamic_slice_in_dim(v, rs, kh, axis=2)
        k_nb = k_band[:, :, :, col_idx, :]
        v_nb = v_band[:, :, :, col_idx, :]
        row_off = rs + jnp.arange(kh) - r + (NA_WIN_H - 1)
        bias = rpb[:, row_off[None, :, None], col_off[:, None, :]]
        s = jnp.einsum('bhwd,bhiwjd->bhwij', q_r, k_nb,
                       preferred_element_type=jnp.float32) * scale
        s = s + bias[None].astype(jnp.float32)
        p = jax.nn.softmax(s.reshape(B, H, GRID_W, kh * kw), axis=-1)
        p = p.reshape(B, H, GRID_W, kh, kw).astype(v.dtype)
        return jnp.einsum('bhwij,bhiwjd->bhwd', p, v_nb)

    o = lax.map(row_block, jnp.arange(rows))
    return o.transpose(1, 2, 0, 3, 4).reshape(B, H, T, D)


def gqa_attention(q, k, v):
    B, Hkv, G, T, D = q.shape
    nb = T // Q_BLOCK
    qb = q.reshape(B, Hkv, G, nb, Q_BLOCK, D).transpose(3, 0, 1, 2, 4, 5)
    scale = D ** -0.5

    def block(q_i):
        s = jnp.einsum('bkgqd,bksd->bkgqs', q_i, k,
                       preferred_element_type=jnp.float32) * scale
        p = jax.nn.softmax(s, axis=-1).astype(v.dtype)
        return jnp.einsum('bkgqs,bksd->bkgqd', p, v)

    o = lax.map(block, qb)
    return o.transpose(1, 2, 3, 0, 4, 5).reshape(B, Hkv * G, T, D)


def setup_inputs(seed: int = 0) -> dict:
    key = jax.random.key(seed)
    ks = jax.random.split(key, 16)
    f32 = jnp.float32
    nrm = lambda k, shape, s: jax.random.normal(k, shape, f32) * s
    return {
        "x": nrm(ks[0], (BATCH, SEQ, D_MODEL), 1.0),
        "c": nrm(ks[1], (BATCH, D_MODEL), 1.0),
        "w_ada": nrm(ks[2], (DEPTH, D_MODEL, N_MOD * D_MODEL), 0.5 * D_MODEL ** -0.5),
        "b_ada": nrm(ks[3], (DEPTH, N_MOD * D_MODEL), 0.02),
        "g_attn": 1.0 + nrm(ks[4], (DEPTH, D_MODEL), 0.02),
        "w_in": nrm(ks[5], (DEPTH, D_MODEL, IN_WIDTH), D_MODEL ** -0.5),
        "g_q": 1.0 + nrm(ks[6], (DEPTH, HEAD_DIM), 0.02),
        "g_k": 1.0 + nrm(ks[7], (DEPTH, HEAD_DIM), 0.02),
        "rpb": nrm(ks[8], (DEPTH, NA_HEADS, 2 * NA_WIN_H - 1, 2 * NA_WIN_W - 1), 0.1),
        "w_o": nrm(ks[9], (DEPTH, MIX_WIDTH, D_MODEL), MIX_WIDTH ** -0.5),
        "g_ffn": 1.0 + nrm(ks[10], (DEPTH, D_MODEL), 0.02),
        "w_gate": nrm(ks[11], (DEPTH, D_MODEL, D_FF), D_MODEL ** -0.5),
        "w_up": nrm(ks[12], (DEPTH, D_MODEL, D_FF), D_MODEL ** -0.5),
        "w_down": nrm(ks[13], (DEPTH, D_FF, D_MODEL), D_FF ** -0.5),
        "g_final": 1.0 + nrm(ks[14], (D_MODEL,), 0.02),
    }


def reference(x, c, w_ada, b_ada, g_attn, w_in, g_q, g_k, rpb, w_o,
              g_ffn, w_gate, w_up, w_down, g_final):
    B, T, _ = x.shape
    cos, sin = axial_rope_tables(T)
    c_act = jax.nn.silu(c)

    def heads(t, n):
        return t.reshape(B, T, n, HEAD_DIM).transpose(0, 2, 1, 3)

    for l in range(DEPTH):
        mod = (jnp.dot(c_act, w_ada[l]) + b_ada[l])[:, None, :]
        shift_a, scale_a, gate_a, shift_f, scale_f, gate_f = jnp.split(mod, N_MOD, axis=-1)

        h = rms_norm(x, g_attn[l]) * (1.0 + scale_a) + shift_a
        proj = jnp.dot(h, w_in[l])
        q_na, k_na, v_na, q_g, k_g, v_g = jnp.split(proj, IN_SPLITS, axis=-1)

        o_na = neighborhood_attention(heads(q_na, NA_HEADS), heads(k_na, NA_HEADS),
                                      heads(v_na, NA_HEADS), rpb[l])

        q_g = apply_axial_rope(rms_norm(heads(q_g, GQA_HEADS), g_q[l]), cos, sin)
        k_g = apply_axial_rope(rms_norm(heads(k_g, GQA_KV_HEADS), g_k[l]), cos, sin)
        v_g = heads(v_g, GQA_KV_HEADS)
        o_g = gqa_attention(q_g.reshape(B, GQA_KV_HEADS, GQA_GROUP, T, HEAD_DIM),
                            k_g, v_g)

        o = jnp.concatenate([o_na, o_g], axis=1).transpose(0, 2, 1, 3).reshape(B, T, MIX_WIDTH)
        x = x + gate_a * jnp.dot(o, w_o[l])

        h = rms_norm(x, g_ffn[l]) * (1.0 + scale_f) + shift_f
        ff = jax.nn.silu(jnp.dot(h, w_gate[l])) * jnp.dot(h, w_up[l])
        x = x + gate_f * jnp.dot(ff, w_down[l])

    return rms_norm(x, g_final)
```

```python
import functools

import jax
import jax.numpy as jnp
from jax import lax
from jax.experimental import pallas as pl
from jax.experimental.pallas import tpu as pltpu

F32 = jnp.float32
BF16 = jnp.bfloat16

D_MODEL = 1024
HEAD_DIM = 64
NA_HEADS = 8
GQA_HEADS = 8
GQA_KV_HEADS = 2
GQA_GROUP = GQA_HEADS // GQA_KV_HEADS
NA_WIDTH = NA_HEADS * HEAD_DIM
GQA_WIDTH = GQA_HEADS * HEAD_DIM
KV_WIDTH = GQA_KV_HEADS * HEAD_DIM
IN_WIDTH = 3 * NA_WIDTH + GQA_WIDTH + 2 * KV_WIDTH
V_WIDTH = NA_WIDTH + KV_WIDTH
GRID_W = 64
NA_WIN_H = 8
NA_WIN_W = 16
ROPE_THETA = 10000.0
N_MOD = 6
EPS = 1e-6

LANES = 128
ROWBLOCK_ROWS = 4
ROWBLOCK = ROWBLOCK_ROWS * GRID_W
BAND_BLOCKS = 3
BAND = BAND_BLOCKS * ROWBLOCK
MASKED = -1e30
VMEM_LIMIT = 56 * 1024 * 1024

NT_DIMS = (((1,), (1,)), ((), ()))


def _silu(x):
    return x / (1.0 + jnp.exp(-x))


def _ada_kernel(c_ref, w_ref, b_ref, o_ref):
    ca = _silu(c_ref[...])
    o_ref[...] = jnp.dot(ca.astype(BF16), w_ref[...].astype(BF16),
                         preferred_element_type=F32) + b_ref[...]


def _ada(c, w_ada, b_ada):
    B = c.shape[0]
    n = w_ada.shape[1]
    return pl.pallas_call(
        _ada_kernel,
        out_shape=jax.ShapeDtypeStruct((B, n), F32),
        grid=(n // D_MODEL,),
        in_specs=[pl.BlockSpec((B, D_MODEL), lambda j: (0, 0)),
                  pl.BlockSpec((D_MODEL, D_MODEL), lambda j: (0, j)),
                  pl.BlockSpec((1, D_MODEL), lambda j: (0, j))],
        out_specs=pl.BlockSpec((B, D_MODEL), lambda j: (0, j)),
        compiler_params=pltpu.CompilerParams(dimension_semantics=("arbitrary",)),
        name="ada",
    )(c, w_ada, b_ada.reshape(1, n))


def _in_kernel(x_ref, mod_ref, g_ref, w_ref, cos_ref, sin_ref, gq_ref, gk_ref,
               qn_ref, kn_ref, qg_ref, ka_ref, kb_ref, vt_ref, *, tm):
    x = x_ref[...]
    ms = jnp.mean(x * x, axis=-1, keepdims=True)
    y = x * lax.rsqrt(ms + EPS) * g_ref[...]
    h = y * (1.0 + mod_ref[1:2, :]) + mod_ref[0:1, :]
    acc = jnp.dot(h.astype(BF16), w_ref[...], preferred_element_type=F32)

    scale = HEAD_DIM ** -0.5
    qn_ref[...] = (acc[:, 0:NA_WIDTH] * scale).astype(BF16)
    kn_ref[...] = acc[:, NA_WIDTH:2 * NA_WIDTH].astype(BF16)

    lane = lax.broadcasted_iota(jnp.int32, (tm, LANES), 1)
    low_head = lane < HEAD_DIM
    first_half = (lane & (HEAD_DIM - 1)) < HEAD_DIM // 2
    cos = cos_ref[...]
    sin = sin_ref[...]

    def norm_rope(xg, gain):
        sq = xg * xg
        s_lo = jnp.sum(jnp.where(low_head, sq, 0.0), axis=-1, keepdims=True)
        s_hi = jnp.sum(jnp.where(low_head, 0.0, sq), axis=-1, keepdims=True)
        inv = lax.rsqrt(jnp.where(low_head, s_lo, s_hi) * (1.0 / HEAD_DIM) + EPS)
        yg = xg * inv * gain
        rot = jnp.where(first_half,
                        pltpu.roll(yg, LANES - HEAD_DIM // 2, 1),
                        pltpu.roll(yg, HEAD_DIM // 2, 1))
        return yg * cos + rot * sin

    q0 = 3 * NA_WIDTH
    for g in range(GQA_WIDTH // LANES):
        xg = acc[:, q0 + g * LANES:q0 + (g + 1) * LANES]
        qg_ref[:, g * LANES:(g + 1) * LANES] = (norm_rope(xg, gq_ref[...]) * scale).astype(BF16)
    k0 = q0 + GQA_WIDTH
    kr = norm_rope(acc[:, k0:k0 + KV_WIDTH], gk_ref[...])
    ka_ref[...] = kr.astype(BF16)
    kb_ref[...] = pltpu.roll(kr, HEAD_DIM, 1).astype(BF16)

    v_cols = [2 * NA_WIDTH + j * LANES for j in range(NA_WIDTH // LANES)] + [k0 + KV_WIDTH]
    for r in range(tm // ROWBLOCK):
        for j, c0 in enumerate(v_cols):
            blk = acc[r * ROWBLOCK:(r + 1) * ROWBLOCK, c0:c0 + LANES]
            vt_ref[r, j * LANES:(j + 1) * LANES, :] = blk.T.astype(BF16)


def _in_proj(x, mod, g_attn, w_in, cos_t, sin_t, gq_t, gk_t, tm):
    B, T, _ = x.shape
    nrb = T // ROWBLOCK
    row = lambda b, i: (b, i, 0)
    const2 = lambda b, i: (0, 0)
    out_shape = (
        jax.ShapeDtypeStruct((B, T, NA_WIDTH), BF16),
        jax.ShapeDtypeStruct((B, T, NA_WIDTH), BF16),
        jax.ShapeDtypeStruct((B, T, GQA_WIDTH), BF16),
        jax.ShapeDtypeStruct((B, T, KV_WIDTH), BF16),
        jax.ShapeDtypeStruct((B, T, KV_WIDTH), BF16),
        jax.ShapeDtypeStruct((B, nrb, V_WIDTH, ROWBLOCK), BF16),
    )
    return pl.pallas_call(
        functools.partial(_in_kernel, tm=tm),
        out_shape=out_shape,
        grid=(B, T // tm),
        in_specs=[
            pl.BlockSpec((None, tm, D_MODEL), row),
            pl.BlockSpec((None, N_MOD, D_MODEL), lambda b, i: (b, 0, 0)),
            pl.BlockSpec((1, D_MODEL), const2),
            pl.BlockSpec((D_MODEL, IN_WIDTH), const2, pipeline_mode=pl.Buffered(1)),
            pl.BlockSpec((tm, LANES), lambda b, i: (i, 0)),
            pl.BlockSpec((tm, LANES), lambda b, i: (i, 0)),
            pl.BlockSpec((1, LANES), const2),
            pl.BlockSpec((1, LANES), const2),
        ],
        out_specs=(
            pl.BlockSpec((None, tm, NA_WIDTH), row),
            pl.BlockSpec((None, tm, NA_WIDTH), row),
            pl.BlockSpec((None, tm, GQA_WIDTH), row),
            pl.BlockSpec((None, tm, KV_WIDTH), row),
            pl.BlockSpec((None, tm, KV_WIDTH), row),
            pl.BlockSpec((None, tm // ROWBLOCK, V_WIDTH, ROWBLOCK), lambda b, i: (b, i, 0, 0)),
        ),
        compiler_params=pltpu.CompilerParams(
            dimension_semantics=("arbitrary", "arbitrary"), vmem_limit_bytes=VMEM_LIMIT),
        name="in_proj",
    )(x, mod, g_attn, w_in, cos_t, sin_t, gq_t, gk_t)


def _na_variant(variant, i, qr):
    if variant == 0:
        valid, di = i < NA_WIN_H, i - qr
    elif variant == 1:
        valid, di = 0 <= i - qr < NA_WIN_H, i - qr - ROWBLOCK_ROWS
    else:
        valid, di = ROWBLOCK_ROWS <= i < ROWBLOCK_ROWS + NA_WIN_H, i - qr - 2 * ROWBLOCK_ROWS
    return di + NA_WIN_H - 1 if valid else None


def _na_kernel(q_ref, k_ref, vt_ref, c2_ref, o_ref, tbl_ref, *, n_rb):
    lane = lax.broadcasted_iota(jnp.int32, (GRID_W, LANES), 1)
    left = lane < GRID_W

    @pl.when(pl.program_id(1) == 0)
    def _build_bias():
        for variant in range(3):
            for hh in range(2):
                for i in range(BAND_BLOCKS * ROWBLOCK_ROWS):
                    for qp in range(ROWBLOCK_ROWS // 2):
                        ro = [_na_variant(variant, i, 2 * qp + e) for e in range(2)]
                        t = [jnp.full((GRID_W, LANES), MASKED, F32) if r is None
                             else c2_ref[hh, r] for r in ro]
                        tbl_ref[variant, hh, i * GRID_W:(i + 1) * GRID_W,
                                qp * LANES:(qp + 1) * LANES] = jnp.where(left, t[0], t[1])

    qlane = lax.broadcasted_iota(jnp.int32, (ROWBLOCK, LANES), 1)

    def body(rb, carry):
        bs = jnp.clip(rb - 1, 0, n_rb - BAND_BLOCKS)
        variant = jnp.where(rb == 0, 0, jnp.where(rb == n_rb - 1, 2, 1))
        q = q_ref[pl.ds(pl.multiple_of(rb * ROWBLOCK, ROWBLOCK), ROWBLOCK), :]
        kband = k_ref[pl.ds(pl.multiple_of(bs * ROWBLOCK, ROWBLOCK), BAND), :]
        outs = []
        for hh in range(2):
            in_head = (qlane < HEAD_DIM) if hh == 0 else (qlane >= HEAD_DIM)
            qh = jnp.where(in_head, q, jnp.zeros_like(q))
            s = lax.dot_general(kband, qh, NT_DIMS, preferred_element_type=F32)
            s = s + tbl_ref[variant, hh]
            m = jnp.max(s, axis=0, keepdims=True)
            p = jnp.exp(s - m)
            l = jnp.sum(p, axis=0, keepdims=True)
            pb = p.astype(BF16)
            o_t = jnp.zeros((HEAD_DIM, ROWBLOCK), F32)
            for j in range(BAND_BLOCKS):
                vt = vt_ref[bs + j, hh * HEAD_DIM:(hh + 1) * HEAD_DIM, :]
                o_t = o_t + jnp.dot(vt, pb[j * ROWBLOCK:(j + 1) * ROWBLOCK, :],
                                    preferred_element_type=F32)
            outs.append(o_t / l)
        o_pair = jnp.concatenate(outs, axis=0)
        o_ref[pl.ds(pl.multiple_of(rb * ROWBLOCK, ROWBLOCK), ROWBLOCK), :] = o_pair.T.astype(BF16)
        return carry

    lax.fori_loop(0, n_rb, body, 0)


def _na_attention(qn, kn, vt, c2):
    B, T, _ = qn.shape
    n_rb = T // ROWBLOCK
    n_pairs = NA_HEADS // 2
    col = lambda p, b: (b, 0, p)
    return pl.pallas_call(
        functools.partial(_na_kernel, n_rb=n_rb),
        out_shape=jax.ShapeDtypeStruct((B, T, NA_WIDTH), BF16),
        grid=(n_pairs, B),
        in_specs=[
            pl.BlockSpec((None, T, LANES), col),
            pl.BlockSpec((None, T, LANES), col),
            pl.BlockSpec((None, n_rb, LANES, ROWBLOCK), lambda p, b: (b, 0, p, 0)),
            pl.BlockSpec((2, 2 * NA_WIN_H - 1, GRID_W, LANES), lambda p, b: (p, 0, 0, 0)),
        ],
        out_specs=pl.BlockSpec((None, T, LANES), col),
        scratch_shapes=[pltpu.VMEM((3, 2, BAND, ROWBLOCK), F32)],
        compiler_params=pltpu.CompilerParams(
            dimension_semantics=("arbitrary", "arbitrary"), vmem_limit_bytes=VMEM_LIMIT),
        name="na_attn",
    )(qn, kn, vt, c2)


def _gqa_kernel(q_ref, ka_ref, kb_ref, vt_ref, o_ref, qm_ref, m_ref, l_ref, acc_ref, *, tq, n_chunks):
    qlane = lax.broadcasted_iota(jnp.int32, (tq, LANES), 1)
    for h in range(GQA_HEADS):
        g, half = h // 2, h % 2
        q2 = q_ref[:, g * LANES:(g + 1) * LANES]
        in_head = (qlane < HEAD_DIM) if half == 0 else (qlane >= HEAD_DIM)
        qm_ref[h] = jnp.where(in_head, q2, jnp.zeros_like(q2))
    m_ref[...] = jnp.full(m_ref.shape, MASKED, F32)
    l_ref[...] = jnp.zeros(l_ref.shape, F32)
    acc_ref[...] = jnp.zeros(acc_ref.shape, F32)

    def chunk(c, carry):
        rows = pl.ds(pl.multiple_of(c * ROWBLOCK, ROWBLOCK), ROWBLOCK)
        ka = ka_ref[rows, :]
        kb = kb_ref[rows, :]
        for h in range(GQA_HEADS):
            kvh, half = h // GQA_GROUP, h % 2
            kk = ka if half == kvh else kb
            s = lax.dot_general(kk, qm_ref[h], NT_DIMS, preferred_element_type=F32)
            m_old = m_ref[h]
            m_new = jnp.maximum(m_old, jnp.max(s, axis=0, keepdims=True))
            alpha = jnp.exp(m_old - m_new)
            p = jnp.exp(s - m_new)
            l_ref[h] = alpha * l_ref[h] + jnp.sum(p, axis=0, keepdims=True)
            vt = vt_ref[c, kvh * HEAD_DIM:(kvh + 1) * HEAD_DIM, :]
            acc_ref[h] = alpha * acc_ref[h] + jnp.dot(vt, p.astype(BF16),
                                                      preferred_element_type=F32)
            m_ref[h] = m_new
        return carry

    lax.fori_loop(0, n_chunks, chunk, 0)

    for g in range(GQA_HEADS // 2):
        o2 = jnp.concatenate([acc_ref[2 * g + e] / l_ref[2 * g + e] for e in range(2)], axis=0)
        o_ref[:, g * LANES:(g + 1) * LANES] = o2.T.astype(BF16)


def _gqa_attention(qg, ka, kb, vt, tq):
    B, T, _ = qg.shape
    n_chunks = T // ROWBLOCK
    kv_block = NA_WIDTH // KV_WIDTH
    return pl.pallas_call(
        functools.partial(_gqa_kernel, tq=tq, n_chunks=n_chunks),
        out_shape=jax.ShapeDtypeStruct((B, T, GQA_WIDTH), BF16),
        grid=(B, T // tq),
        in_specs=[
            pl.BlockSpec((None, tq, GQA_WIDTH), lambda b, i: (b, i, 0)),
            pl.BlockSpec((None, T, KV_WIDTH), lambda b, i: (b, 0, 0)),
            pl.BlockSpec((None, T, KV_WIDTH), lambda b, i: (b, 0, 0)),
            pl.BlockSpec((None, n_chunks, KV_WIDTH, ROWBLOCK), lambda b, i: (b, 0, kv_block, 0)),
        ],
        out_specs=pl.BlockSpec((None, tq, GQA_WIDTH), lambda b, i: (b, i, 0)),
        scratch_shapes=[
            pltpu.VMEM((GQA_HEADS, tq, LANES), BF16),
            pltpu.VMEM((GQA_HEADS, 1, tq), F32),
            pltpu.VMEM((GQA_HEADS, 1, tq), F32),
            pltpu.VMEM((GQA_HEADS, HEAD_DIM, tq), F32),
        ],
        compiler_params=pltpu.CompilerParams(
            dimension_semantics=("arbitrary", "arbitrary"), vmem_limit_bytes=VMEM_LIMIT),
        name="gqa_attn",
    )(qg, ka, kb, vt)


def _post_kernel(x_ref, ona_ref, og_ref, mod_ref, wo_ref, gf_ref, wg_ref, wu_ref, wd_ref,
                 gfin_ref, o_ref):
    attn = (jnp.dot(ona_ref[...], wo_ref[0:NA_WIDTH, :], preferred_element_type=F32)
            + jnp.dot(og_ref[...], wo_ref[NA_WIDTH:, :], preferred_element_type=F32))
    x1 = x_ref[...] + mod_ref[2:3, :] * attn
    ms = jnp.mean(x1 * x1, axis=-1, keepdims=True)
    h = (x1 * lax.rsqrt(ms + EPS) * gf_ref[...]) * (1.0 + mod_ref[4:5, :]) + mod_ref[3:4, :]
    hb = h.astype(BF16)
    gate = jnp.dot(hb, wg_ref[...], preferred_element_type=F32)
    up = jnp.dot(hb, wu_ref[...], preferred_element_type=F32)
    ff = (_silu(gate) * up).astype(BF16)
    x2 = x1 + mod_ref[5:6, :] * jnp.dot(ff, wd_ref[...], preferred_element_type=F32)
    ms2 = jnp.mean(x2 * x2, axis=-1, keepdims=True)
    o_ref[...] = x2 * lax.rsqrt(ms2 + EPS) * gfin_ref[...]


def _post(x, o_na, o_g, mod, w_o, g_ffn, w_gate, w_up, w_down, g_final, tm):
    B, T, _ = x.shape
    d_ff = w_gate.shape[1]
    row = lambda b, i: (b, i, 0)
    const2 = lambda b, i: (0, 0)
    resident = pl.Buffered(1)
    return pl.pallas_call(
        _post_kernel,
        out_shape=jax.ShapeDtypeStruct((B, T, D_MODEL), F32),
        grid=(B, T // tm),
        in_specs=[
            pl.BlockSpec((None, tm, D_MODEL), row),
            pl.BlockSpec((None, tm, NA_WIDTH), row),
            pl.BlockSpec((None, tm, GQA_WIDTH), row),
            pl.BlockSpec((None, N_MOD, D_MODEL), lambda b, i: (b, 0, 0)),
            pl.BlockSpec((D_MODEL, D_MODEL), const2, pipeline_mode=resident),
            pl.BlockSpec((1, D_MODEL), const2),
            pl.BlockSpec((D_MODEL, d_ff), const2, pipeline_mode=resident),
            pl.BlockSpec((D_MODEL, d_ff), const2, pipeline_mode=resident),
            pl.BlockSpec((d_ff, D_MODEL), const2, pipeline_mode=resident),
            pl.BlockSpec((1, D_MODEL), const2),
        ],
        out_specs=pl.BlockSpec((None, tm, D_MODEL), row),
        compiler_params=pltpu.CompilerParams(
            dimension_semantics=("arbitrary", "arbitrary"), vmem_limit_bytes=VMEM_LIMIT),
        name="post",
    )(x, o_na, o_g, mod, w_o, g_ffn, w_gate, w_up, w_down, g_final)


def _rope_tables(n_tokens):
    t = jnp.arange(n_tokens)
    row = (t // GRID_W).astype(F32)
    col = (t % GRID_W).astype(F32)

    def angles(pos, dims):
        inv = ROPE_THETA ** (-jnp.arange(0, dims, 2, dtype=F32) / dims)
        return pos[:, None] * inv[None, :]

    ang = jnp.concatenate([angles(row, HEAD_DIM // 2), angles(col, HEAD_DIM // 2)], axis=-1)
    cos, sin = jnp.cos(ang), jnp.sin(ang)
    reps = LANES // HEAD_DIM
    return (jnp.tile(jnp.concatenate([cos, cos], axis=-1), (1, reps)),
            jnp.tile(jnp.concatenate([-sin, sin], axis=-1), (1, reps)))


def _na_bias_columns(rpb_l):
    kc = jnp.arange(GRID_W)[:, None]
    qc = jnp.arange(GRID_W)[None, :]
    cs = jnp.clip(qc - NA_WIN_W // 2, 0, GRID_W - NA_WIN_W)
    valid = (kc >= cs) & (kc < cs + NA_WIN_W)
    off = jnp.clip(kc - qc + NA_WIN_W - 1, 0, 2 * NA_WIN_W - 2)
    c = jnp.where(valid, rpb_l[:, :, off], MASKED)
    return jnp.concatenate([c, c], axis=-1).astype(F32)


def kernel(x, c, w_ada, b_ada, g_attn, w_in, g_q, g_k, rpb, w_o, g_ffn, w_gate, w_up, w_down, g_final):
    B, T, _ = x.shape
    assert w_ada.shape[0] == 1, "single-layer block: the final norm is fused into the layer"
    cos_t, sin_t = _rope_tables(T)
    reps = LANES // HEAD_DIM
    mod = _ada(c, w_ada[0], b_ada[0]).reshape(B, N_MOD, D_MODEL)
    qn, kn, qg, ka, kb, vt = _in_proj(
        x, mod, g_attn[0].reshape(1, D_MODEL), w_in[0].astype(BF16), cos_t, sin_t,
        jnp.tile(g_q[0], reps).reshape(1, LANES), jnp.tile(g_k[0], reps).reshape(1, LANES),
        tm=256)
    o_na = _na_attention(qn, kn, vt, _na_bias_columns(rpb[0]))
    o_g = _gqa_attention(qg, ka, kb, vt, tq=256)
    return _post(x, o_na, o_g, mod, w_o[0].astype(BF16), g_ffn[0].reshape(1, D_MODEL),
                 w_gate[0].astype(BF16), w_up[0].astype(BF16), w_down[0].astype(BF16),
                 g_final.reshape(1, D_MODEL), tm=256)
```

```python
import functools

import jax
import jax.numpy as jnp
from jax import lax
from jax.experimental import pallas as pl
from jax.experimental.pallas import tpu as pltpu

F32 = jnp.float32
BF16 = jnp.bfloat16

D_MODEL = 1024
HEAD_DIM = 64
NA_HEADS = 8
GQA_HEADS = 8
GQA_KV_HEADS = 2
GQA_GROUP = GQA_HEADS // GQA_KV_HEADS
NA_WIDTH = NA_HEADS * HEAD_DIM
GQA_WIDTH = GQA_HEADS * HEAD_DIM
KV_WIDTH = GQA_KV_HEADS * HEAD_DIM
IN_WIDTH = 3 * NA_WIDTH + GQA_WIDTH + 2 * KV_WIDTH
V_WIDTH = NA_WIDTH + KV_WIDTH
GRID_W = 64
NA_WIN_H = 8
NA_WIN_W = 16
ROPE_THETA = 10000.0
N_MOD = 6
EPS = 1e-6

LANES = 128
ROWBLOCK_ROWS = 4
ROWBLOCK = ROWBLOCK_ROWS * GRID_W
BAND_BLOCKS = 3
BAND = BAND_BLOCKS * ROWBLOCK
MASKED = -1e30
LOG2E = 1.4426950408889634
VMEM_LIMIT = 56 * 1024 * 1024

NT_DIMS = (((1,), (1,)), ((), ()))


def _silu(x):
    return x / (1.0 + jnp.exp(-x))


def _ada_kernel(c_ref, w_ref, b_ref, o_ref):
    ca = _silu(c_ref[...])
    o_ref[...] = jnp.dot(ca.astype(BF16), w_ref[...].astype(BF16),
                         preferred_element_type=F32) + b_ref[...]


def _ada(c, w_ada, b_ada):
    B = c.shape[0]
    n = w_ada.shape[1]
    return pl.pallas_call(
        _ada_kernel,
        out_shape=jax.ShapeDtypeStruct((B, n), F32),
        grid=(n // D_MODEL,),
        in_specs=[pl.BlockSpec((B, D_MODEL), lambda j: (0, 0)),
                  pl.BlockSpec((D_MODEL, D_MODEL), lambda j: (0, j)),
                  pl.BlockSpec((1, D_MODEL), lambda j: (0, j))],
        out_specs=pl.BlockSpec((B, D_MODEL), lambda j: (0, j)),
        compiler_params=pltpu.CompilerParams(dimension_semantics=("arbitrary",)),
        name="ada",
    )(c, w_ada, b_ada.reshape(1, n))


def _in_kernel(x_ref, mod_ref, g_ref, w_ref, cos_ref, sin_ref, gq_ref, gk_ref,
               qn_ref, kn_ref, qg_ref, ka_ref, kb_ref, vt_ref, *, tm):
    x = x_ref[...]
    ms = jnp.mean(x * x, axis=-1, keepdims=True)
    y = x * lax.rsqrt(ms + EPS) * g_ref[...]
    h = y * (1.0 + mod_ref[1:2, :]) + mod_ref[0:1, :]
    acc = jnp.dot(h.astype(BF16), w_ref[...], preferred_element_type=F32)

    scale = HEAD_DIM ** -0.5
    qn_ref[...] = (acc[:, 0:NA_WIDTH] * (scale * LOG2E)).astype(BF16)
    kn_ref[...] = acc[:, NA_WIDTH:2 * NA_WIDTH].astype(BF16)

    lane = lax.broadcasted_iota(jnp.int32, (tm, LANES), 1)
    low_head = lane < HEAD_DIM
    first_half = (lane & (HEAD_DIM - 1)) < HEAD_DIM // 2
    cos = cos_ref[...]
    sin = sin_ref[...]

    def norm_rope(xg, gain):
        sq = xg * xg
        s_lo = jnp.sum(jnp.where(low_head, sq, 0.0), axis=-1, keepdims=True)
        s_hi = jnp.sum(jnp.where(low_head, 0.0, sq), axis=-1, keepdims=True)
        inv = lax.rsqrt(jnp.where(low_head, s_lo, s_hi) * (1.0 / HEAD_DIM) + EPS)
        yg = xg * inv * gain
        rot = jnp.where(first_half,
                        pltpu.roll(yg, LANES - HEAD_DIM // 2, 1),
                        pltpu.roll(yg, HEAD_DIM // 2, 1))
        return yg * cos + rot * sin

    q0 = 3 * NA_WIDTH
    for g in range(GQA_WIDTH // LANES):
        xg = acc[:, q0 + g * LANES:q0 + (g + 1) * LANES]
        qg_ref[:, g * LANES:(g + 1) * LANES] = (norm_rope(xg, gq_ref[...]) * (scale * LOG2E)).astype(BF16)
    k0 = q0 + GQA_WIDTH
    kr = norm_rope(acc[:, k0:k0 + KV_WIDTH], gk_ref[...])
    ka_ref[...] = kr.astype(BF16)
    kb_ref[...] = pltpu.roll(kr, HEAD_DIM, 1).astype(BF16)

    v_cols = [2 * NA_WIDTH + j * LANES for j in range(NA_WIDTH // LANES)] + [k0 + KV_WIDTH]
    for r in range(tm // ROWBLOCK):
        for j, c0 in enumerate(v_cols):
            blk = acc[r * ROWBLOCK:(r + 1) * ROWBLOCK, c0:c0 + LANES]
            vt_ref[r, j * LANES:(j + 1) * LANES, :] = blk.T.astype(BF16)


def _in_proj(x, mod, g_attn, w_in, cos_t, sin_t, gq_t, gk_t, tm):
    B, T, _ = x.shape
    nrb = T // ROWBLOCK
    row = lambda b, i: (b, i, 0)
    const2 = lambda b, i: (0, 0)
    out_shape = (
        jax.ShapeDtypeStruct((B, T, NA_WIDTH), BF16),
        jax.ShapeDtypeStruct((B, T, NA_WIDTH), BF16),
        jax.ShapeDtypeStruct((B, T, GQA_WIDTH), BF16),
        jax.ShapeDtypeStruct((B, T, KV_WIDTH), BF16),
        jax.ShapeDtypeStruct((B, T, KV_WIDTH), BF16),
        jax.ShapeDtypeStruct((B, nrb, V_WIDTH, ROWBLOCK), BF16),
    )
    return pl.pallas_call(
        functools.partial(_in_kernel, tm=tm),
        out_shape=out_shape,
        grid=(B, T // tm),
        in_specs=[
            pl.BlockSpec((None, tm, D_MODEL), row),
            pl.BlockSpec((None, N_MOD, D_MODEL), lambda b, i: (b, 0, 0)),
            pl.BlockSpec((1, D_MODEL), const2),
            pl.BlockSpec((D_MODEL, IN_WIDTH), const2, pipeline_mode=pl.Buffered(1)),
            pl.BlockSpec((tm, LANES), lambda b, i: (i, 0)),
            pl.BlockSpec((tm, LANES), lambda b, i: (i, 0)),
            pl.BlockSpec((1, LANES), const2),
            pl.BlockSpec((1, LANES), const2),
        ],
        out_specs=(
            pl.BlockSpec((None, tm, NA_WIDTH), row),
            pl.BlockSpec((None, tm, NA_WIDTH), row),
            pl.BlockSpec((None, tm, GQA_WIDTH), row),
            pl.BlockSpec((None, tm, KV_WIDTH), row),
            pl.BlockSpec((None, tm, KV_WIDTH), row),
            pl.BlockSpec((None, tm // ROWBLOCK, V_WIDTH, ROWBLOCK), lambda b, i: (b, i, 0, 0)),
        ),
        compiler_params=pltpu.CompilerParams(
            dimension_semantics=("arbitrary", "arbitrary"), vmem_limit_bytes=VMEM_LIMIT),
        name="in_proj",
    )(x, mod, g_attn, w_in, cos_t, sin_t, gq_t, gk_t)


def _na_variant(variant, i, qr):
    if variant == 0:
        valid, di = i < NA_WIN_H, i - qr
    elif variant == 1:
        valid, di = 0 <= i - qr < NA_WIN_H, i - qr - ROWBLOCK_ROWS
    else:
        valid, di = ROWBLOCK_ROWS <= i < ROWBLOCK_ROWS + NA_WIN_H, i - qr - 2 * ROWBLOCK_ROWS
    return di + NA_WIN_H - 1 if valid else None


def _na_kernel(q_ref, k_ref, vt_ref, c2_ref, o_ref, tbl_ref, s0_ref, s1_ref, *, n_rb):
    lane = lax.broadcasted_iota(jnp.int32, (GRID_W, LANES), 1)
    left = lane < GRID_W

    @pl.when(pl.program_id(1) == 0)
    def _build_bias():
        for variant in range(3):
            for hh in range(2):
                for i in range(BAND_BLOCKS * ROWBLOCK_ROWS):
                    for qp in range(ROWBLOCK_ROWS // 2):
                        ro = [_na_variant(variant, i, 2 * qp + e) for e in range(2)]
                        t = [jnp.full((GRID_W, LANES), MASKED, F32) if r is None
                             else c2_ref[hh, r] for r in ro]
                        tbl_ref[variant, hh, i * GRID_W:(i + 1) * GRID_W,
                                qp * LANES:(qp + 1) * LANES] = jnp.where(left, t[0], t[1]) * LOG2E

    qlane = lax.broadcasted_iota(jnp.int32, (ROWBLOCK, LANES), 1)
    s_refs = (s0_ref, s1_ref)

    def band_start(rb):
        return jnp.clip(rb - 1, 0, n_rb - BAND_BLOCKS)

    def scores(rb, slot):
        q = q_ref[pl.ds(pl.multiple_of(rb * ROWBLOCK, ROWBLOCK), ROWBLOCK), :]
        kband = k_ref[pl.ds(pl.multiple_of(band_start(rb) * ROWBLOCK, ROWBLOCK), BAND), :]
        for hh in range(2):
            in_head = (qlane < HEAD_DIM) if hh == 0 else (qlane >= HEAD_DIM)
            qh = jnp.where(in_head, q, jnp.zeros_like(q))
            s_refs[slot][hh] = lax.dot_general(kband, qh, NT_DIMS,
                                               preferred_element_type=F32)

    def softmax_pv(rb, slot):
        bs = band_start(rb)
        variant = jnp.where(rb == 0, 0, jnp.where(rb == n_rb - 1, 2, 1))
        outs = []
        for hh in range(2):
            s = s_refs[slot][hh] + tbl_ref[variant, hh]
            m = jnp.max(s, axis=0, keepdims=True)
            p = jnp.exp2(s - m)
            l = jnp.sum(p, axis=0, keepdims=True)
            pb = p.astype(BF16)
            o_t = jnp.zeros((HEAD_DIM, ROWBLOCK), F32)
            for j in range(BAND_BLOCKS):
                vt = vt_ref[bs + j, hh * HEAD_DIM:(hh + 1) * HEAD_DIM, :]
                o_t = o_t + jnp.dot(vt, pb[j * ROWBLOCK:(j + 1) * ROWBLOCK, :],
                                    preferred_element_type=F32)
            outs.append(o_t / l)
        o_pair = jnp.concatenate(outs, axis=0)
        o_ref[pl.ds(pl.multiple_of(rb * ROWBLOCK, ROWBLOCK), ROWBLOCK), :] = o_pair.T.astype(BF16)

    def step(rb, slot, prefetch):
        if prefetch:
            scores(rb + 1, 1 - slot)
        softmax_pv(rb, slot)

    scores(0, 0)

    def block_pair(j, carry):
        step(2 * j, 0, True)
        step(2 * j + 1, 1, True)
        return carry

    assert n_rb % 2 == 0
    lax.fori_loop(0, n_rb // 2 - 1, block_pair, 0)
    step(n_rb - 2, 0, True)
    step(n_rb - 1, 1, False)


def _na_attention(qn, kn, vt, c2):
    B, T, _ = qn.shape
    n_rb = T // ROWBLOCK
    n_pairs = NA_HEADS // 2
    col = lambda p, b: (b, 0, p)
    return pl.pallas_call(
        functools.partial(_na_kernel, n_rb=n_rb),
        out_shape=jax.ShapeDtypeStruct((B, T, NA_WIDTH), BF16),
        grid=(n_pairs, B),
        in_specs=[
            pl.BlockSpec((None, T, LANES), col),
            pl.BlockSpec((None, T, LANES), col),
            pl.BlockSpec((None, n_rb, LANES, ROWBLOCK), lambda p, b: (b, 0, p, 0)),
            pl.BlockSpec((2, 2 * NA_WIN_H - 1, GRID_W, LANES), lambda p, b: (p, 0, 0, 0)),
        ],
        out_specs=pl.BlockSpec((None, T, LANES), col),
        scratch_shapes=[pltpu.VMEM((3, 2, BAND, ROWBLOCK), F32),
                        pltpu.VMEM((2, BAND, ROWBLOCK), F32),
                        pltpu.VMEM((2, BAND, ROWBLOCK), F32)],
        compiler_params=pltpu.CompilerParams(
            dimension_semantics=("arbitrary", "arbitrary"), vmem_limit_bytes=VMEM_LIMIT),
        name="na_attn",
    )(qn, kn, vt, c2)


def _gqa_kernel(q_ref, ka_ref, kb_ref, vt_ref, o_ref, qm_ref, s0_ref, s1_ref, m_ref, l_ref,
                acc_ref, *, tq, n_chunks):
    qlane = lax.broadcasted_iota(jnp.int32, (tq, LANES), 1)
    for h in range(GQA_HEADS):
        g, half = h // 2, h % 2
        q2 = q_ref[:, g * LANES:(g + 1) * LANES]
        in_head = (qlane < HEAD_DIM) if half == 0 else (qlane >= HEAD_DIM)
        qm_ref[h] = jnp.where(in_head, q2, jnp.zeros_like(q2))
    m_ref[...] = jnp.full(m_ref.shape, MASKED, F32)
    l_ref[...] = jnp.zeros(l_ref.shape, F32)
    acc_ref[...] = jnp.zeros(acc_ref.shape, F32)

    s_refs = (s0_ref, s1_ref)

    def scores(h, c, slot):
        kvh, half = h // GQA_GROUP, h % 2
        k_ref = ka_ref if half == kvh else kb_ref
        rows = pl.ds(pl.multiple_of(c * ROWBLOCK, ROWBLOCK), ROWBLOCK)
        s_refs[slot][h] = lax.dot_general(k_ref[rows, :], qm_ref[h], NT_DIMS,
                                          preferred_element_type=F32)

    def softmax_pv(h, c, slot):
        kvh = h // GQA_GROUP
        s = s_refs[slot][h]
        m_old = m_ref[h]
        m_new = jnp.maximum(m_old, jnp.max(s, axis=0, keepdims=True))
        alpha = jnp.exp2(m_old - m_new)
        p = jnp.exp2(s - m_new)
        l_ref[h] = alpha * l_ref[h] + jnp.sum(p, axis=0, keepdims=True)
        vt = vt_ref[c, kvh * HEAD_DIM:(kvh + 1) * HEAD_DIM, :]
        acc_ref[h] = alpha * acc_ref[h] + jnp.dot(vt, p.astype(BF16),
                                                  preferred_element_type=F32)
        m_ref[h] = m_new

    def step(c, slot, prefetch):
        if prefetch:
            scores(0, c + 1, 1 - slot)
        for h in range(GQA_HEADS):
            if prefetch and h + 1 < GQA_HEADS:
                scores(h + 1, c + 1, 1 - slot)
            softmax_pv(h, c, slot)

    for h in range(GQA_HEADS):
        scores(h, 0, 0)

    def chunk_pair(j, carry):
        step(2 * j, 0, True)
        step(2 * j + 1, 1, True)
        return carry

    assert n_chunks % 2 == 0
    lax.fori_loop(0, n_chunks // 2 - 1, chunk_pair, 0)
    step(n_chunks - 2, 0, True)
    step(n_chunks - 1, 1, False)

    for g in range(GQA_HEADS // 2):
        o2 = jnp.concatenate([acc_ref[2 * g + e] / l_ref[2 * g + e] for e in range(2)], axis=0)
        o_ref[:, g * LANES:(g + 1) * LANES] = o2.T.astype(BF16)


def _gqa_attention(qg, ka, kb, vt, tq):
    B, T, _ = qg.shape
    n_chunks = T // ROWBLOCK
    kv_block = NA_WIDTH // KV_WIDTH
    return pl.pallas_call(
        functools.partial(_gqa_kernel, tq=tq, n_chunks=n_chunks),
        out_shape=jax.ShapeDtypeStruct((B, T, GQA_WIDTH), BF16),
        grid=(B, T // tq),
        in_specs=[
            pl.BlockSpec((None, tq, GQA_WIDTH), lambda b, i: (b, i, 0)),
            pl.BlockSpec((None, T, KV_WIDTH), lambda b, i: (b, 0, 0)),
            pl.BlockSpec((None, T, KV_WIDTH), lambda b, i: (b, 0, 0)),
            pl.BlockSpec((None, n_chunks, KV_WIDTH, ROWBLOCK), lambda b, i: (b, 0, kv_block, 0)),
        ],
        out_specs=pl.BlockSpec((None, tq, GQA_WIDTH), lambda b, i: (b, i, 0)),
        scratch_shapes=[
            pltpu.VMEM((GQA_HEADS, tq, LANES), BF16),
            pltpu.VMEM((GQA_HEADS, ROWBLOCK, tq), F32),
            pltpu.VMEM((GQA_HEADS, ROWBLOCK, tq), F32),
            pltpu.VMEM((GQA_HEADS, 1, tq), F32),
            pltpu.VMEM((GQA_HEADS, 1, tq), F32),
            pltpu.VMEM((GQA_HEADS, HEAD_DIM, tq), F32),
        ],
        compiler_params=pltpu.CompilerParams(
            dimension_semantics=("arbitrary", "arbitrary"), vmem_limit_bytes=VMEM_LIMIT),
        name="gqa_attn",
    )(qg, ka, kb, vt)


def _post_kernel(x_ref, ona_ref, og_ref, mod_ref, wo_ref, gf_ref, wg_ref, wu_ref, wd_ref,
                 gfin_ref, o_ref):
    attn = (jnp.dot(ona_ref[...], wo_ref[0:NA_WIDTH, :], preferred_element_type=F32)
            + jnp.dot(og_ref[...], wo_ref[NA_WIDTH:, :], preferred_element_type=F32))
    x1 = x_ref[...] + mod_ref[2:3, :] * attn
    ms = jnp.mean(x1 * x1, axis=-1, keepdims=True)
    h = (x1 * lax.rsqrt(ms + EPS) * gf_ref[...]) * (1.0 + mod_ref[4:5, :]) + mod_ref[3:4, :]
    hb = h.astype(BF16)
    gate = jnp.dot(hb, wg_ref[...], preferred_element_type=F32)
    up = jnp.dot(hb, wu_ref[...], preferred_element_type=F32)
    ff = (_silu(gate) * up).astype(BF16)
    x2 = x1 + mod_ref[5:6, :] * jnp.dot(ff, wd_ref[...], preferred_element_type=F32)
    ms2 = jnp.mean(x2 * x2, axis=-1, keepdims=True)
    o_ref[...] = x2 * lax.rsqrt(ms2 + EPS) * gfin_ref[...]


def _post(x, o_na, o_g, mod, w_o, g_ffn, w_gate, w_up, w_down, g_final, tm):
    B, T, _ = x.shape
    d_ff = w_gate.shape[1]
    row = lambda b, i: (b, i, 0)
    const2 = lambda b, i: (0, 0)
    resident = pl.Buffered(1)
    return pl.pallas_call(
        _post_kernel,
        out_shape=jax.ShapeDtypeStruct((B, T, D_MODEL), F32),
        grid=(B, T // tm),
        in_specs=[
            pl.BlockSpec((None, tm, D_MODEL), row),
            pl.BlockSpec((None, tm, NA_WIDTH), row),
            pl.BlockSpec((None, tm, GQA_WIDTH), row),
            pl.BlockSpec((None, N_MOD, D_MODEL), lambda b, i: (b, 0, 0)),
            pl.BlockSpec((D_MODEL, D_MODEL), const2, pipeline_mode=resident),
            pl.BlockSpec((1, D_MODEL), const2),
            pl.BlockSpec((D_MODEL, d_ff), const2, pipeline_mode=resident),
            pl.BlockSpec((D_MODEL, d_ff), const2, pipeline_mode=resident),
            pl.BlockSpec((d_ff, D_MODEL), const2, pipeline_mode=resident),
            pl.BlockSpec((1, D_MODEL), const2),
        ],
        out_specs=pl.BlockSpec((None, tm, D_MODEL), row),
        compiler_params=pltpu.CompilerParams(
            dimension_semantics=("arbitrary", "arbitrary"), vmem_limit_bytes=VMEM_LIMIT),
        name="post",
    )(x, o_na, o_g, mod, w_o, g_ffn, w_gate, w_up, w_down, g_final)


def _rope_tables(n_tokens):
    t = jnp.arange(n_tokens)
    row = (t // GRID_W).astype(F32)
    col = (t % GRID_W).astype(F32)

    def angles(pos, dims):
        inv = ROPE_THETA ** (-jnp.arange(0, dims, 2, dtype=F32) / dims)
        return pos[:, None] * inv[None, :]

    ang = jnp.concatenate([angles(row, HEAD_DIM // 2), angles(col, HEAD_DIM // 2)], axis=-1)
    cos, sin = jnp.cos(ang), jnp.sin(ang)
    reps = LANES // HEAD_DIM
    return (jnp.tile(jnp.concatenate([cos, cos], axis=-1), (1, reps)),
            jnp.tile(jnp.concatenate([-sin, sin], axis=-1), (1, reps)))


def _na_bias_columns(rpb_l):
    kc = jnp.arange(GRID_W)[:, None]
    qc = jnp.arange(GRID_W)[None, :]
    cs = jnp.clip(qc - NA_WIN_W // 2, 0, GRID_W - NA_WIN_W)
    valid = (kc >= cs) & (kc < cs + NA_WIN_W)
    off = jnp.clip(kc - qc + NA_WIN_W - 1, 0, 2 * NA_WIN_W - 2)
    c = jnp.where(valid, rpb_l[:, :, off], MASKED)
    return jnp.concatenate([c, c], axis=-1).astype(F32)


def kernel(x, c, w_ada, b_ada, g_attn, w_in, g_q, g_k, rpb, w_o, g_ffn, w_gate, w_up, w_down, g_final):
    B, T, _ = x.shape
    assert w_ada.shape[0] == 1, "single-layer block: the final norm is fused into the layer"
    cos_t, sin_t = _rope_tables(T)
    reps = LANES // HEAD_DIM
    mod = _ada(c, w_ada[0], b_ada[0]).reshape(B, N_MOD, D_MODEL)
    qn, kn, qg, ka, kb, vt = _in_proj(
        x, mod, g_attn[0].reshape(1, D_MODEL), w_in[0].astype(BF16), cos_t, sin_t,
        jnp.tile(g_q[0], reps).reshape(1, LANES), jnp.tile(g_k[0], reps).reshape(1, LANES),
        tm=256)
    o_na = _na_attention(qn, kn, vt, _na_bias_columns(rpb[0]))
    o_g = _gqa_attention(qg, ka, kb, vt, tq=256)
    return _post(x, o_na, o_g, mod, w_o[0].astype(BF16), g_ffn[0].reshape(1, D_MODEL),
                 w_gate[0].astype(BF16), w_up[0].astype(BF16), w_down[0].astype(BF16),
                 g_final.reshape(1, D_MODEL), tm=256)
```

```python
import functools

import jax
import jax.numpy as jnp
from jax import lax
from jax.experimental import pallas as pl
from jax.experimental.pallas import tpu as pltpu

F32 = jnp.float32
BF16 = jnp.bfloat16

D_MODEL = 1024
HEAD_DIM = 64
NA_HEADS = 8
GQA_HEADS = 8
GQA_KV_HEADS = 2
GQA_GROUP = GQA_HEADS // GQA_KV_HEADS
NA_WIDTH = NA_HEADS * HEAD_DIM
GQA_WIDTH = GQA_HEADS * HEAD_DIM
KV_WIDTH = GQA_KV_HEADS * HEAD_DIM
IN_WIDTH = 3 * NA_WIDTH + GQA_WIDTH + 2 * KV_WIDTH
V_WIDTH = NA_WIDTH + KV_WIDTH
GRID_W = 64
NA_WIN_H = 8
NA_WIN_W = 16
ROPE_THETA = 10000.0
N_MOD = 6
EPS = 1e-6

LANES = 128
BF16_SUBLANES = 16
ROWBLOCK_ROWS = 4
ROWBLOCK = ROWBLOCK_ROWS * GRID_W
BAND_BLOCKS = 3
BAND = BAND_BLOCKS * ROWBLOCK
V_AUG = HEAD_DIM + BF16_SUBLANES
MASKED = -1e30
LOG2E = 1.4426950408889634
VMEM_LIMIT = 56 * 1024 * 1024

NT_DIMS = (((1,), (1,)), ((), ()))

COL_QG, COL_KG, COL_VG, COL_VN, COL_QN, COL_KN = 0, 512, 640, 768, 1280, 1792
IN_GROUPS = ((0, COL_VN), (COL_VN, COL_QN), (COL_QN, IN_WIDTH))


def _silu(x):
    return x / (1.0 + jnp.exp(-x))


def _with_ones(vt):
    return jnp.concatenate([vt, jnp.ones((BF16_SUBLANES, vt.shape[1]), vt.dtype)], axis=0)


def _ada_kernel(c_ref, w_ref, b_ref, o_ref):
    ca = _silu(c_ref[...])
    o_ref[...] = jnp.dot(ca.astype(BF16), w_ref[...].astype(BF16),
                         preferred_element_type=F32) + b_ref[...]


def _ada(c, w_ada, b_ada):
    B = c.shape[0]
    n = w_ada.shape[1]
    return pl.pallas_call(
        _ada_kernel,
        out_shape=jax.ShapeDtypeStruct((B, n), F32),
        grid=(n // D_MODEL,),
        in_specs=[pl.BlockSpec((B, D_MODEL), lambda j: (0, 0)),
                  pl.BlockSpec((D_MODEL, D_MODEL), lambda j: (0, j)),
                  pl.BlockSpec((1, D_MODEL), lambda j: (0, j))],
        out_specs=pl.BlockSpec((B, D_MODEL), lambda j: (0, j)),
        compiler_params=pltpu.CompilerParams(dimension_semantics=("arbitrary",)),
        name="ada",
    )(c, w_ada, b_ada.reshape(1, n))


def _in_kernel(x_ref, mod_ref, g_ref, w_ref, cos_ref, sin_ref, gq_ref, gk_ref,
               qn_ref, kn_ref, qg_ref, ka_ref, kb_ref, vt_ref, *, tm):
    scale = HEAD_DIM ** -0.5 * LOG2E
    lane = lax.broadcasted_iota(jnp.int32, (ROWBLOCK, LANES), 1)
    low_head = lane < HEAD_DIM
    first_half = (lane & (HEAD_DIM - 1)) < HEAD_DIM // 2
    shift, gain = mod_ref[0:1, :], 1.0 + mod_ref[1:2, :]

    def hidden(rows):
        x = x_ref[rows, :]
        ms = jnp.mean(x * x, axis=-1, keepdims=True)
        return ((x * lax.rsqrt(ms + EPS) * g_ref[...]) * gain + shift).astype(BF16)

    def norm_rope(xg, head_gain, cos, sin):
        sq = xg * xg
        s_lo = jnp.sum(jnp.where(low_head, sq, 0.0), axis=-1, keepdims=True)
        s_hi = jnp.sum(jnp.where(low_head, 0.0, sq), axis=-1, keepdims=True)
        inv = lax.rsqrt(jnp.where(low_head, s_lo, s_hi) * (1.0 / HEAD_DIM) + EPS)
        yg = xg * inv * head_gain
        rot = jnp.where(first_half,
                        pltpu.roll(yg, LANES - HEAD_DIM // 2, 1),
                        pltpu.roll(yg, HEAD_DIM // 2, 1))
        return yg * cos + rot * sin

    subs = [pl.ds(r * ROWBLOCK, ROWBLOCK) for r in range(tm // ROWBLOCK)]
    hs = [hidden(rows) for rows in subs]
    accs = [[jnp.dot(hb, w_ref[:, c0:c1], preferred_element_type=F32) for c0, c1 in IN_GROUPS]
            for hb in hs]
    for r, rows in enumerate(subs):
        a_gqa, a_vn, a_na = accs[r]
        cos, sin = cos_ref[rows, :], sin_ref[rows, :]
        for g in range(GQA_WIDTH // LANES):
            cols = slice(COL_QG + g * LANES, COL_QG + (g + 1) * LANES)
            qg_ref[rows, g * LANES:(g + 1) * LANES] = (
                norm_rope(a_gqa[:, cols], gq_ref[...], cos, sin) * scale).astype(BF16)
        kr = norm_rope(a_gqa[:, COL_KG:COL_KG + KV_WIDTH], gk_ref[...], cos, sin)
        ka_ref[rows, :] = kr.astype(BF16)
        kb_ref[rows, :] = pltpu.roll(kr, HEAD_DIM, 1).astype(BF16)
        for j in range(NA_WIDTH // LANES):
            vt_ref[r, j * LANES:(j + 1) * LANES, :] = a_vn[:, j * LANES:(j + 1) * LANES].T.astype(BF16)
        vt_ref[r, NA_WIDTH:V_WIDTH, :] = a_gqa[:, COL_VG:COL_VG + KV_WIDTH].T.astype(BF16)
        qn_ref[rows, :] = (a_na[:, 0:NA_WIDTH] * scale).astype(BF16)
        kn_ref[rows, :] = a_na[:, NA_WIDTH:2 * NA_WIDTH].astype(BF16)


def _in_proj(x, mod, g_attn, w_in, cos_t, sin_t, gq_t, gk_t, tm):
    B, T, _ = x.shape
    nrb = T // ROWBLOCK
    row = lambda b, i: (b, i, 0)
    const2 = lambda b, i: (0, 0)
    out_shape = (
        jax.ShapeDtypeStruct((B, T, NA_WIDTH), BF16),
        jax.ShapeDtypeStruct((B, T, NA_WIDTH), BF16),
        jax.ShapeDtypeStruct((B, T, GQA_WIDTH), BF16),
        jax.ShapeDtypeStruct((B, T, KV_WIDTH), BF16),
        jax.ShapeDtypeStruct((B, T, KV_WIDTH), BF16),
        jax.ShapeDtypeStruct((B, nrb, V_WIDTH, ROWBLOCK), BF16),
    )
    return pl.pallas_call(
        functools.partial(_in_kernel, tm=tm),
        out_shape=out_shape,
        grid=(B, T // tm),
        in_specs=[
            pl.BlockSpec((None, tm, D_MODEL), row),
            pl.BlockSpec((None, N_MOD, D_MODEL), lambda b, i: (b, 0, 0)),
            pl.BlockSpec((1, D_MODEL), const2),
            pl.BlockSpec((D_MODEL, IN_WIDTH), const2, pipeline_mode=pl.Buffered(1)),
            pl.BlockSpec((tm, LANES), lambda b, i: (i, 0)),
            pl.BlockSpec((tm, LANES), lambda b, i: (i, 0)),
            pl.BlockSpec((1, LANES), const2),
            pl.BlockSpec((1, LANES), const2),
        ],
        out_specs=(
            pl.BlockSpec((None, tm, NA_WIDTH), row),
            pl.BlockSpec((None, tm, NA_WIDTH), row),
            pl.BlockSpec((None, tm, GQA_WIDTH), row),
            pl.BlockSpec((None, tm, KV_WIDTH), row),
            pl.BlockSpec((None, tm, KV_WIDTH), row),
            pl.BlockSpec((None, tm // ROWBLOCK, V_WIDTH, ROWBLOCK), lambda b, i: (b, i, 0, 0)),
        ),
        compiler_params=pltpu.CompilerParams(
            dimension_semantics=("arbitrary", "arbitrary"), vmem_limit_bytes=VMEM_LIMIT),
        name="in_proj",
    )(x, mod, g_attn, w_in, cos_t, sin_t, gq_t, gk_t)


def _na_variant(variant, i, qr):
    if variant == 0:
        valid, di = i < NA_WIN_H, i - qr
    elif variant == 1:
        valid, di = 0 <= i - qr < NA_WIN_H, i - qr - ROWBLOCK_ROWS
    else:
        valid, di = ROWBLOCK_ROWS <= i < ROWBLOCK_ROWS + NA_WIN_H, i - qr - 2 * ROWBLOCK_ROWS
    return di + NA_WIN_H - 1 if valid else None


def _na_kernel(q_ref, k_ref, vt_ref, c2_ref, o_ref, tbl_ref, s0_ref, s1_ref, *, n_rb):
    lane = lax.broadcasted_iota(jnp.int32, (GRID_W, LANES), 1)
    left = lane < GRID_W

    @pl.when(pl.program_id(1) == 0)
    def _build_bias():
        for variant in range(3):
            for hh in range(2):
                for i in range(BAND_BLOCKS * ROWBLOCK_ROWS):
                    for qp in range(ROWBLOCK_ROWS // 2):
                        ro = [_na_variant(variant, i, 2 * qp + e) for e in range(2)]
                        t = [jnp.full((GRID_W, LANES), MASKED, F32) if r is None
                             else c2_ref[hh, r] for r in ro]
                        tbl_ref[variant, hh, i * GRID_W:(i + 1) * GRID_W,
                                qp * LANES:(qp + 1) * LANES] = jnp.where(left, t[0], t[1]) * LOG2E

    qlane = lax.broadcasted_iota(jnp.int32, (ROWBLOCK, LANES), 1)
    s_refs = (s0_ref, s1_ref)

    def band_start(rb):
        return jnp.clip(rb - 1, 0, n_rb - BAND_BLOCKS)

    def scores(rb, slot):
        q = q_ref[pl.ds(pl.multiple_of(rb * ROWBLOCK, ROWBLOCK), ROWBLOCK), :]
        kband = k_ref[pl.ds(pl.multiple_of(band_start(rb) * ROWBLOCK, ROWBLOCK), BAND), :]
        for hh in range(2):
            in_head = (qlane < HEAD_DIM) if hh == 0 else (qlane >= HEAD_DIM)
            qh = jnp.where(in_head, q, jnp.zeros_like(q))
            s_refs[slot][hh] = lax.dot_general(kband, qh, NT_DIMS,
                                               preferred_element_type=F32)

    def softmax_pv(rb, slot):
        bs = band_start(rb)
        variant = jnp.where(rb == 0, 0, jnp.where(rb == n_rb - 1, 2, 1))
        outs = []
        for hh in range(2):
            s = s_refs[slot][hh] + tbl_ref[variant, hh]
            m = jnp.max(s, axis=0, keepdims=True)
            pb = jnp.exp2(s - m).astype(BF16)
            o_aug = jnp.zeros((V_AUG, ROWBLOCK), F32)
            for j in range(BAND_BLOCKS):
                vt = _with_ones(vt_ref[bs + j, hh * HEAD_DIM:(hh + 1) * HEAD_DIM, :])
                o_aug = o_aug + jnp.dot(vt, pb[j * ROWBLOCK:(j + 1) * ROWBLOCK, :],
                                        preferred_element_type=F32)
            outs.append(o_aug[0:HEAD_DIM] / o_aug[HEAD_DIM:HEAD_DIM + 1])
        o_pair = jnp.concatenate(outs, axis=0)
        o_ref[pl.ds(pl.multiple_of(rb * ROWBLOCK, ROWBLOCK), ROWBLOCK), :] = o_pair.T.astype(BF16)

    def step(rb, slot, prefetch):
        if prefetch:
            scores(rb + 1, 1 - slot)
        softmax_pv(rb, slot)

    scores(0, 0)

    def block_pair(j, carry):
        step(2 * j, 0, True)
        step(2 * j + 1, 1, True)
        return carry

    assert n_rb % 2 == 0
    lax.fori_loop(0, n_rb // 2 - 1, block_pair, 0)
    step(n_rb - 2, 0, True)
    step(n_rb - 1, 1, False)


def _na_attention(qn, kn, vt, c2):
    B, T, _ = qn.shape
    n_rb = T // ROWBLOCK
    n_pairs = NA_HEADS // 2
    col = lambda p, b: (b, 0, p)
    return pl.pallas_call(
        functools.partial(_na_kernel, n_rb=n_rb),
        out_shape=jax.ShapeDtypeStruct((B, T, NA_WIDTH), BF16),
        grid=(n_pairs, B),
        in_specs=[
            pl.BlockSpec((None, T, LANES), col),
            pl.BlockSpec((None, T, LANES), col),
            pl.BlockSpec((None, n_rb, LANES, ROWBLOCK), lambda p, b: (b, 0, p, 0)),
            pl.BlockSpec((2, 2 * NA_WIN_H - 1, GRID_W, LANES), lambda p, b: (p, 0, 0, 0)),
        ],
        out_specs=pl.BlockSpec((None, T, LANES), col),
        scratch_shapes=[pltpu.VMEM((3, 2, BAND, ROWBLOCK), F32),
                        pltpu.VMEM((2, BAND, ROWBLOCK), F32),
                        pltpu.VMEM((2, BAND, ROWBLOCK), F32)],
        compiler_params=pltpu.CompilerParams(
            dimension_semantics=("arbitrary", "arbitrary"), vmem_limit_bytes=VMEM_LIMIT),
        name="na_attn",
    )(qn, kn, vt, c2)


def _gqa_kernel(q_ref, ka_ref, kb_ref, vt_ref, o_ref, qm_ref, s0_ref, s1_ref, m_ref, acc_ref, *,
                tq, n_chunks):
    qlane = lax.broadcasted_iota(jnp.int32, (tq, LANES), 1)
    for h in range(GQA_HEADS):
        g, half = h // 2, h % 2
        q2 = q_ref[:, g * LANES:(g + 1) * LANES]
        in_head = (qlane < HEAD_DIM) if half == 0 else (qlane >= HEAD_DIM)
        qm_ref[h] = jnp.where(in_head, q2, jnp.zeros_like(q2))
    m_ref[...] = jnp.full(m_ref.shape, MASKED, F32)
    acc_ref[...] = jnp.zeros(acc_ref.shape, F32)

    s_refs = (s0_ref, s1_ref)

    def scores(h, c, slot):
        kvh, half = h // GQA_GROUP, h % 2
        k_ref = ka_ref if half == kvh else kb_ref
        rows = pl.ds(pl.multiple_of(c * ROWBLOCK, ROWBLOCK), ROWBLOCK)
        s_refs[slot][h] = lax.dot_general(k_ref[rows, :], qm_ref[h], NT_DIMS,
                                          preferred_element_type=F32)

    def softmax_pv(h, slot, vt_aug):
        s = s_refs[slot][h]
        m_old = m_ref[h]
        m_new = jnp.maximum(m_old, jnp.max(s, axis=0, keepdims=True))
        alpha = jnp.exp2(m_old - m_new)
        pb = jnp.exp2(s - m_new).astype(BF16)
        acc_ref[h] = alpha * acc_ref[h] + jnp.dot(vt_aug, pb, preferred_element_type=F32)
        m_ref[h] = m_new

    def step(c, slot, prefetch):
        vt_aug = [_with_ones(vt_ref[c, kvh * HEAD_DIM:(kvh + 1) * HEAD_DIM, :])
                  for kvh in range(GQA_KV_HEADS)]
        if prefetch:
            scores(0, c + 1, 1 - slot)
        for h in range(GQA_HEADS):
            if prefetch and h + 1 < GQA_HEADS:
                scores(h + 1, c + 1, 1 - slot)
            softmax_pv(h, slot, vt_aug[h // GQA_GROUP])

    for h in range(GQA_HEADS):
        scores(h, 0, 0)

    def chunk_pair(j, carry):
        step(2 * j, 0, True)
        step(2 * j + 1, 1, True)
        return carry

    assert n_chunks % 2 == 0
    lax.fori_loop(0, n_chunks // 2 - 1, chunk_pair, 0)
    step(n_chunks - 2, 0, True)
    step(n_chunks - 1, 1, False)

    for g in range(GQA_HEADS // 2):
        pair = []
        for e in range(2):
            a = acc_ref[2 * g + e]
            pair.append(a[0:HEAD_DIM] / a[HEAD_DIM:HEAD_DIM + 1])
        o_ref[:, g * LANES:(g + 1) * LANES] = jnp.concatenate(pair, axis=0).T.astype(BF16)


def _gqa_attention(qg, ka, kb, vt, tq):
    B, T, _ = qg.shape
    n_chunks = T // ROWBLOCK
    kv_block = NA_WIDTH // KV_WIDTH
    return pl.pallas_call(
        functools.partial(_gqa_kernel, tq=tq, n_chunks=n_chunks),
        out_shape=jax.ShapeDtypeStruct((B, T, GQA_WIDTH), BF16),
        grid=(B, T // tq),
        in_specs=[
            pl.BlockSpec((None, tq, GQA_WIDTH), lambda b, i: (b, i, 0)),
            pl.BlockSpec((None, T, KV_WIDTH), lambda b, i: (b, 0, 0)),
            pl.BlockSpec((None, T, KV_WIDTH), lambda b, i: (b, 0, 0)),
            pl.BlockSpec((None, n_chunks, KV_WIDTH, ROWBLOCK), lambda b, i: (b, 0, kv_block, 0)),
        ],
        out_specs=pl.BlockSpec((None, tq, GQA_WIDTH), lambda b, i: (b, i, 0)),
        scratch_shapes=[
            pltpu.VMEM((GQA_HEADS, tq, LANES), BF16),
            pltpu.VMEM((GQA_HEADS, ROWBLOCK, tq), F32),
            pltpu.VMEM((GQA_HEADS, ROWBLOCK, tq), F32),
            pltpu.VMEM((GQA_HEADS, 1, tq), F32),
            pltpu.VMEM((GQA_HEADS, V_AUG, tq), F32),
        ],
        compiler_params=pltpu.CompilerParams(
            dimension_semantics=("arbitrary", "arbitrary"), vmem_limit_bytes=VMEM_LIMIT),
        name="gqa_attn",
    )(qg, ka, kb, vt)


def _post_kernel(x_ref, ona_ref, og_ref, mod_ref, wo_ref, gf_ref, wg_ref, wu_ref, wd_ref,
                 gfin_ref, o_ref, *, tm):
    gate_a, shift_f, gain_f, gate_f = (mod_ref[2:3, :], mod_ref[3:4, :], 1.0 + mod_ref[4:5, :],
                                       mod_ref[5:6, :])

    def rms(x):
        return x * lax.rsqrt(jnp.mean(x * x, axis=-1, keepdims=True) + EPS)

    subs = [pl.ds(r * ROWBLOCK, ROWBLOCK) for r in range(tm // ROWBLOCK)]
    attn = [jnp.dot(ona_ref[rows, :], wo_ref[0:NA_WIDTH, :], preferred_element_type=F32)
            + jnp.dot(og_ref[rows, :], wo_ref[NA_WIDTH:, :], preferred_element_type=F32)
            for rows in subs]
    x1 = [x_ref[rows, :] + gate_a * a for rows, a in zip(subs, attn)]
    hb = [((rms(v) * gf_ref[...]) * gain_f + shift_f).astype(BF16) for v in x1]
    gu = [(jnp.dot(h, wg_ref[...], preferred_element_type=F32),
           jnp.dot(h, wu_ref[...], preferred_element_type=F32)) for h in hb]
    ff = [(_silu(g) * u).astype(BF16) for g, u in gu]
    x2 = [v + gate_f * jnp.dot(f, wd_ref[...], preferred_element_type=F32) for v, f in zip(x1, ff)]
    for rows, v in zip(subs, x2):
        o_ref[rows, :] = rms(v) * gfin_ref[...]


def _post(x, o_na, o_g, mod, w_o, g_ffn, w_gate, w_up, w_down, g_final, tm):
    B, T, _ = x.shape
    d_ff = w_gate.shape[1]
    row = lambda b, i: (b, i, 0)
    const2 = lambda b, i: (0, 0)
    resident = pl.Buffered(1)
    return pl.pallas_call(
        functools.partial(_post_kernel, tm=tm),
        out_shape=jax.ShapeDtypeStruct((B, T, D_MODEL), F32),
        grid=(B, T // tm),
        in_specs=[
            pl.BlockSpec((None, tm, D_MODEL), row),
            pl.BlockSpec((None, tm, NA_WIDTH), row),
            pl.BlockSpec((None, tm, GQA_WIDTH), row),
            pl.BlockSpec((None, N_MOD, D_MODEL), lambda b, i: (b, 0, 0)),
            pl.BlockSpec((D_MODEL, D_MODEL), const2, pipeline_mode=resident),
            pl.BlockSpec((1, D_MODEL), const2),
            pl.BlockSpec((D_MODEL, d_ff), const2, pipeline_mode=resident),
            pl.BlockSpec((D_MODEL, d_ff), const2, pipeline_mode=resident),
            pl.BlockSpec((d_ff, D_MODEL), const2, pipeline_mode=resident),
            pl.BlockSpec((1, D_MODEL), const2),
        ],
        out_specs=pl.BlockSpec((None, tm, D_MODEL), row),
        compiler_params=pltpu.CompilerParams(
            dimension_semantics=("arbitrary", "arbitrary"), vmem_limit_bytes=VMEM_LIMIT),
        name="post",
    )(x, o_na, o_g, mod, w_o, g_ffn, w_gate, w_up, w_down, g_final)


def _rope_tables(n_tokens):
    t = jnp.arange(n_tokens)
    row = (t // GRID_W).astype(F32)
    col = (t % GRID_W).astype(F32)

    def angles(pos, dims):
        inv = ROPE_THETA ** (-jnp.arange(0, dims, 2, dtype=F32) / dims)
        return pos[:, None] * inv[None, :]

    ang = jnp.concatenate([angles(row, HEAD_DIM // 2), angles(col, HEAD_DIM // 2)], axis=-1)
    cos, sin = jnp.cos(ang), jnp.sin(ang)
    reps = LANES // HEAD_DIM
    return (jnp.tile(jnp.concatenate([cos, cos], axis=-1), (1, reps)),
            jnp.tile(jnp.concatenate([-sin, sin], axis=-1), (1, reps)))


def _na_bias_columns(rpb_l):
    kc = jnp.arange(GRID_W)[:, None]
    qc = jnp.arange(GRID_W)[None, :]
    cs = jnp.clip(qc - NA_WIN_W // 2, 0, GRID_W - NA_WIN_W)
    valid = (kc >= cs) & (kc < cs + NA_WIN_W)
    off = jnp.clip(kc - qc + NA_WIN_W - 1, 0, 2 * NA_WIN_W - 2)
    c = jnp.where(valid, rpb_l[:, :, off], MASKED)
    return jnp.concatenate([c, c], axis=-1).astype(F32)


def _permute_in_columns(w):
    q_na, k_na, v_na, q_g, k_g, v_g = jnp.split(
        w, (NA_WIDTH, 2 * NA_WIDTH, 3 * NA_WIDTH, 3 * NA_WIDTH + GQA_WIDTH,
            3 * NA_WIDTH + GQA_WIDTH + KV_WIDTH), axis=-1)
    return jnp.concatenate([q_g, k_g, v_g, v_na, q_na, k_na], axis=-1)


def kernel(x, c, w_ada, b_ada, g_attn, w_in, g_q, g_k, rpb, w_o, g_ffn, w_gate, w_up, w_down, g_final):
    B, T, _ = x.shape
    assert w_ada.shape[0] == 1, "single-layer block: the final norm is fused into the layer"
    cos_t, sin_t = _rope_tables(T)
    reps = LANES // HEAD_DIM
    mod = _ada(c, w_ada[0], b_ada[0]).reshape(B, N_MOD, D_MODEL)
    qn, kn, qg, ka, kb, vt = _in_proj(
        x, mod, g_attn[0].reshape(1, D_MODEL), _permute_in_columns(w_in[0]).astype(BF16),
        cos_t, sin_t,
        jnp.tile(g_q[0], reps).reshape(1, LANES), jnp.tile(g_k[0], reps).reshape(1, LANES),
        tm=512)
    o_na = _na_attention(qn, kn, vt, _na_bias_columns(rpb[0]))
    o_g = _gqa_attention(qg, ka, kb, vt, tq=256)
    return _post(x, o_na, o_g, mod, w_o[0].astype(BF16), g_ffn[0].reshape(1, D_MODEL),
                 w_gate[0].astype(BF16), w_up[0].astype(BF16), w_down[0].astype(BF16),
                 g_final.reshape(1, D_MODEL), tm=512)
```

```python
import functools

import jax
import jax.numpy as jnp
from jax import lax
from jax.experimental import pallas as pl
from jax.experimental.pallas import tpu as pltpu

F32 = jnp.float32
BF16 = jnp.bfloat16

D_MODEL = 1024
HEAD_DIM = 64
NA_HEADS = 8
GQA_HEADS = 8
GQA_KV_HEADS = 2
GQA_GROUP = GQA_HEADS // GQA_KV_HEADS
NA_WIDTH = NA_HEADS * HEAD_DIM
GQA_WIDTH = GQA_HEADS * HEAD_DIM
KV_WIDTH = GQA_KV_HEADS * HEAD_DIM
IN_WIDTH = 3 * NA_WIDTH + GQA_WIDTH + 2 * KV_WIDTH
V_WIDTH = NA_WIDTH + KV_WIDTH
GRID_W = 64
NA_WIN_H = 8
NA_WIN_W = 16
ROPE_THETA = 10000.0
N_MOD = 6
EPS = 1e-6

LANES = 128
BF16_SUBLANES = 16
ROWBLOCK_ROWS = 4
ROWBLOCK = ROWBLOCK_ROWS * GRID_W
BAND_BLOCKS = 3
BAND = BAND_BLOCKS * ROWBLOCK
V_AUG = HEAD_DIM + BF16_SUBLANES
MASKED = -1e30
LOG2E = 1.4426950408889634
VMEM_LIMIT = 56 * 1024 * 1024

COL_QG, COL_KG, COL_VG, COL_VN, COL_QN, COL_KN = 0, 512, 640, 768, 1280, 1792
IN_GROUPS = ((0, COL_VN), (COL_VN, COL_QN), (COL_QN, IN_WIDTH))


def _silu(x):
    return x / (1.0 + jnp.exp(-x))


def _with_ones(vt):
    return jnp.concatenate([vt, jnp.ones((BF16_SUBLANES, vt.shape[1]), vt.dtype)], axis=0)


def _ada_kernel(c_ref, w_ref, b_ref, o_ref):
    ca = _silu(c_ref[...])
    o_ref[...] = jnp.dot(ca.astype(BF16), w_ref[...].astype(BF16),
                         preferred_element_type=F32) + b_ref[...]


def _ada(c, w_ada, b_ada):
    B = c.shape[0]
    n = w_ada.shape[1]
    return pl.pallas_call(
        _ada_kernel,
        out_shape=jax.ShapeDtypeStruct((B, n), F32),
        grid=(n // D_MODEL,),
        in_specs=[pl.BlockSpec((B, D_MODEL), lambda j: (0, 0)),
                  pl.BlockSpec((D_MODEL, D_MODEL), lambda j: (0, j)),
                  pl.BlockSpec((1, D_MODEL), lambda j: (0, j))],
        out_specs=pl.BlockSpec((B, D_MODEL), lambda j: (0, j)),
        compiler_params=pltpu.CompilerParams(dimension_semantics=("arbitrary",)),
        name="ada",
    )(c, w_ada, b_ada.reshape(1, n))


def _in_kernel(x_ref, mod_ref, g_ref, w_ref, cos_ref, sin_ref, gq_ref, gk_ref,
               qn_ref, kn_ref, qg_ref, ka_ref, kb_ref, vt_ref, *, tm):
    scale = HEAD_DIM ** -0.5 * LOG2E
    lane = lax.broadcasted_iota(jnp.int32, (ROWBLOCK, LANES), 1)
    low_head = lane < HEAD_DIM
    first_half = (lane & (HEAD_DIM - 1)) < HEAD_DIM // 2
    shift, gain = mod_ref[0:1, :], 1.0 + mod_ref[1:2, :]

    def hidden(rows):
        x = x_ref[rows, :]
        ms = jnp.mean(x * x, axis=-1, keepdims=True)
        return ((x * lax.rsqrt(ms + EPS) * g_ref[...]) * gain + shift).astype(BF16)

    def norm_rope(xg, head_gain, cos, sin):
        sq = xg * xg
        s_lo = jnp.sum(jnp.where(low_head, sq, 0.0), axis=-1, keepdims=True)
        s_hi = jnp.sum(jnp.where(low_head, 0.0, sq), axis=-1, keepdims=True)
        inv = lax.rsqrt(jnp.where(low_head, s_lo, s_hi) * (1.0 / HEAD_DIM) + EPS)
        yg = xg * inv * head_gain
        rot = jnp.where(first_half,
                        pltpu.roll(yg, LANES - HEAD_DIM // 2, 1),
                        pltpu.roll(yg, HEAD_DIM // 2, 1))
        return yg * cos + rot * sin

    subs = [pl.ds(r * ROWBLOCK, ROWBLOCK) for r in range(tm // ROWBLOCK)]
    hs = [hidden(rows) for rows in subs]
    accs = [[jnp.dot(hb, w_ref[:, c0:c1], preferred_element_type=F32) for c0, c1 in IN_GROUPS]
            for hb in hs]
    for r, rows in enumerate(subs):
        a_gqa, a_vn, a_na = accs[r]
        cos, sin = cos_ref[rows, :], sin_ref[rows, :]
        for g in range(GQA_WIDTH // LANES):
            cols = slice(COL_QG + g * LANES, COL_QG + (g + 1) * LANES)
            qg_ref[rows, g * LANES:(g + 1) * LANES] = (
                norm_rope(a_gqa[:, cols], gq_ref[...], cos, sin) * scale).astype(BF16)
        kr = norm_rope(a_gqa[:, COL_KG:COL_KG + KV_WIDTH], gk_ref[...], cos, sin)
        ka_ref[rows, :] = kr.astype(BF16)
        kb_ref[rows, :] = pltpu.roll(kr, HEAD_DIM, 1).astype(BF16)
        for j in range(NA_WIDTH // LANES):
            vt_ref[r, j * LANES:(j + 1) * LANES, :] = a_vn[:, j * LANES:(j + 1) * LANES].T.astype(BF16)
        vt_ref[r, NA_WIDTH:V_WIDTH, :] = a_gqa[:, COL_VG:COL_VG + KV_WIDTH].T.astype(BF16)
        qn_ref[rows, :] = (a_na[:, 0:NA_WIDTH] * scale).astype(BF16)
        kn_ref[rows, :] = a_na[:, NA_WIDTH:2 * NA_WIDTH].astype(BF16)


def _in_proj(x, mod, g_attn, w_in, cos_t, sin_t, gq_t, gk_t, tm):
    B, T, _ = x.shape
    nrb = T // ROWBLOCK
    row = lambda b, i: (b, i, 0)
    const2 = lambda b, i: (0, 0)
    out_shape = (
        jax.ShapeDtypeStruct((B, T, NA_WIDTH), BF16),
        jax.ShapeDtypeStruct((B, T, NA_WIDTH), BF16),
        jax.ShapeDtypeStruct((B, T, GQA_WIDTH), BF16),
        jax.ShapeDtypeStruct((B, T, KV_WIDTH), BF16),
        jax.ShapeDtypeStruct((B, T, KV_WIDTH), BF16),
        jax.ShapeDtypeStruct((B, nrb, V_WIDTH, ROWBLOCK), BF16),
    )
    return pl.pallas_call(
        functools.partial(_in_kernel, tm=tm),
        out_shape=out_shape,
        grid=(B, T // tm),
        in_specs=[
            pl.BlockSpec((None, tm, D_MODEL), row),
            pl.BlockSpec((None, N_MOD, D_MODEL), lambda b, i: (b, 0, 0)),
            pl.BlockSpec((1, D_MODEL), const2),
            pl.BlockSpec((D_MODEL, IN_WIDTH), const2, pipeline_mode=pl.Buffered(1)),
            pl.BlockSpec((tm, LANES), lambda b, i: (i, 0)),
            pl.BlockSpec((tm, LANES), lambda b, i: (i, 0)),
            pl.BlockSpec((1, LANES), const2),
            pl.BlockSpec((1, LANES), const2),
        ],
        out_specs=(
            pl.BlockSpec((None, tm, NA_WIDTH), row),
            pl.BlockSpec((None, tm, NA_WIDTH), row),
            pl.BlockSpec((None, tm, GQA_WIDTH), row),
            pl.BlockSpec((None, tm, KV_WIDTH), row),
            pl.BlockSpec((None, tm, KV_WIDTH), row),
            pl.BlockSpec((None, tm // ROWBLOCK, V_WIDTH, ROWBLOCK), lambda b, i: (b, i, 0, 0)),
        ),
        compiler_params=pltpu.CompilerParams(
            dimension_semantics=("arbitrary", "arbitrary"), vmem_limit_bytes=VMEM_LIMIT),
        name="in_proj",
    )(x, mod, g_attn, w_in, cos_t, sin_t, gq_t, gk_t)


def _na_variant(variant, i, qr):
    if variant == 0:
        valid, di = i < NA_WIN_H, i - qr
    elif variant == 1:
        valid, di = 0 <= i - qr < NA_WIN_H, i - qr - ROWBLOCK_ROWS
    else:
        valid, di = ROWBLOCK_ROWS <= i < ROWBLOCK_ROWS + NA_WIN_H, i - qr - 2 * ROWBLOCK_ROWS
    return di + NA_WIN_H - 1 if valid else None


def _na_kernel(q_ref, k_ref, vt_ref, c2_ref, o_ref, tbl_ref, s0_ref, s1_ref, *, n_rb):
    lane = lax.broadcasted_iota(jnp.int32, (GRID_W, LANES), 1)
    left = lane < GRID_W

    @pl.when(pl.program_id(1) == 0)
    def _build_bias():
        for variant in range(3):
            for hh in range(2):
                for i in range(BAND_BLOCKS * ROWBLOCK_ROWS):
                    for qp in range(ROWBLOCK_ROWS // 2):
                        ro = [_na_variant(variant, i, 2 * qp + e) for e in range(2)]
                        t = [jnp.full((GRID_W, LANES), MASKED, F32) if r is None
                             else c2_ref[hh, r] for r in ro]
                        tbl_ref[variant, hh, i * GRID_W:(i + 1) * GRID_W,
                                qp * LANES:(qp + 1) * LANES] = jnp.where(left, t[0], t[1]) * LOG2E

    feat = lax.broadcasted_iota(jnp.int32, (LANES, ROWBLOCK), 0)
    s_refs = (s0_ref, s1_ref)

    def band_start(rb):
        return jnp.clip(rb - 1, 0, n_rb - BAND_BLOCKS)

    def scores(rb, slot):
        q = q_ref[pl.ds(pl.multiple_of(rb * ROWBLOCK, ROWBLOCK), ROWBLOCK), :]
        kband = k_ref[pl.ds(pl.multiple_of(band_start(rb) * ROWBLOCK, ROWBLOCK), BAND), :]
        qt = q.astype(F32).T
        for hh in range(2):
            in_head = (feat < HEAD_DIM) if hh == 0 else (feat >= HEAD_DIM)
            qh = jnp.where(in_head, qt, 0.0).astype(BF16)
            s_refs[slot][hh] = jnp.dot(kband, qh, preferred_element_type=F32)

    def softmax_pv(rb, slot):
        bs = band_start(rb)
        variant = jnp.where(rb == 0, 0, jnp.where(rb == n_rb - 1, 2, 1))
        outs = []
        for hh in range(2):
            s = s_refs[slot][hh] + tbl_ref[variant, hh]
            m = jnp.max(s, axis=0, keepdims=True)
            pb = jnp.exp2(s - m).astype(BF16)
            o_aug = jnp.zeros((V_AUG, ROWBLOCK), F32)
            for j in range(BAND_BLOCKS):
                vt = _with_ones(vt_ref[bs + j, hh * HEAD_DIM:(hh + 1) * HEAD_DIM, :])
                o_aug = o_aug + jnp.dot(vt, pb[j * ROWBLOCK:(j + 1) * ROWBLOCK, :],
                                        preferred_element_type=F32)
            outs.append(o_aug[0:HEAD_DIM] / o_aug[HEAD_DIM:HEAD_DIM + 1])
        o_pair = jnp.concatenate(outs, axis=0)
        o_ref[pl.ds(pl.multiple_of(rb * ROWBLOCK, ROWBLOCK), ROWBLOCK), :] = o_pair.T.astype(BF16)

    def step(rb, slot, prefetch):
        if prefetch:
            scores(rb + 1, 1 - slot)
        softmax_pv(rb, slot)

    scores(0, 0)

    def block_pair(j, carry):
        step(2 * j, 0, True)
        step(2 * j + 1, 1, True)
        return carry

    assert n_rb % 2 == 0
    lax.fori_loop(0, n_rb // 2 - 1, block_pair, 0)
    step(n_rb - 2, 0, True)
    step(n_rb - 1, 1, False)


def _na_attention(qn, kn, vt, c2):
    B, T, _ = qn.shape
    n_rb = T // ROWBLOCK
    n_pairs = NA_HEADS // 2
    col = lambda p, b: (b, 0, p)
    return pl.pallas_call(
        functools.partial(_na_kernel, n_rb=n_rb),
        out_shape=jax.ShapeDtypeStruct((B, T, NA_WIDTH), BF16),
        grid=(n_pairs, B),
        in_specs=[
            pl.BlockSpec((None, T, LANES), col),
            pl.BlockSpec((None, T, LANES), col),
            pl.BlockSpec((None, n_rb, LANES, ROWBLOCK), lambda p, b: (b, 0, p, 0)),
            pl.BlockSpec((2, 2 * NA_WIN_H - 1, GRID_W, LANES), lambda p, b: (p, 0, 0, 0)),
        ],
        out_specs=pl.BlockSpec((None, T, LANES), col),
        scratch_shapes=[pltpu.VMEM((3, 2, BAND, ROWBLOCK), F32),
                        pltpu.VMEM((2, BAND, ROWBLOCK), F32),
                        pltpu.VMEM((2, BAND, ROWBLOCK), F32)],
        compiler_params=pltpu.CompilerParams(
            dimension_semantics=("arbitrary", "arbitrary"), vmem_limit_bytes=VMEM_LIMIT),
        name="na_attn",
    )(qn, kn, vt, c2)


def _gqa_kernel(q_ref, ka_ref, kb_ref, vt_ref, o_ref, qm_ref, s0_ref, s1_ref, m_ref, acc_ref, *,
                tq, n_chunks):
    feat = lax.broadcasted_iota(jnp.int32, (LANES, tq), 0)
    for g in range(GQA_HEADS // 2):
        q2t = q_ref[:, g * LANES:(g + 1) * LANES].astype(F32).T
        for half in range(2):
            in_head = (feat < HEAD_DIM) if half == 0 else (feat >= HEAD_DIM)
            qm_ref[2 * g + half] = jnp.where(in_head, q2t, 0.0).astype(BF16)
    m_ref[...] = jnp.full(m_ref.shape, MASKED, F32)
    acc_ref[...] = jnp.zeros(acc_ref.shape, F32)

    s_refs = (s0_ref, s1_ref)

    def scores(h, c, slot):
        kvh, half = h // GQA_GROUP, h % 2
        k_ref = ka_ref if half == kvh else kb_ref
        rows = pl.ds(pl.multiple_of(c * ROWBLOCK, ROWBLOCK), ROWBLOCK)
        s_refs[slot][h] = jnp.dot(k_ref[rows, :], qm_ref[h], preferred_element_type=F32)

    def softmax_pv(h, slot, vt_aug):
        s = s_refs[slot][h]
        m_old = m_ref[h]
        m_new = jnp.maximum(m_old, jnp.max(s, axis=0, keepdims=True))
        alpha = jnp.exp2(m_old - m_new)
        pb = jnp.exp2(s - m_new).astype(BF16)
        acc_ref[h] = alpha * acc_ref[h] + jnp.dot(vt_aug, pb, preferred_element_type=F32)
        m_ref[h] = m_new

    def step(c, slot, prefetch):
        vt_aug = [_with_ones(vt_ref[c, kvh * HEAD_DIM:(kvh + 1) * HEAD_DIM, :])
                  for kvh in range(GQA_KV_HEADS)]
        if prefetch:
            scores(0, c + 1, 1 - slot)
        for h in range(GQA_HEADS):
            if prefetch and h + 1 < GQA_HEADS:
                scores(h + 1, c + 1, 1 - slot)
            softmax_pv(h, slot, vt_aug[h // GQA_GROUP])

    for h in range(GQA_HEADS):
        scores(h, 0, 0)

    def chunk_pair(j, carry):
        step(2 * j, 0, True)
        step(2 * j + 1, 1, True)
        return carry

    assert n_chunks % 2 == 0
    lax.fori_loop(0, n_chunks // 2 - 1, chunk_pair, 0)
    step(n_chunks - 2, 0, True)
    step(n_chunks - 1, 1, False)

    for g in range(GQA_HEADS // 2):
        pair = []
        for e in range(2):
            a = acc_ref[2 * g + e]
            pair.append(a[0:HEAD_DIM] / a[HEAD_DIM:HEAD_DIM + 1])
        o_ref[:, g * LANES:(g + 1) * LANES] = jnp.concatenate(pair, axis=0).T.astype(BF16)


def _gqa_attention(qg, ka, kb, vt, tq):
    B, T, _ = qg.shape
    n_chunks = T // ROWBLOCK
    kv_block = NA_WIDTH // KV_WIDTH
    return pl.pallas_call(
        functools.partial(_gqa_kernel, tq=tq, n_chunks=n_chunks),
        out_shape=jax.ShapeDtypeStruct((B, T, GQA_WIDTH), BF16),
        grid=(B, T // tq),
        in_specs=[
            pl.BlockSpec((None, tq, GQA_WIDTH), lambda b, i: (b, i, 0)),
            pl.BlockSpec((None, T, KV_WIDTH), lambda b, i: (b, 0, 0)),
            pl.BlockSpec((None, T, KV_WIDTH), lambda b, i: (b, 0, 0)),
            pl.BlockSpec((None, n_chunks, KV_WIDTH, ROWBLOCK), lambda b, i: (b, 0, kv_block, 0)),
        ],
        out_specs=pl.BlockSpec((None, tq, GQA_WIDTH), lambda b, i: (b, i, 0)),
        scratch_shapes=[
            pltpu.VMEM((GQA_HEADS, LANES, tq), BF16),
            pltpu.VMEM((GQA_HEADS, ROWBLOCK, tq), F32),
            pltpu.VMEM((GQA_HEADS, ROWBLOCK, tq), F32),
            pltpu.VMEM((GQA_HEADS, 1, tq), F32),
            pltpu.VMEM((GQA_HEADS, V_AUG, tq), F32),
        ],
        compiler_params=pltpu.CompilerParams(
            dimension_semantics=("arbitrary", "arbitrary"), vmem_limit_bytes=VMEM_LIMIT),
        name="gqa_attn",
    )(qg, ka, kb, vt)


def _post_kernel(x_ref, ona_ref, og_ref, mod_ref, wo_ref, gf_ref, wg_ref, wu_ref, wd_ref,
                 gfin_ref, o_ref, *, tm):
    gate_a, shift_f, gain_f, gate_f = (mod_ref[2:3, :], mod_ref[3:4, :], 1.0 + mod_ref[4:5, :],
                                       mod_ref[5:6, :])

    def rms(x):
        return x * lax.rsqrt(jnp.mean(x * x, axis=-1, keepdims=True) + EPS)

    subs = [pl.ds(r * ROWBLOCK, ROWBLOCK) for r in range(tm // ROWBLOCK)]
    attn = [jnp.dot(ona_ref[rows, :], wo_ref[0:NA_WIDTH, :], preferred_element_type=F32)
            + jnp.dot(og_ref[rows, :], wo_ref[NA_WIDTH:, :], preferred_element_type=F32)
            for rows in subs]
    x1 = [x_ref[rows, :] + gate_a * a for rows, a in zip(subs, attn)]
    hb = [((rms(v) * gf_ref[...]) * gain_f + shift_f).astype(BF16) for v in x1]
    gu = [(jnp.dot(h, wg_ref[...], preferred_element_type=F32),
           jnp.dot(h, wu_ref[...], preferred_element_type=F32)) for h in hb]
    ff = [(_silu(g) * u).astype(BF16) for g, u in gu]
    x2 = [v + gate_f * jnp.dot(f, wd_ref[...], preferred_element_type=F32) for v, f in zip(x1, ff)]
    for rows, v in zip(subs, x2):
        o_ref[rows, :] = rms(v) * gfin_ref[...]


def _post(x, o_na, o_g, mod, w_o, g_ffn, w_gate, w_up, w_down, g_final, tm):
    B, T, _ = x.shape
    d_ff = w_gate.shape[1]
    row = lambda b, i: (b, i, 0)
    const2 = lambda b, i: (0, 0)
    resident = pl.Buffered(1)
    return pl.pallas_call(
        functools.partial(_post_kernel, tm=tm),
        out_shape=jax.ShapeDtypeStruct((B, T, D_MODEL), F32),
        grid=(B, T // tm),
        in_specs=[
            pl.BlockSpec((None, tm, D_MODEL), row),
            pl.BlockSpec((None, tm, NA_WIDTH), row),
            pl.BlockSpec((None, tm, GQA_WIDTH), row),
            pl.BlockSpec((None, N_MOD, D_MODEL), lambda b, i: (b, 0, 0)),
            pl.BlockSpec((D_MODEL, D_MODEL), const2, pipeline_mode=resident),
            pl.BlockSpec((1, D_MODEL), const2),
            pl.BlockSpec((D_MODEL, d_ff), const2, pipeline_mode=resident),
            pl.BlockSpec((D_MODEL, d_ff), const2, pipeline_mode=resident),
            pl.BlockSpec((d_ff, D_MODEL), const2, pipeline_mode=resident),
            pl.BlockSpec((1, D_MODEL), const2),
        ],
        out_specs=pl.BlockSpec((None, tm, D_MODEL), row),
        compiler_params=pltpu.CompilerParams(
            dimension_semantics=("arbitrary", "arbitrary"), vmem_limit_bytes=VMEM_LIMIT),
        name="post",
    )(x, o_na, o_g, mod, w_o, g_ffn, w_gate, w_up, w_down, g_final)


def _rope_tables(n_tokens):
    t = jnp.arange(n_tokens)
    row = (t // GRID_W).astype(F32)
    col = (t % GRID_W).astype(F32)

    def angles(pos, dims):
        inv = ROPE_THETA ** (-jnp.arange(0, dims, 2, dtype=F32) / dims)
        return pos[:, None] * inv[None, :]

    ang = jnp.concatenate([angles(row, HEAD_DIM // 2), angles(col, HEAD_DIM // 2)], axis=-1)
    cos, sin = jnp.cos(ang), jnp.sin(ang)
    reps = LANES // HEAD_DIM
    return (jnp.tile(jnp.concatenate([cos, cos], axis=-1), (1, reps)),
            jnp.tile(jnp.concatenate([-sin, sin], axis=-1), (1, reps)))


def _na_bias_columns(rpb_l):
    kc = jnp.arange(GRID_W)[:, None]
    qc = jnp.arange(GRID_W)[None, :]
    cs = jnp.clip(qc - NA_WIN_W // 2, 0, GRID_W - NA_WIN_W)
    valid = (kc >= cs) & (kc < cs + NA_WIN_W)
    off = jnp.clip(kc - qc + NA_WIN_W - 1, 0, 2 * NA_WIN_W - 2)
    c = jnp.where(valid, rpb_l[:, :, off], MASKED)
    return jnp.concatenate([c, c], axis=-1).astype(F32)


def _permute_in_columns(w):
    q_na, k_na, v_na, q_g, k_g, v_g = jnp.split(
        w, (NA_WIDTH, 2 * NA_WIDTH, 3 * NA_WIDTH, 3 * NA_WIDTH + GQA_WIDTH,
            3 * NA_WIDTH + GQA_WIDTH + KV_WIDTH), axis=-1)
    return jnp.concatenate([q_g, k_g, v_g, v_na, q_na, k_na], axis=-1)


def kernel(x, c, w_ada, b_ada, g_attn, w_in, g_q, g_k, rpb, w_o, g_ffn, w_gate, w_up, w_down, g_final):
    B, T, _ = x.shape
    assert w_ada.shape[0] == 1, "single-layer block: the final norm is fused into the layer"
    cos_t, sin_t = _rope_tables(T)
    reps = LANES // HEAD_DIM
    mod = _ada(c, w_ada[0], b_ada[0]).reshape(B, N_MOD, D_MODEL)
    qn, kn, qg, ka, kb, vt = _in_proj(
        x, mod, g_attn[0].reshape(1, D_MODEL), _permute_in_columns(w_in[0]).astype(BF16),
        cos_t, sin_t,
        jnp.tile(g_q[0], reps).reshape(1, LANES), jnp.tile(g_k[0], reps).reshape(1, LANES),
        tm=512)
    o_na = _na_attention(qn, kn, vt, _na_bias_columns(rpb[0]))
    o_g = _gqa_attention(qg, ka, kb, vt, tq=256)
    return _post(x, o_na, o_g, mod, w_o[0].astype(BF16), g_ffn[0].reshape(1, D_MODEL),
                 w_gate[0].astype(BF16), w_up[0].astype(BF16), w_down[0].astype(BF16),
                 g_final.reshape(1, D_MODEL), tm=512)
```

```python
import functools

import jax
import jax.numpy as jnp
from jax import lax
from jax.experimental import pallas as pl
from jax.experimental.pallas import tpu as pltpu

F32 = jnp.float32
BF16 = jnp.bfloat16

D_MODEL = 1024
HEAD_DIM = 64
NA_HEADS = 8
GQA_HEADS = 8
GQA_KV_HEADS = 2
GQA_GROUP = GQA_HEADS // GQA_KV_HEADS
NA_WIDTH = NA_HEADS * HEAD_DIM
GQA_WIDTH = GQA_HEADS * HEAD_DIM
KV_WIDTH = GQA_KV_HEADS * HEAD_DIM
IN_WIDTH = 3 * NA_WIDTH + GQA_WIDTH + 2 * KV_WIDTH
V_WIDTH = NA_WIDTH + KV_WIDTH
GRID_W = 64
NA_WIN_H = 8
NA_WIN_W = 16
ROPE_THETA = 10000.0
N_MOD = 6
EPS = 1e-6

LANES = 128
BF16_SUBLANES = 16
ROWBLOCK_ROWS = 4
ROWBLOCK = ROWBLOCK_ROWS * GRID_W
BAND_BLOCKS = 3
BAND = BAND_BLOCKS * ROWBLOCK
V_AUG = HEAD_DIM + BF16_SUBLANES
MASKED = -1e30
LOG2E = 1.4426950408889634
VMEM_LIMIT = 56 * 1024 * 1024

COL_QG, COL_KG, COL_VG, COL_VN, COL_QN, COL_KN = 0, 512, 640, 768, 1280, 1792
IN_GROUPS = ((0, COL_VN), (COL_VN, COL_QN), (COL_QN, IN_WIDTH))


def _silu(x):
    return x / (1.0 + jnp.exp(-x))


def _with_ones(vt):
    return jnp.concatenate([vt, jnp.ones((BF16_SUBLANES, vt.shape[1]), vt.dtype)], axis=0)


def _ada_kernel(c_ref, w_ref, b_ref, o_ref):
    ca = _silu(c_ref[...])
    o_ref[...] = jnp.dot(ca.astype(BF16), w_ref[...].astype(BF16),
                         preferred_element_type=F32) + b_ref[...]


def _ada(c, w_ada, b_ada):
    B = c.shape[0]
    n = w_ada.shape[1]
    return pl.pallas_call(
        _ada_kernel,
        out_shape=jax.ShapeDtypeStruct((B, n), F32),
        grid=(n // D_MODEL,),
        in_specs=[pl.BlockSpec((B, D_MODEL), lambda j: (0, 0)),
                  pl.BlockSpec((D_MODEL, D_MODEL), lambda j: (0, j)),
                  pl.BlockSpec((1, D_MODEL), lambda j: (0, j))],
        out_specs=pl.BlockSpec((B, D_MODEL), lambda j: (0, j)),
        compiler_params=pltpu.CompilerParams(dimension_semantics=("arbitrary",)),
        name="ada",
    )(c, w_ada, b_ada.reshape(1, n))


def _in_kernel(x_ref, mod_ref, g_ref, w_ref, cos_ref, sin_ref, gq_ref, gk_ref,
               qn_ref, kn_ref, qg_ref, ka_ref, kb_ref, vt_ref, *, tm):
    scale = HEAD_DIM ** -0.5 * LOG2E
    lane = lax.broadcasted_iota(jnp.int32, (ROWBLOCK, LANES), 1)
    low_head = lane < HEAD_DIM
    first_half = (lane & (HEAD_DIM - 1)) < HEAD_DIM // 2
    shift, gain = mod_ref[0:1, :], 1.0 + mod_ref[1:2, :]

    def hidden(rows):
        x = x_ref[rows, :]
        ms = jnp.mean(x * x, axis=-1, keepdims=True)
        return ((x * lax.rsqrt(ms + EPS) * g_ref[...]) * gain + shift).astype(BF16)

    def norm_rope(xg, head_gain, cos, sin):
        sq = xg * xg
        s_lo = jnp.sum(jnp.where(low_head, sq, 0.0), axis=-1, keepdims=True)
        s_hi = jnp.sum(jnp.where(low_head, 0.0, sq), axis=-1, keepdims=True)
        inv = lax.rsqrt(jnp.where(low_head, s_lo, s_hi) * (1.0 / HEAD_DIM) + EPS)
        yg = xg * inv * head_gain
        rot = jnp.where(first_half,
                        pltpu.roll(yg, LANES - HEAD_DIM // 2, 1),
                        pltpu.roll(yg, HEAD_DIM // 2, 1))
        return yg * cos + rot * sin

    subs = [pl.ds(r * ROWBLOCK, ROWBLOCK) for r in range(tm // ROWBLOCK)]
    hs = [hidden(rows) for rows in subs]
    accs = [[jnp.dot(hb, w_ref[:, c0:c1], preferred_element_type=F32) for c0, c1 in IN_GROUPS]
            for hb in hs]
    for r, rows in enumerate(subs):
        a_gqa, a_vn, a_na = accs[r]
        cos, sin = cos_ref[rows, :], sin_ref[rows, :]
        for g in range(GQA_WIDTH // LANES):
            cols = slice(COL_QG + g * LANES, COL_QG + (g + 1) * LANES)
            qg_ref[rows, g * LANES:(g + 1) * LANES] = (
                norm_rope(a_gqa[:, cols], gq_ref[...], cos, sin) * scale).astype(BF16)
        kr = norm_rope(a_gqa[:, COL_KG:COL_KG + KV_WIDTH], gk_ref[...], cos, sin)
        ka_ref[rows, :] = kr.astype(BF16)
        kb_ref[rows, :] = pltpu.roll(kr, HEAD_DIM, 1).astype(BF16)
        for j in range(NA_WIDTH // LANES):
            vt_ref[r, j * LANES:(j + 1) * LANES, :] = a_vn[:, j * LANES:(j + 1) * LANES].T.astype(BF16)
        vt_ref[r, NA_WIDTH:V_WIDTH, :] = a_gqa[:, COL_VG:COL_VG + KV_WIDTH].T.astype(BF16)
        qn_ref[rows, :] = (a_na[:, 0:NA_WIDTH] * scale).astype(BF16)
        kn_ref[rows, :] = a_na[:, NA_WIDTH:2 * NA_WIDTH].astype(BF16)


def _in_proj(x, mod, g_attn, w_in, cos_t, sin_t, gq_t, gk_t, tm):
    B, T, _ = x.shape
    nrb = T // ROWBLOCK
    row = lambda b, i: (b, i, 0)
    const2 = lambda b, i: (0, 0)
    out_shape = (
        jax.ShapeDtypeStruct((B, T, NA_WIDTH), BF16),
        jax.ShapeDtypeStruct((B, T, NA_WIDTH), BF16),
        jax.ShapeDtypeStruct((B, T, GQA_WIDTH), BF16),
        jax.ShapeDtypeStruct((B, T, KV_WIDTH), BF16),
        jax.ShapeDtypeStruct((B, T, KV_WIDTH), BF16),
        jax.ShapeDtypeStruct((B, nrb, V_WIDTH, ROWBLOCK), BF16),
    )
    return pl.pallas_call(
        functools.partial(_in_kernel, tm=tm),
        out_shape=out_shape,
        grid=(B, T // tm),
        in_specs=[
            pl.BlockSpec((None, tm, D_MODEL), row),
            pl.BlockSpec((None, N_MOD, D_MODEL), lambda b, i: (b, 0, 0)),
            pl.BlockSpec((1, D_MODEL), const2),
            pl.BlockSpec((D_MODEL, IN_WIDTH), const2, pipeline_mode=pl.Buffered(1)),
            pl.BlockSpec((tm, LANES), lambda b, i: (i, 0)),
            pl.BlockSpec((tm, LANES), lambda b, i: (i, 0)),
            pl.BlockSpec((1, LANES), const2),
            pl.BlockSpec((1, LANES), const2),
        ],
        out_specs=(
            pl.BlockSpec((None, tm, NA_WIDTH), row),
            pl.BlockSpec((None, tm, NA_WIDTH), row),
            pl.BlockSpec((None, tm, GQA_WIDTH), row),
            pl.BlockSpec((None, tm, KV_WIDTH), row),
            pl.BlockSpec((None, tm, KV_WIDTH), row),
            pl.BlockSpec((None, tm // ROWBLOCK, V_WIDTH, ROWBLOCK), lambda b, i: (b, i, 0, 0)),
        ),
        compiler_params=pltpu.CompilerParams(
            dimension_semantics=("arbitrary", "arbitrary"), vmem_limit_bytes=VMEM_LIMIT),
        name="in_proj",
    )(x, mod, g_attn, w_in, cos_t, sin_t, gq_t, gk_t)


def _na_variant(variant, i, qr):
    if variant == 0:
        valid, di = i < NA_WIN_H, i - qr
    elif variant == 1:
        valid, di = 0 <= i - qr < NA_WIN_H, i - qr - ROWBLOCK_ROWS
    else:
        valid, di = ROWBLOCK_ROWS <= i < ROWBLOCK_ROWS + NA_WIN_H, i - qr - 2 * ROWBLOCK_ROWS
    return di + NA_WIN_H - 1 if valid else None


def _na_kernel(rpb_ref, q_ref, k_ref, vt_ref, o_ref, cols_ref, tbl_ref, s0_ref, s1_ref, *, n_rb):
    n_row_off, n_col_off = 2 * NA_WIN_H - 1, 2 * NA_WIN_W - 1

    @pl.when(pl.program_id(1) == 0)
    def _build_bias():
        lane = lax.broadcasted_iota(jnp.int32, (GRID_W, LANES), 1)
        kc = lax.broadcasted_iota(jnp.int32, (GRID_W, LANES), 0)
        qc = lane & (GRID_W - 1)
        win = jnp.clip(qc - NA_WIN_W // 2, 0, GRID_W - NA_WIN_W)
        col_off = jnp.where((kc >= win) & (kc < win + NA_WIN_W), kc - qc + NA_WIN_W - 1, -1)
        hits = [col_off == o for o in range(n_col_off)]
        for hh in range(2):
            head = 2 * pl.program_id(0) + hh
            for r in range(n_row_off):
                t = jnp.full((GRID_W, LANES), MASKED, F32)
                for o in range(n_col_off):
                    t = jnp.where(hits[o], rpb_ref[head, r * n_col_off + o], t)
                cols_ref[hh, r] = t * LOG2E
        left = lane < GRID_W
        masked = jnp.full((GRID_W, LANES), MASKED, F32)
        for variant in range(3):
            for hh in range(2):
                for i in range(BAND_BLOCKS * ROWBLOCK_ROWS):
                    for qp in range(ROWBLOCK_ROWS // 2):
                        ro = [_na_variant(variant, i, 2 * qp + e) for e in range(2)]
                        t = [masked if r is None else cols_ref[hh, r] for r in ro]
                        tbl_ref[variant, hh, i * GRID_W:(i + 1) * GRID_W,
                                qp * LANES:(qp + 1) * LANES] = jnp.where(left, t[0], t[1])

    feat = lax.broadcasted_iota(jnp.int32, (LANES, ROWBLOCK), 0)
    s_refs = (s0_ref, s1_ref)

    def band_start(rb):
        return min(max(rb - 1, 0), n_rb - BAND_BLOCKS)

    def scores(rb, slot):
        variant = 0 if rb == 0 else (2 if rb == n_rb - 1 else 1)
        q = q_ref[rb * ROWBLOCK:(rb + 1) * ROWBLOCK, :]
        kband = k_ref[band_start(rb) * ROWBLOCK:band_start(rb) * ROWBLOCK + BAND, :]
        qt = q.astype(F32).T
        for hh in range(2):
            in_head = (feat < HEAD_DIM) if hh == 0 else (feat >= HEAD_DIM)
            qh = jnp.where(in_head, qt, 0.0).astype(BF16)
            s_refs[slot][hh] = (jnp.dot(kband, qh, preferred_element_type=F32)
                                + tbl_ref[variant, hh])

    def softmax_pv(rb, slot):
        bs = band_start(rb)
        outs = []
        for hh in range(2):
            m = jnp.max(s_refs[slot][hh], axis=0, keepdims=True)
            pb = jnp.exp2(s_refs[slot][hh] - m).astype(BF16)
            o_aug = jnp.zeros((V_AUG, ROWBLOCK), F32)
            for j in range(BAND_BLOCKS):
                vt = _with_ones(vt_ref[bs + j, hh * HEAD_DIM:(hh + 1) * HEAD_DIM, :])
                o_aug = o_aug + jnp.dot(vt, pb[j * ROWBLOCK:(j + 1) * ROWBLOCK, :],
                                        preferred_element_type=F32)
            outs.append(o_aug[0:HEAD_DIM] / o_aug[HEAD_DIM:HEAD_DIM + 1])
        o_pair = jnp.concatenate(outs, axis=0)
        o_ref[rb * ROWBLOCK:(rb + 1) * ROWBLOCK, :] = o_pair.T.astype(BF16)

    scores(0, 0)
    for rb in range(n_rb):
        if rb + 1 < n_rb:
            scores(rb + 1, (rb + 1) % 2)
        softmax_pv(rb, rb % 2)


def _na_attention(rpb, qn, kn, vt):
    B, T, _ = qn.shape
    n_rb = T // ROWBLOCK
    n_pairs = NA_HEADS // 2
    col = lambda p, b: (b, 0, p)
    return pl.pallas_call(
        functools.partial(_na_kernel, n_rb=n_rb),
        out_shape=jax.ShapeDtypeStruct((B, T, NA_WIDTH), BF16),
        grid=(n_pairs, B),
        in_specs=[
            pl.BlockSpec(memory_space=pltpu.SMEM),
            pl.BlockSpec((None, T, LANES), col),
            pl.BlockSpec((None, T, LANES), col),
            pl.BlockSpec((None, n_rb, LANES, ROWBLOCK), lambda p, b: (b, 0, p, 0)),
        ],
        out_specs=pl.BlockSpec((None, T, LANES), col),
        scratch_shapes=[pltpu.VMEM((2, 2 * NA_WIN_H - 1, GRID_W, LANES), F32),
                        pltpu.VMEM((3, 2, BAND, ROWBLOCK), F32),
                        pltpu.VMEM((2, BAND, ROWBLOCK), F32),
                        pltpu.VMEM((2, BAND, ROWBLOCK), F32)],
        compiler_params=pltpu.CompilerParams(
            dimension_semantics=("arbitrary", "arbitrary"), vmem_limit_bytes=VMEM_LIMIT),
        name="na_attn",
    )(rpb.reshape(NA_HEADS, -1), qn, kn, vt)


def _gqa_kernel(q_ref, ka_ref, kb_ref, vt_ref, o_ref, qm_ref, s0_ref, s1_ref, m_ref, acc_ref, *,
                tq, n_chunks, unroll):
    feat = lax.broadcasted_iota(jnp.int32, (LANES, tq), 0)
    for g in range(GQA_HEADS // 2):
        q2t = q_ref[:, g * LANES:(g + 1) * LANES].astype(F32).T
        for half in range(2):
            in_head = (feat < HEAD_DIM) if half == 0 else (feat >= HEAD_DIM)
            qm_ref[2 * g + half] = jnp.where(in_head, q2t, 0.0).astype(BF16)
    m_ref[...] = jnp.full(m_ref.shape, MASKED, F32)
    acc_ref[...] = jnp.zeros(acc_ref.shape, F32)

    s_refs = (s0_ref, s1_ref)

    def scores(h, c, slot):
        kvh, half = h // GQA_GROUP, h % 2
        k_ref = ka_ref if half == kvh else kb_ref
        rows = pl.ds(pl.multiple_of(c * ROWBLOCK, ROWBLOCK), ROWBLOCK)
        s_refs[slot][h] = jnp.dot(k_ref[rows, :], qm_ref[h], preferred_element_type=F32)

    def softmax_pv(h, slot, vt_aug):
        s = s_refs[slot][h]
        m_old = m_ref[h]
        m_new = jnp.maximum(m_old, jnp.max(s, axis=0, keepdims=True))
        alpha = jnp.exp2(m_old - m_new)
        pb = jnp.exp2(s - m_new).astype(BF16)
        acc_ref[h] = alpha * acc_ref[h] + jnp.dot(vt_aug, pb, preferred_element_type=F32)
        m_ref[h] = m_new

    def step(c, slot, prefetch):
        vt_aug = [_with_ones(vt_ref[c, kvh * HEAD_DIM:(kvh + 1) * HEAD_DIM, :])
                  for kvh in range(GQA_KV_HEADS)]
        if prefetch:
            scores(0, c + 1, 1 - slot)
        for h in range(GQA_HEADS):
            if prefetch and h + 1 < GQA_HEADS:
                scores(h + 1, c + 1, 1 - slot)
            softmax_pv(h, slot, vt_aug[h // GQA_GROUP])

    for h in range(GQA_HEADS):
        scores(h, 0, 0)

    def chunk_group(j, carry):
        for i in range(unroll):
            step(unroll * j + i, i % 2, True)
        return carry

    assert n_chunks % unroll == 0 and unroll % 2 == 0
    lax.fori_loop(0, n_chunks // unroll - 1, chunk_group, 0)
    for c in range(n_chunks - unroll, n_chunks):
        step(c, c % 2, c + 1 < n_chunks)

    for g in range(GQA_HEADS // 2):
        pair = []
        for e in range(2):
            a = acc_ref[2 * g + e]
            pair.append(a[0:HEAD_DIM] / a[HEAD_DIM:HEAD_DIM + 1])
        o_ref[:, g * LANES:(g + 1) * LANES] = jnp.concatenate(pair, axis=0).T.astype(BF16)


def _gqa_attention(qg, ka, kb, vt, tq, unroll):
    B, T, _ = qg.shape
    n_chunks = T // ROWBLOCK
    kv_block = NA_WIDTH // KV_WIDTH
    return pl.pallas_call(
        functools.partial(_gqa_kernel, tq=tq, n_chunks=n_chunks, unroll=unroll),
        out_shape=jax.ShapeDtypeStruct((B, T, GQA_WIDTH), BF16),
        grid=(B, T // tq),
        in_specs=[
            pl.BlockSpec((None, tq, GQA_WIDTH), lambda b, i: (b, i, 0)),
            pl.BlockSpec((None, T, KV_WIDTH), lambda b, i: (b, 0, 0)),
            pl.BlockSpec((None, T, KV_WIDTH), lambda b, i: (b, 0, 0)),
            pl.BlockSpec((None, n_chunks, KV_WIDTH, ROWBLOCK), lambda b, i: (b, 0, kv_block, 0)),
        ],
        out_specs=pl.BlockSpec((None, tq, GQA_WIDTH), lambda b, i: (b, i, 0)),
        scratch_shapes=[
            pltpu.VMEM((GQA_HEADS, LANES, tq), BF16),
            pltpu.VMEM((GQA_HEADS, ROWBLOCK, tq), F32),
            pltpu.VMEM((GQA_HEADS, ROWBLOCK, tq), F32),
            pltpu.VMEM((GQA_HEADS, 1, tq), F32),
            pltpu.VMEM((GQA_HEADS, V_AUG, tq), F32),
        ],
        compiler_params=pltpu.CompilerParams(
            dimension_semantics=("arbitrary", "arbitrary"), vmem_limit_bytes=VMEM_LIMIT),
        name="gqa_attn",
    )(qg, ka, kb, vt)


def _post_kernel(x_ref, ona_ref, og_ref, mod_ref, wo_ref, gf_ref, wg_ref, wu_ref, wd_ref,
                 gfin_ref, o_ref, *, tm):
    gate_a, shift_f, gain_f, gate_f = (mod_ref[2:3, :], mod_ref[3:4, :], 1.0 + mod_ref[4:5, :],
                                       mod_ref[5:6, :])

    def rms(x):
        return x * lax.rsqrt(jnp.mean(x * x, axis=-1, keepdims=True) + EPS)

    subs = [pl.ds(r * ROWBLOCK, ROWBLOCK) for r in range(tm // ROWBLOCK)]
    attn = [jnp.dot(ona_ref[rows, :], wo_ref[0:NA_WIDTH, :], preferred_element_type=F32)
            + jnp.dot(og_ref[rows, :], wo_ref[NA_WIDTH:, :], preferred_element_type=F32)
            for rows in subs]
    x1 = [x_ref[rows, :] + gate_a * a for rows, a in zip(subs, attn)]
    hb = [((rms(v) * gf_ref[...]) * gain_f + shift_f).astype(BF16) for v in x1]
    gu = [(jnp.dot(h, wg_ref[...], preferred_element_type=F32),
           jnp.dot(h, wu_ref[...], preferred_element_type=F32)) for h in hb]
    ff = [(_silu(g) * u).astype(BF16) for g, u in gu]
    x2 = [v + gate_f * jnp.dot(f, wd_ref[...], preferred_element_type=F32) for v, f in zip(x1, ff)]
    for rows, v in zip(subs, x2):
        o_ref[rows, :] = rms(v) * gfin_ref[...]


def _post(x, o_na, o_g, mod, w_o, g_ffn, w_gate, w_up, w_down, g_final, tm):
    B, T, _ = x.shape
    d_ff = w_gate.shape[1]
    row = lambda b, i: (b, i, 0)
    const2 = lambda b, i: (0, 0)
    resident = pl.Buffered(1)
    return pl.pallas_call(
        functools.partial(_post_kernel, tm=tm),
        out_shape=jax.ShapeDtypeStruct((B, T, D_MODEL), F32),
        grid=(B, T // tm),
        in_specs=[
            pl.BlockSpec((None, tm, D_MODEL), row),
            pl.BlockSpec((None, tm, NA_WIDTH), row),
            pl.BlockSpec((None, tm, GQA_WIDTH), row),
            pl.BlockSpec((None, N_MOD, D_MODEL), lambda b, i: (b, 0, 0)),
            pl.BlockSpec((D_MODEL, D_MODEL), const2, pipeline_mode=resident),
            pl.BlockSpec((1, D_MODEL), const2),
            pl.BlockSpec((D_MODEL, d_ff), const2, pipeline_mode=resident),
            pl.BlockSpec((D_MODEL, d_ff), const2, pipeline_mode=resident),
            pl.BlockSpec((d_ff, D_MODEL), const2, pipeline_mode=resident),
            pl.BlockSpec((1, D_MODEL), const2),
        ],
        out_specs=pl.BlockSpec((None, tm, D_MODEL), row),
        compiler_params=pltpu.CompilerParams(
            dimension_semantics=("arbitrary", "arbitrary"), vmem_limit_bytes=VMEM_LIMIT),
        name="post",
    )(x, o_na, o_g, mod, w_o, g_ffn, w_gate, w_up, w_down, g_final)


def _rope_tables(n_tokens):
    t = jnp.arange(n_tokens)
    row = (t // GRID_W).astype(F32)
    col = (t % GRID_W).astype(F32)

    def angles(pos, dims):
        inv = ROPE_THETA ** (-jnp.arange(0, dims, 2, dtype=F32) / dims)
        return pos[:, None] * inv[None, :]

    ang = jnp.concatenate([angles(row, HEAD_DIM // 2), angles(col, HEAD_DIM // 2)], axis=-1)
    cos, sin = jnp.cos(ang), jnp.sin(ang)
    reps = LANES // HEAD_DIM
    return (jnp.tile(jnp.concatenate([cos, cos], axis=-1), (1, reps)),
            jnp.tile(jnp.concatenate([-sin, sin], axis=-1), (1, reps)))


def _permute_in_columns(w):
    q_na, k_na, v_na, q_g, k_g, v_g = jnp.split(
        w, (NA_WIDTH, 2 * NA_WIDTH, 3 * NA_WIDTH, 3 * NA_WIDTH + GQA_WIDTH,
            3 * NA_WIDTH + GQA_WIDTH + KV_WIDTH), axis=-1)
    return jnp.concatenate([q_g, k_g, v_g, v_na, q_na, k_na], axis=-1)


def kernel(x, c, w_ada, b_ada, g_attn, w_in, g_q, g_k, rpb, w_o, g_ffn, w_gate, w_up, w_down, g_final):
    B, T, _ = x.shape
    assert w_ada.shape[0] == 1, "single-layer block: the final norm is fused into the layer"
    cos_t, sin_t = _rope_tables(T)
    reps = LANES // HEAD_DIM
    mod = _ada(c, w_ada[0], b_ada[0]).reshape(B, N_MOD, D_MODEL)
    qn, kn, qg, ka, kb, vt = _in_proj(
        x, mod, g_attn[0].reshape(1, D_MODEL), _permute_in_columns(w_in[0]).astype(BF16),
        cos_t, sin_t,
        jnp.tile(g_q[0], reps).reshape(1, LANES), jnp.tile(g_k[0], reps).reshape(1, LANES),
        tm=512)
    o_na = _na_attention(rpb[0], qn, kn, vt)
    o_g = _gqa_attention(qg, ka, kb, vt, tq=256, unroll=4)
    return _post(x, o_na, o_g, mod, w_o[0].astype(BF16), g_ffn[0].reshape(1, D_MODEL),
                 w_gate[0].astype(BF16), w_up[0].astype(BF16), w_down[0].astype(BF16),
                 g_final.reshape(1, D_MODEL), tm=512)
```

```python
import functools

import jax
import jax.numpy as jnp
from jax import lax
from jax.experimental import pallas as pl
from jax.experimental.pallas import tpu as pltpu

F32 = jnp.float32
BF16 = jnp.bfloat16

D_MODEL = 1024
HEAD_DIM = 64
NA_HEADS = 8
GQA_HEADS = 8
GQA_KV_HEADS = 2
GQA_GROUP = GQA_HEADS // GQA_KV_HEADS
NA_WIDTH = NA_HEADS * HEAD_DIM
GQA_WIDTH = GQA_HEADS * HEAD_DIM
KV_WIDTH = GQA_KV_HEADS * HEAD_DIM
IN_WIDTH = 3 * NA_WIDTH + GQA_WIDTH + 2 * KV_WIDTH
V_WIDTH = NA_WIDTH + KV_WIDTH
GRID_W = 64
NA_WIN_H = 8
NA_WIN_W = 16
ROPE_THETA = 10000.0
N_MOD = 6
EPS = 1e-6

LANES = 128
BF16_SUBLANES = 16
ROWBLOCK_ROWS = 4
ROWBLOCK = ROWBLOCK_ROWS * GRID_W
BAND_BLOCKS = 3
BAND = BAND_BLOCKS * ROWBLOCK
V_AUG = HEAD_DIM + BF16_SUBLANES
MASKED = -1e30
LOG2E = 1.4426950408889634
VMEM_LIMIT = 56 * 1024 * 1024

COL_QG, COL_KG, COL_VG, COL_VN, COL_QN, COL_KN = 0, 512, 640, 768, 1280, 1792
IN_GROUPS = ((0, COL_VN), (COL_VN, COL_QN), (COL_QN, IN_WIDTH))


def _silu(x):
    return x / (1.0 + jnp.exp(-x))


def _with_ones(vt):
    return jnp.concatenate([vt, jnp.ones((BF16_SUBLANES, vt.shape[1]), vt.dtype)], axis=0)


def _ada_kernel(c_ref, w_ref, b_ref, o_ref):
    ca = _silu(c_ref[...])
    o_ref[...] = jnp.dot(ca.astype(BF16), w_ref[...].astype(BF16),
                         preferred_element_type=F32) + b_ref[...]


def _ada(c, w_ada, b_ada):
    B = c.shape[0]
    n = w_ada.shape[1]
    return pl.pallas_call(
        _ada_kernel,
        out_shape=jax.ShapeDtypeStruct((B, n), F32),
        grid=(n // D_MODEL,),
        in_specs=[pl.BlockSpec((B, D_MODEL), lambda j: (0, 0)),
                  pl.BlockSpec((D_MODEL, D_MODEL), lambda j: (0, j)),
                  pl.BlockSpec((1, D_MODEL), lambda j: (0, j))],
        out_specs=pl.BlockSpec((B, D_MODEL), lambda j: (0, j)),
        compiler_params=pltpu.CompilerParams(dimension_semantics=("arbitrary",)),
        name="ada",
    )(c, w_ada, b_ada.reshape(1, n))


def _in_kernel(x_ref, mod_ref, g_ref, w_ref, cos_ref, sin_ref, gq_ref, gk_ref,
               qn_ref, kn_ref, qg_ref, ka_ref, kb_ref, vt_ref, *, tm):
    scale = HEAD_DIM ** -0.5 * LOG2E
    lane = lax.broadcasted_iota(jnp.int32, (ROWBLOCK, LANES), 1)
    low_head = lane < HEAD_DIM
    first_half = (lane & (HEAD_DIM - 1)) < HEAD_DIM // 2
    shift, gain = mod_ref[0:1, :], 1.0 + mod_ref[1:2, :]

    def hidden(rows):
        x = x_ref[rows, :]
        ms = jnp.mean(x * x, axis=-1, keepdims=True)
        return ((x * lax.rsqrt(ms + EPS) * g_ref[...]) * gain + shift).astype(BF16)

    def norm_rope(xg, head_gain, cos, sin):
        sq = xg * xg
        s_lo = jnp.sum(jnp.where(low_head, sq, 0.0), axis=-1, keepdims=True)
        s_hi = jnp.sum(jnp.where(low_head, 0.0, sq), axis=-1, keepdims=True)
        inv = lax.rsqrt(jnp.where(low_head, s_lo, s_hi) * (1.0 / HEAD_DIM) + EPS)
        yg = xg * inv * head_gain
        rot = jnp.where(first_half,
                        pltpu.roll(yg, LANES - HEAD_DIM // 2, 1),
                        pltpu.roll(yg, HEAD_DIM // 2, 1))
        return yg * cos + rot * sin

    subs = [pl.ds(r * ROWBLOCK, ROWBLOCK) for r in range(tm // ROWBLOCK)]
    hs = [hidden(rows) for rows in subs]
    accs = [[jnp.dot(hb, w_ref[:, c0:c1], preferred_element_type=F32) for c0, c1 in IN_GROUPS]
            for hb in hs]
    for r, rows in enumerate(subs):
        a_gqa, a_vn, a_na = accs[r]
        cos, sin = cos_ref[rows, :], sin_ref[rows, :]
        for g in range(GQA_WIDTH // LANES):
            cols = slice(COL_QG + g * LANES, COL_QG + (g + 1) * LANES)
            qg_ref[rows, g * LANES:(g + 1) * LANES] = (
                norm_rope(a_gqa[:, cols], gq_ref[...], cos, sin) * scale).astype(BF16)
        kr = norm_rope(a_gqa[:, COL_KG:COL_KG + KV_WIDTH], gk_ref[...], cos, sin)
        ka_ref[rows, :] = kr.astype(BF16)
        kb_ref[rows, :] = pltpu.roll(kr, HEAD_DIM, 1).astype(BF16)
        for j in range(NA_WIDTH // LANES):
            vt_ref[r, j * LANES:(j + 1) * LANES, :] = a_vn[:, j * LANES:(j + 1) * LANES].T.astype(BF16)
        vt_ref[r, NA_WIDTH:V_WIDTH, :] = a_gqa[:, COL_VG:COL_VG + KV_WIDTH].T.astype(BF16)
        qn_ref[rows, :] = (a_na[:, 0:NA_WIDTH] * scale).astype(BF16)
        kn_ref[rows, :] = a_na[:, NA_WIDTH:2 * NA_WIDTH].astype(BF16)


def _in_proj(x, mod, g_attn, w_in, cos_t, sin_t, gq_t, gk_t, tm):
    B, T, _ = x.shape
    nrb = T // ROWBLOCK
    row = lambda b, i: (b, i, 0)
    const2 = lambda b, i: (0, 0)
    out_shape = (
        jax.ShapeDtypeStruct((B, T, NA_WIDTH), BF16),
        jax.ShapeDtypeStruct((B, T, NA_WIDTH), BF16),
        jax.ShapeDtypeStruct((B, T, GQA_WIDTH), BF16),
        jax.ShapeDtypeStruct((B, T, KV_WIDTH), BF16),
        jax.ShapeDtypeStruct((B, T, KV_WIDTH), BF16),
        jax.ShapeDtypeStruct((B, nrb, V_WIDTH, ROWBLOCK), BF16),
    )
    return pl.pallas_call(
        functools.partial(_in_kernel, tm=tm),
        out_shape=out_shape,
        grid=(B, T // tm),
        in_specs=[
            pl.BlockSpec((None, tm, D_MODEL), row),
            pl.BlockSpec((None, N_MOD, D_MODEL), lambda b, i: (b, 0, 0)),
            pl.BlockSpec((1, D_MODEL), const2),
            pl.BlockSpec((D_MODEL, IN_WIDTH), const2, pipeline_mode=pl.Buffered(1)),
            pl.BlockSpec((tm, LANES), lambda b, i: (i, 0)),
            pl.BlockSpec((tm, LANES), lambda b, i: (i, 0)),
            pl.BlockSpec((1, LANES), const2),
            pl.BlockSpec((1, LANES), const2),
        ],
        out_specs=(
            pl.BlockSpec((None, tm, NA_WIDTH), row),
            pl.BlockSpec((None, tm, NA_WIDTH), row),
            pl.BlockSpec((None, tm, GQA_WIDTH), row),
            pl.BlockSpec((None, tm, KV_WIDTH), row),
            pl.BlockSpec((None, tm, KV_WIDTH), row),
            pl.BlockSpec((None, tm // ROWBLOCK, V_WIDTH, ROWBLOCK), lambda b, i: (b, i, 0, 0)),
        ),
        compiler_params=pltpu.CompilerParams(
            dimension_semantics=("arbitrary", "arbitrary"), vmem_limit_bytes=VMEM_LIMIT),
        name="in_proj",
    )(x, mod, g_attn, w_in, cos_t, sin_t, gq_t, gk_t)


def _na_variant(variant, i, qr):
    if variant == 0:
        valid, di = i < NA_WIN_H, i - qr
    elif variant == 1:
        valid, di = 0 <= i - qr < NA_WIN_H, i - qr - ROWBLOCK_ROWS
    else:
        valid, di = ROWBLOCK_ROWS <= i < ROWBLOCK_ROWS + NA_WIN_H, i - qr - 2 * ROWBLOCK_ROWS
    return di + NA_WIN_H - 1 if valid else None


def _na_key_ranges(variant):
    n_rows = BAND_BLOCKS * ROWBLOCK_ROWS
    halves = []
    for e in range(ROWBLOCK_ROWS // 2):
        rows = [i for i in range(n_rows)
                if any(_na_variant(variant, i, 2 * e + d) is not None for d in range(2))]
        halves.append((rows[0] * GRID_W, (rows[-1] + 1) * GRID_W))
    return (min(r0 for r0, _ in halves), max(r1 for _, r1 in halves)), halves


def _na_kernel(rpb_ref, q_ref, k_ref, vt_ref, o_ref, cols_ref, tbl_ref, s0_ref, s1_ref, *, n_rb):
    n_row_off, n_col_off = 2 * NA_WIN_H - 1, 2 * NA_WIN_W - 1

    @pl.when(pl.program_id(1) == 0)
    def _build_bias():
        lane = lax.broadcasted_iota(jnp.int32, (GRID_W, LANES), 1)
        kc = lax.broadcasted_iota(jnp.int32, (GRID_W, LANES), 0)
        qc = lane & (GRID_W - 1)
        win = jnp.clip(qc - NA_WIN_W // 2, 0, GRID_W - NA_WIN_W)
        col_off = jnp.where((kc >= win) & (kc < win + NA_WIN_W), kc - qc + NA_WIN_W - 1, -1)
        hits = [col_off == o for o in range(n_col_off)]
        for hh in range(2):
            head = 2 * pl.program_id(0) + hh
            for r in range(n_row_off):
                t = jnp.full((GRID_W, LANES), MASKED, F32)
                for o in range(n_col_off):
                    t = jnp.where(hits[o], rpb_ref[head, r * n_col_off + o], t)
                cols_ref[hh, r] = t * LOG2E
        left = lane < GRID_W
        masked = jnp.full((GRID_W, LANES), MASKED, F32)
        for variant in range(3):
            for hh in range(2):
                for i in range(BAND_BLOCKS * ROWBLOCK_ROWS):
                    for qp in range(ROWBLOCK_ROWS // 2):
                        ro = [_na_variant(variant, i, 2 * qp + e) for e in range(2)]
                        t = [masked if r is None else cols_ref[hh, r] for r in ro]
                        tbl_ref[variant, hh, i * GRID_W:(i + 1) * GRID_W,
                                qp * LANES:(qp + 1) * LANES] = jnp.where(left, t[0], t[1])

    feat = lax.broadcasted_iota(jnp.int32, (LANES, ROWBLOCK), 0)
    s_refs = (s0_ref, s1_ref)

    def band_start(rb):
        return min(max(rb - 1, 0), n_rb - BAND_BLOCKS)

    def variant_of(rb):
        return 0 if rb == 0 else (2 if rb == n_rb - 1 else 1)

    def scores(rb, slot):
        variant = variant_of(rb)
        (u0, u1), _ = _na_key_ranges(variant)
        k0 = band_start(rb) * ROWBLOCK
        q = q_ref[rb * ROWBLOCK:(rb + 1) * ROWBLOCK, :]
        kband = k_ref[k0 + u0:k0 + u1, :]
        qt = q.astype(F32).T
        for hh in range(2):
            in_head = (feat < HEAD_DIM) if hh == 0 else (feat >= HEAD_DIM)
            qh = jnp.where(in_head, qt, 0.0).astype(BF16)
            s_refs[slot][hh, 0:u1 - u0] = (jnp.dot(kband, qh, preferred_element_type=F32)
                                           + tbl_ref[variant, hh, u0:u1])

    def softmax_pv(rb, slot):
        bs = band_start(rb)
        (u0, u1), halves = _na_key_ranges(variant_of(rb))
        outs = []
        for hh in range(2):
            cols = []
            for e, (r0, r1) in enumerate(halves):
                sl = (hh, slice(r0 - u0, r1 - u0), slice(e * LANES, (e + 1) * LANES))
                m = jnp.max(s_refs[slot][sl], axis=0, keepdims=True)
                p = jnp.exp2(s_refs[slot][sl] - m).astype(BF16)
                pieces = [jnp.zeros((r0, LANES), BF16), p, jnp.zeros((BAND - r1, LANES), BF16)]
                cols.append(jnp.concatenate([x for x in pieces if x.shape[0]], axis=0))
            pb = jnp.concatenate(cols, axis=1)
            o_aug = jnp.zeros((V_AUG, ROWBLOCK), F32)
            for j in range(u0 // ROWBLOCK, pl.cdiv(u1, ROWBLOCK)):
                vt = _with_ones(vt_ref[bs + j, hh * HEAD_DIM:(hh + 1) * HEAD_DIM, :])
                o_aug = o_aug + jnp.dot(vt, pb[j * ROWBLOCK:(j + 1) * ROWBLOCK, :],
                                        preferred_element_type=F32)
            outs.append(o_aug[0:HEAD_DIM] / o_aug[HEAD_DIM:HEAD_DIM + 1])
        o_pair = jnp.concatenate(outs, axis=0)
        o_ref[rb * ROWBLOCK:(rb + 1) * ROWBLOCK, :] = o_pair.T.astype(BF16)

    scores(0, 0)
    for rb in range(n_rb):
        if rb + 1 < n_rb:
            scores(rb + 1, (rb + 1) % 2)
        softmax_pv(rb, rb % 2)


def _na_attention(rpb, qn, kn, vt):
    B, T, _ = qn.shape
    n_rb = T // ROWBLOCK
    n_pairs = NA_HEADS // 2
    col = lambda p, b: (b, 0, p)
    return pl.pallas_call(
        functools.partial(_na_kernel, n_rb=n_rb),
        out_shape=jax.ShapeDtypeStruct((B, T, NA_WIDTH), BF16),
        grid=(n_pairs, B),
        in_specs=[
            pl.BlockSpec(memory_space=pltpu.SMEM),
            pl.BlockSpec((None, T, LANES), col),
            pl.BlockSpec((None, T, LANES), col),
            pl.BlockSpec((None, n_rb, LANES, ROWBLOCK), lambda p, b: (b, 0, p, 0)),
        ],
        out_specs=pl.BlockSpec((None, T, LANES), col),
        scratch_shapes=[pltpu.VMEM((2, 2 * NA_WIN_H - 1, GRID_W, LANES), F32),
                        pltpu.VMEM((3, 2, BAND, ROWBLOCK), F32),
                        pltpu.VMEM((2, BAND, ROWBLOCK), F32),
                        pltpu.VMEM((2, BAND, ROWBLOCK), F32)],
        compiler_params=pltpu.CompilerParams(
            dimension_semantics=("arbitrary", "arbitrary"), vmem_limit_bytes=VMEM_LIMIT),
        name="na_attn",
    )(rpb.reshape(NA_HEADS, -1), qn, kn, vt)


def _gqa_kernel(q_ref, ka_ref, kb_ref, vt_ref, o_ref, qm_ref, s0_ref, s1_ref, m_ref, acc_ref, *,
                tq, n_chunks, unroll):
    feat = lax.broadcasted_iota(jnp.int32, (LANES, tq), 0)
    for g in range(GQA_HEADS // 2):
        q2t = q_ref[:, g * LANES:(g + 1) * LANES].astype(F32).T
        for half in range(2):
            in_head = (feat < HEAD_DIM) if half == 0 else (feat >= HEAD_DIM)
            qm_ref[2 * g + half] = jnp.where(in_head, q2t, 0.0).astype(BF16)
    m_ref[...] = jnp.full(m_ref.shape, MASKED, F32)
    acc_ref[...] = jnp.zeros(acc_ref.shape, F32)

    s_refs = (s0_ref, s1_ref)

    def scores(h, c, slot):
        kvh, half = h // GQA_GROUP, h % 2
        k_ref = ka_ref if half == kvh else kb_ref
        rows = pl.ds(pl.multiple_of(c * ROWBLOCK, ROWBLOCK), ROWBLOCK)
        s_refs[slot][h] = jnp.dot(k_ref[rows, :], qm_ref[h], preferred_element_type=F32)

    def softmax_pv(h, slot, vt_aug):
        s = s_refs[slot][h]
        m_old = m_ref[h]
        m_new = jnp.maximum(m_old, jnp.max(s, axis=0, keepdims=True))
        alpha = jnp.exp2(m_old - m_new)
        pb = jnp.exp2(s - m_new).astype(BF16)
        acc_ref[h] = alpha * acc_ref[h] + jnp.dot(vt_aug, pb, preferred_element_type=F32)
        m_ref[h] = m_new

    def step(c, slot, prefetch):
        vt_aug = [_with_ones(vt_ref[c, kvh * HEAD_DIM:(kvh + 1) * HEAD_DIM, :])
                  for kvh in range(GQA_KV_HEADS)]
        if prefetch:
            scores(0, c + 1, 1 - slot)
        for h in range(GQA_HEADS):
            if prefetch and h + 1 < GQA_HEADS:
                scores(h + 1, c + 1, 1 - slot)
            softmax_pv(h, slot, vt_aug[h // GQA_GROUP])

    for h in range(GQA_HEADS):
        scores(h, 0, 0)

    def chunk_group(j, carry):
        for i in range(unroll):
            step(unroll * j + i, i % 2, True)
        return carry

    assert n_chunks % unroll == 0 and unroll % 2 == 0
    lax.fori_loop(0, n_chunks // unroll - 1, chunk_group, 0)
    for c in range(n_chunks - unroll, n_chunks):
        step(c, c % 2, c + 1 < n_chunks)

    for g in range(GQA_HEADS // 2):
        pair = []
        for e in range(2):
            a = acc_ref[2 * g + e]
            pair.append(a[0:HEAD_DIM] / a[HEAD_DIM:HEAD_DIM + 1])
        o_ref[:, g * LANES:(g + 1) * LANES] = jnp.concatenate(pair, axis=0).T.astype(BF16)


def _gqa_attention(qg, ka, kb, vt, tq, unroll):
    B, T, _ = qg.shape
    n_chunks = T // ROWBLOCK
    kv_block = NA_WIDTH // KV_WIDTH
    return pl.pallas_call(
        functools.partial(_gqa_kernel, tq=tq, n_chunks=n_chunks, unroll=unroll),
        out_shape=jax.ShapeDtypeStruct((B, T, GQA_WIDTH), BF16),
        grid=(B, T // tq),
        in_specs=[
            pl.BlockSpec((None, tq, GQA_WIDTH), lambda b, i: (b, i, 0)),
            pl.BlockSpec((None, T, KV_WIDTH), lambda b, i: (b, 0, 0)),
            pl.BlockSpec((None, T, KV_WIDTH), lambda b, i: (b, 0, 0)),
            pl.BlockSpec((None, n_chunks, KV_WIDTH, ROWBLOCK), lambda b, i: (b, 0, kv_block, 0)),
        ],
        out_specs=pl.BlockSpec((None, tq, GQA_WIDTH), lambda b, i: (b, i, 0)),
        scratch_shapes=[
            pltpu.VMEM((GQA_HEADS, LANES, tq), BF16),
            pltpu.VMEM((GQA_HEADS, ROWBLOCK, tq), F32),
            pltpu.VMEM((GQA_HEADS, ROWBLOCK, tq), F32),
            pltpu.VMEM((GQA_HEADS, 1, tq), F32),
            pltpu.VMEM((GQA_HEADS, V_AUG, tq), F32),
        ],
        compiler_params=pltpu.CompilerParams(
            dimension_semantics=("arbitrary", "arbitrary"), vmem_limit_bytes=VMEM_LIMIT),
        name="gqa_attn",
    )(qg, ka, kb, vt)


def _post_kernel(x_ref, ona_ref, og_ref, mod_ref, wo_ref, gf_ref, wg_ref, wu_ref, wd_ref,
                 gfin_ref, o_ref, *, tm):
    gate_a, shift_f, gain_f, gate_f = (mod_ref[2:3, :], mod_ref[3:4, :], 1.0 + mod_ref[4:5, :],
                                       mod_ref[5:6, :])

    def rms(x):
        return x * lax.rsqrt(jnp.mean(x * x, axis=-1, keepdims=True) + EPS)

    subs = [pl.ds(r * ROWBLOCK, ROWBLOCK) for r in range(tm // ROWBLOCK)]
    attn = [jnp.dot(ona_ref[rows, :], wo_ref[0:NA_WIDTH, :], preferred_element_type=F32)
            + jnp.dot(og_ref[rows, :], wo_ref[NA_WIDTH:, :], preferred_element_type=F32)
            for rows in subs]
    x1 = [x_ref[rows, :] + gate_a * a for rows, a in zip(subs, attn)]
    hb = [((rms(v) * gf_ref[...]) * gain_f + shift_f).astype(BF16) for v in x1]
    gu = [(jnp.dot(h, wg_ref[...], preferred_element_type=F32),
           jnp.dot(h, wu_ref[...], preferred_element_type=F32)) for h in hb]
    ff = [(_silu(g) * u).astype(BF16) for g, u in gu]
    x2 = [v + gate_f * jnp.dot(f, wd_ref[...], preferred_element_type=F32) for v, f in zip(x1, ff)]
    for rows, v in zip(subs, x2):
        o_ref[rows, :] = rms(v) * gfin_ref[...]


def _post(x, o_na, o_g, mod, w_o, g_ffn, w_gate, w_up, w_down, g_final, tm):
    B, T, _ = x.shape
    d_ff = w_gate.shape[1]
    row = lambda b, i: (b, i, 0)
    const2 = lambda b, i: (0, 0)
    resident = pl.Buffered(1)
    return pl.pallas_call(
        functools.partial(_post_kernel, tm=tm),
        out_shape=jax.ShapeDtypeStruct((B, T, D_MODEL), F32),
        grid=(B, T // tm),
        in_specs=[
            pl.BlockSpec((None, tm, D_MODEL), row),
            pl.BlockSpec((None, tm, NA_WIDTH), row),
            pl.BlockSpec((None, tm, GQA_WIDTH), row),
            pl.BlockSpec((None, N_MOD, D_MODEL), lambda b, i: (b, 0, 0)),
            pl.BlockSpec((D_MODEL, D_MODEL), const2, pipeline_mode=resident),
            pl.BlockSpec((1, D_MODEL), const2),
            pl.BlockSpec((D_MODEL, d_ff), const2, pipeline_mode=resident),
            pl.BlockSpec((D_MODEL, d_ff), const2, pipeline_mode=resident),
            pl.BlockSpec((d_ff, D_MODEL), const2, pipeline_mode=resident),
            pl.BlockSpec((1, D_MODEL), const2),
        ],
        out_specs=pl.BlockSpec((None, tm, D_MODEL), row),
        compiler_params=pltpu.CompilerParams(
            dimension_semantics=("arbitrary", "arbitrary"), vmem_limit_bytes=VMEM_LIMIT),
        name="post",
    )(x, o_na, o_g, mod, w_o, g_ffn, w_gate, w_up, w_down, g_final)


def _rope_tables(n_tokens):
    t = jnp.arange(n_tokens)
    row = (t // GRID_W).astype(F32)
    col = (t % GRID_W).astype(F32)

    def angles(pos, dims):
        inv = ROPE_THETA ** (-jnp.arange(0, dims, 2, dtype=F32) / dims)
        return pos[:, None] * inv[None, :]

    ang = jnp.concatenate([angles(row, HEAD_DIM // 2), angles(col, HEAD_DIM // 2)], axis=-1)
    cos, sin = jnp.cos(ang), jnp.sin(ang)
    reps = LANES // HEAD_DIM
    return (jnp.tile(jnp.concatenate([cos, cos], axis=-1), (1, reps)),
            jnp.tile(jnp.concatenate([-sin, sin], axis=-1), (1, reps)))


def _permute_in_columns(w):
    q_na, k_na, v_na, q_g, k_g, v_g = jnp.split(
        w, (NA_WIDTH, 2 * NA_WIDTH, 3 * NA_WIDTH, 3 * NA_WIDTH + GQA_WIDTH,
            3 * NA_WIDTH + GQA_WIDTH + KV_WIDTH), axis=-1)
    return jnp.concatenate([q_g, k_g, v_g, v_na, q_na, k_na], axis=-1)


def kernel(x, c, w_ada, b_ada, g_attn, w_in, g_q, g_k, rpb, w_o, g_ffn, w_gate, w_up, w_down, g_final):
    B, T, _ = x.shape
    assert w_ada.shape[0] == 1, "single-layer block: the final norm is fused into the layer"
    cos_t, sin_t = _rope_tables(T)
    reps = LANES // HEAD_DIM
    mod = _ada(c, w_ada[0], b_ada[0]).reshape(B, N_MOD, D_MODEL)
    qn, kn, qg, ka, kb, vt = _in_proj(
        x, mod, g_attn[0].reshape(1, D_MODEL), _permute_in_columns(w_in[0]).astype(BF16),
        cos_t, sin_t,
        jnp.tile(g_q[0], reps).reshape(1, LANES), jnp.tile(g_k[0], reps).reshape(1, LANES),
        tm=512)
    o_na = _na_attention(rpb[0], qn, kn, vt)
    o_g = _gqa_attention(qg, ka, kb, vt, tq=512, unroll=4)
    return _post(x, o_na, o_g, mod, w_o[0].astype(BF16), g_ffn[0].reshape(1, D_MODEL),
                 w_gate[0].astype(BF16), w_up[0].astype(BF16), w_down[0].astype(BF16),
                 g_final.reshape(1, D_MODEL), tm=512)
```

```python
import functools

import jax
import jax.numpy as jnp
import numpy as np
from jax import lax
from jax.experimental import pallas as pl
from jax.experimental.pallas import tpu as pltpu

F32 = jnp.float32
BF16 = jnp.bfloat16

D_MODEL = 1024
HEAD_DIM = 64
NA_HEADS = 8
GQA_HEADS = 8
GQA_KV_HEADS = 2
GQA_GROUP = GQA_HEADS // GQA_KV_HEADS
NA_WIDTH = NA_HEADS * HEAD_DIM
GQA_WIDTH = GQA_HEADS * HEAD_DIM
KV_WIDTH = GQA_KV_HEADS * HEAD_DIM
IN_WIDTH = 3 * NA_WIDTH + GQA_WIDTH + 2 * KV_WIDTH
V_WIDTH = NA_WIDTH + KV_WIDTH
GRID_W = 64
NA_WIN_H = 8
NA_WIN_W = 16
ROPE_THETA = 10000.0
N_MOD = 6
EPS = 1e-6

LANES = 128
BF16_SUBLANES = 16
ROWBLOCK_ROWS = 4
ROWBLOCK = ROWBLOCK_ROWS * GRID_W
BAND_BLOCKS = 3
BAND = BAND_BLOCKS * ROWBLOCK
V_AUG = HEAD_DIM + BF16_SUBLANES
MASKED = -1e30
LOG2E = 1.4426950408889634
VMEM_LIMIT = 56 * 1024 * 1024

COL_QG, COL_KG, COL_VG, COL_VN, COL_QN, COL_KN = 0, 512, 640, 768, 1280, 1792
IN_GROUPS = ((0, COL_VN), (COL_VN, COL_QN), (COL_QN, IN_WIDTH))


def _silu(x):
    return x / (1.0 + jnp.exp(-x))


def _with_ones(vt):
    return jnp.concatenate([vt, jnp.ones((BF16_SUBLANES, vt.shape[1]), vt.dtype)], axis=0)


def _ada_kernel(c_ref, w_ref, b_ref, o_ref):
    ca = _silu(c_ref[...])
    o_ref[...] = jnp.dot(ca.astype(BF16), w_ref[...].astype(BF16),
                         preferred_element_type=F32) + b_ref[...]


def _ada(c, w_ada, b_ada):
    B = c.shape[0]
    n = w_ada.shape[1]
    return pl.pallas_call(
        _ada_kernel,
        out_shape=jax.ShapeDtypeStruct((B, n), F32),
        grid=(n // D_MODEL,),
        in_specs=[pl.BlockSpec((B, D_MODEL), lambda j: (0, 0)),
                  pl.BlockSpec((D_MODEL, D_MODEL), lambda j: (0, j)),
                  pl.BlockSpec((1, D_MODEL), lambda j: (0, j))],
        out_specs=pl.BlockSpec((B, D_MODEL), lambda j: (0, j)),
        compiler_params=pltpu.CompilerParams(dimension_semantics=("arbitrary",)),
        name="ada",
    )(c, w_ada, b_ada.reshape(1, n))


def _in_kernel(x_ref, mod_ref, g_ref, w_ref, cos_ref, sin_ref, gq_ref, gk_ref,
               qn_ref, kn_ref, qg_ref, ka_ref, kb_ref, vt_ref, *, tm):
    scale = HEAD_DIM ** -0.5 * LOG2E
    lane = lax.broadcasted_iota(jnp.int32, (ROWBLOCK, LANES), 1)
    low_head = lane < HEAD_DIM
    first_half = (lane & (HEAD_DIM - 1)) < HEAD_DIM // 2
    shift, gain = mod_ref[0:1, :], 1.0 + mod_ref[1:2, :]

    def hidden(rows):
        x = x_ref[rows, :]
        ms = jnp.mean(x * x, axis=-1, keepdims=True)
        return ((x * lax.rsqrt(ms + EPS) * g_ref[...]) * gain + shift).astype(BF16)

    def norm_rope(xg, head_gain, cos, sin):
        sq = xg * xg
        s_lo = jnp.sum(jnp.where(low_head, sq, 0.0), axis=-1, keepdims=True)
        s_hi = jnp.sum(jnp.where(low_head, 0.0, sq), axis=-1, keepdims=True)
        inv = lax.rsqrt(jnp.where(low_head, s_lo, s_hi) * (1.0 / HEAD_DIM) + EPS)
        yg = xg * inv * head_gain
        rot = jnp.where(first_half,
                        pltpu.roll(yg, LANES - HEAD_DIM // 2, 1),
                        pltpu.roll(yg, HEAD_DIM // 2, 1))
        return yg * cos + rot * sin

    subs = [pl.ds(r * ROWBLOCK, ROWBLOCK) for r in range(tm // ROWBLOCK)]
    hs = [hidden(rows) for rows in subs]
    accs = [[jnp.dot(hb, w_ref[:, c0:c1], preferred_element_type=F32) for c0, c1 in IN_GROUPS]
            for hb in hs]
    for r, rows in enumerate(subs):
        a_gqa, a_vn, a_na = accs[r]
        cos, sin = cos_ref[rows, :], sin_ref[rows, :]
        for g in range(GQA_WIDTH // LANES):
            cols = slice(COL_QG + g * LANES, COL_QG + (g + 1) * LANES)
            qg_ref[rows, g * LANES:(g + 1) * LANES] = (
                norm_rope(a_gqa[:, cols], gq_ref[...], cos, sin) * scale).astype(BF16)
        kr = norm_rope(a_gqa[:, COL_KG:COL_KG + KV_WIDTH], gk_ref[...], cos, sin)
        ka_ref[rows, :] = kr.astype(BF16)
        kb_ref[rows, :] = pltpu.roll(kr, HEAD_DIM, 1).astype(BF16)
        for j in range(NA_WIDTH // LANES):
            vt_ref[r, j * LANES:(j + 1) * LANES, :] = a_vn[:, j * LANES:(j + 1) * LANES].T.astype(BF16)
        vt_ref[r, NA_WIDTH:V_WIDTH, :] = a_gqa[:, COL_VG:COL_VG + KV_WIDTH].T.astype(BF16)
        qn_ref[rows, :] = (a_na[:, 0:NA_WIDTH] * scale).astype(BF16)
        kn_ref[rows, :] = a_na[:, NA_WIDTH:2 * NA_WIDTH].astype(BF16)


def _in_proj(x, mod, g_attn, w_in, cos_t, sin_t, gq_t, gk_t, tm):
    B, T, _ = x.shape
    nrb = T // ROWBLOCK
    row = lambda b, i: (b, i, 0)
    const2 = lambda b, i: (0, 0)
    out_shape = (
        jax.ShapeDtypeStruct((B, T, NA_WIDTH), BF16),
        jax.ShapeDtypeStruct((B, T, NA_WIDTH), BF16),
        jax.ShapeDtypeStruct((B, T, GQA_WIDTH), BF16),
        jax.ShapeDtypeStruct((B, T, KV_WIDTH), BF16),
        jax.ShapeDtypeStruct((B, T, KV_WIDTH), BF16),
        jax.ShapeDtypeStruct((B, nrb, V_WIDTH, ROWBLOCK), BF16),
    )
    return pl.pallas_call(
        functools.partial(_in_kernel, tm=tm),
        out_shape=out_shape,
        grid=(B, T // tm),
        in_specs=[
            pl.BlockSpec((None, tm, D_MODEL), row),
            pl.BlockSpec((None, N_MOD, D_MODEL), lambda b, i: (b, 0, 0)),
            pl.BlockSpec((1, D_MODEL), const2),
            pl.BlockSpec((D_MODEL, IN_WIDTH), const2, pipeline_mode=pl.Buffered(1)),
            pl.BlockSpec((tm, LANES), lambda b, i: (i, 0)),
            pl.BlockSpec((tm, LANES), lambda b, i: (i, 0)),
            pl.BlockSpec((1, LANES), const2),
            pl.BlockSpec((1, LANES), const2),
        ],
        out_specs=(
            pl.BlockSpec((None, tm, NA_WIDTH), row),
            pl.BlockSpec((None, tm, NA_WIDTH), row),
            pl.BlockSpec((None, tm, GQA_WIDTH), row),
            pl.BlockSpec((None, tm, KV_WIDTH), row),
            pl.BlockSpec((None, tm, KV_WIDTH), row),
            pl.BlockSpec((None, tm // ROWBLOCK, V_WIDTH, ROWBLOCK), lambda b, i: (b, i, 0, 0)),
        ),
        compiler_params=pltpu.CompilerParams(
            dimension_semantics=("arbitrary", "arbitrary"), vmem_limit_bytes=VMEM_LIMIT),
        name="in_proj",
    )(x, mod, g_attn, w_in, cos_t, sin_t, gq_t, gk_t)


def _na_variant(variant, i, qr):
    if variant == 0:
        valid, di = i < NA_WIN_H, i - qr
    elif variant == 1:
        valid, di = 0 <= i - qr < NA_WIN_H, i - qr - ROWBLOCK_ROWS
    else:
        valid, di = ROWBLOCK_ROWS <= i < ROWBLOCK_ROWS + NA_WIN_H, i - qr - 2 * ROWBLOCK_ROWS
    return di + NA_WIN_H - 1 if valid else None


def _na_key_ranges(variant):
    n_rows = BAND_BLOCKS * ROWBLOCK_ROWS
    halves = []
    for e in range(ROWBLOCK_ROWS // 2):
        rows = [i for i in range(n_rows)
                if any(_na_variant(variant, i, 2 * e + d) is not None for d in range(2))]
        halves.append((rows[0] * GRID_W, (rows[-1] + 1) * GRID_W))
    return (min(r0 for r0, _ in halves), max(r1 for _, r1 in halves)), halves


def _na_kernel(rpb_ref, q_ref, k_ref, vt_ref, o_ref, cols_ref, tbl_ref, s0_ref, s1_ref, *, n_rb):
    n_row_off, n_col_off = 2 * NA_WIN_H - 1, 2 * NA_WIN_W - 1

    @pl.when(pl.program_id(1) == 0)
    def _build_bias():
        lane = lax.broadcasted_iota(jnp.int32, (GRID_W, LANES), 1)
        kc = lax.broadcasted_iota(jnp.int32, (GRID_W, LANES), 0)
        qc = lane & (GRID_W - 1)
        win = jnp.clip(qc - NA_WIN_W // 2, 0, GRID_W - NA_WIN_W)
        col_off = jnp.where((kc >= win) & (kc < win + NA_WIN_W), kc - qc + NA_WIN_W - 1, -1)
        hits = [col_off == o for o in range(n_col_off)]
        for hh in range(2):
            head = 2 * pl.program_id(0) + hh
            for r in range(n_row_off):
                t = jnp.full((GRID_W, LANES), MASKED, F32)
                for o in range(n_col_off):
                    t = jnp.where(hits[o], rpb_ref[head, r * n_col_off + o], t)
                cols_ref[hh, r] = t * LOG2E
        left = lane < GRID_W
        masked = jnp.full((GRID_W, LANES), MASKED, F32)
        for variant in range(3):
            for hh in range(2):
                for i in range(BAND_BLOCKS * ROWBLOCK_ROWS):
                    for qp in range(ROWBLOCK_ROWS // 2):
                        ro = [_na_variant(variant, i, 2 * qp + e) for e in range(2)]
                        t = [masked if r is None else cols_ref[hh, r] for r in ro]
                        tbl_ref[variant, hh, i * GRID_W:(i + 1) * GRID_W,
                                qp * LANES:(qp + 1) * LANES] = jnp.where(left, t[0], t[1])

    feat = lax.broadcasted_iota(jnp.int32, (LANES, ROWBLOCK), 0)
    s_refs = (s0_ref, s1_ref)

    def band_start(rb):
        return min(max(rb - 1, 0), n_rb - BAND_BLOCKS)

    def variant_of(rb):
        return 0 if rb == 0 else (2 if rb == n_rb - 1 else 1)

    def scores(rb, slot):
        variant = variant_of(rb)
        (u0, u1), _ = _na_key_ranges(variant)
        k0 = band_start(rb) * ROWBLOCK
        q = q_ref[rb * ROWBLOCK:(rb + 1) * ROWBLOCK, :]
        kband = k_ref[k0 + u0:k0 + u1, :]
        qt = q.astype(F32).T
        for hh in range(2):
            in_head = (feat < HEAD_DIM) if hh == 0 else (feat >= HEAD_DIM)
            qh = jnp.where(in_head, qt, 0.0).astype(BF16)
            s_refs[slot][hh, 0:u1 - u0] = (jnp.dot(kband, qh, preferred_element_type=F32)
                                           + tbl_ref[variant, hh, u0:u1])

    def softmax_pv(rb, slot):
        bs = band_start(rb)
        (u0, u1), halves = _na_key_ranges(variant_of(rb))
        outs = []
        for hh in range(2):
            cols = []
            for e, (r0, r1) in enumerate(halves):
                sl = (hh, slice(r0 - u0, r1 - u0), slice(e * LANES, (e + 1) * LANES))
                m = jnp.max(s_refs[slot][sl], axis=0, keepdims=True)
                p = jnp.exp2(s_refs[slot][sl] - m).astype(BF16)
                pieces = [jnp.zeros((r0, LANES), BF16), p, jnp.zeros((BAND - r1, LANES), BF16)]
                cols.append(jnp.concatenate([x for x in pieces if x.shape[0]], axis=0))
            pb = jnp.concatenate(cols, axis=1)
            o_aug = jnp.zeros((V_AUG, ROWBLOCK), F32)
            for j in range(u0 // ROWBLOCK, pl.cdiv(u1, ROWBLOCK)):
                vt = _with_ones(vt_ref[bs + j, hh * HEAD_DIM:(hh + 1) * HEAD_DIM, :])
                o_aug = o_aug + jnp.dot(vt, pb[j * ROWBLOCK:(j + 1) * ROWBLOCK, :],
                                        preferred_element_type=F32)
            outs.append(o_aug[0:HEAD_DIM] / o_aug[HEAD_DIM:HEAD_DIM + 1])
        o_pair = jnp.concatenate(outs, axis=0)
        o_ref[rb * ROWBLOCK:(rb + 1) * ROWBLOCK, :] = o_pair.T.astype(BF16)

    scores(0, 0)
    for rb in range(n_rb):
        if rb + 1 < n_rb:
            scores(rb + 1, (rb + 1) % 2)
        softmax_pv(rb, rb % 2)


def _na_attention(rpb, qn, kn, vt):
    B, T, _ = qn.shape
    n_rb = T // ROWBLOCK
    n_pairs = NA_HEADS // 2
    col = lambda p, b: (b, 0, p)
    return pl.pallas_call(
        functools.partial(_na_kernel, n_rb=n_rb),
        out_shape=jax.ShapeDtypeStruct((B, T, NA_WIDTH), BF16),
        grid=(n_pairs, B),
        in_specs=[
            pl.BlockSpec(memory_space=pltpu.SMEM),
            pl.BlockSpec((None, T, LANES), col),
            pl.BlockSpec((None, T, LANES), col),
            pl.BlockSpec((None, n_rb, LANES, ROWBLOCK), lambda p, b: (b, 0, p, 0)),
        ],
        out_specs=pl.BlockSpec((None, T, LANES), col),
        scratch_shapes=[pltpu.VMEM((2, 2 * NA_WIN_H - 1, GRID_W, LANES), F32),
                        pltpu.VMEM((3, 2, BAND, ROWBLOCK), F32),
                        pltpu.VMEM((2, BAND, ROWBLOCK), F32),
                        pltpu.VMEM((2, BAND, ROWBLOCK), F32)],
        compiler_params=pltpu.CompilerParams(
            dimension_semantics=("arbitrary", "arbitrary"), vmem_limit_bytes=VMEM_LIMIT),
        name="na_attn",
    )(rpb.reshape(NA_HEADS, -1), qn, kn, vt)


def _gqa_kernel(q_ref, ka_ref, kb_ref, vt_ref, o_ref, qm_ref, s0_ref, s1_ref, m_ref, acc_ref, *,
                tq, kc, n_chunks, unroll):
    feat = lax.broadcasted_iota(jnp.int32, (LANES, tq), 0)
    for g in range(GQA_HEADS // 2):
        q2t = q_ref[:, g * LANES:(g + 1) * LANES].astype(F32).T
        for half in range(2):
            in_head = (feat < HEAD_DIM) if half == 0 else (feat >= HEAD_DIM)
            qm_ref[2 * g + half] = jnp.where(in_head, q2t, 0.0).astype(BF16)
    m_ref[...] = jnp.full(m_ref.shape, MASKED, F32)
    acc_ref[...] = jnp.zeros(acc_ref.shape, F32)

    s_refs = (s0_ref, s1_ref)

    def scores(h, c, slot):
        kvh, half = h // GQA_GROUP, h % 2
        k_ref = ka_ref if half == kvh else kb_ref
        rows = pl.ds(pl.multiple_of(c * kc, kc), kc)
        s_refs[slot][h] = jnp.dot(k_ref[rows, :], qm_ref[h], preferred_element_type=F32)

    def softmax_pv(h, slot, vt_augs):
        m_old = m_ref[h]
        m_new = jnp.maximum(m_old, jnp.max(s_refs[slot][h], axis=0, keepdims=True))
        alpha = jnp.exp2(m_old - m_new)
        pb = jnp.exp2(s_refs[slot][h] - m_new).astype(BF16)
        pv = sum(jnp.dot(vt, pb[j * ROWBLOCK:(j + 1) * ROWBLOCK, :], preferred_element_type=F32)
                 for j, vt in enumerate(vt_augs))
        acc_ref[h] = alpha * acc_ref[h] + pv
        m_ref[h] = m_new

    def step(c, slot, prefetch):
        blocks = kc // ROWBLOCK
        vt_augs = [[_with_ones(vt_ref[c * blocks + j, kvh * HEAD_DIM:(kvh + 1) * HEAD_DIM, :])
                    for j in range(blocks)] for kvh in range(GQA_KV_HEADS)]
        if prefetch:
            scores(0, c + 1, 1 - slot)
        for h in range(GQA_HEADS):
            if prefetch and h + 1 < GQA_HEADS:
                scores(h + 1, c + 1, 1 - slot)
            softmax_pv(h, slot, vt_augs[h // GQA_GROUP])

    for h in range(GQA_HEADS):
        scores(h, 0, 0)

    def chunk_group(j, carry):
        for i in range(unroll):
            step(unroll * j + i, i % 2, True)
        return carry

    assert n_chunks % unroll == 0 and unroll % 2 == 0
    lax.fori_loop(0, n_chunks // unroll - 1, chunk_group, 0)
    for c in range(n_chunks - unroll, n_chunks):
        step(c, c % 2, c + 1 < n_chunks)

    for g in range(GQA_HEADS // 2):
        pair = []
        for e in range(2):
            a = acc_ref[2 * g + e]
            pair.append(a[0:HEAD_DIM] / a[HEAD_DIM:HEAD_DIM + 1])
        o_ref[:, g * LANES:(g + 1) * LANES] = jnp.concatenate(pair, axis=0).T.astype(BF16)


def _gqa_attention(qg, ka, kb, vt, tq, kc, unroll):
    B, T, _ = qg.shape
    n_chunks = T // kc
    n_rb = T // ROWBLOCK
    kv_block = NA_WIDTH // KV_WIDTH
    return pl.pallas_call(
        functools.partial(_gqa_kernel, tq=tq, kc=kc, n_chunks=n_chunks, unroll=unroll),
        out_shape=jax.ShapeDtypeStruct((B, T, GQA_WIDTH), BF16),
        grid=(B, T // tq),
        in_specs=[
            pl.BlockSpec((None, tq, GQA_WIDTH), lambda b, i: (b, i, 0)),
            pl.BlockSpec((None, T, KV_WIDTH), lambda b, i: (b, 0, 0)),
            pl.BlockSpec((None, T, KV_WIDTH), lambda b, i: (b, 0, 0)),
            pl.BlockSpec((None, n_rb, KV_WIDTH, ROWBLOCK), lambda b, i: (b, 0, kv_block, 0)),
        ],
        out_specs=pl.BlockSpec((None, tq, GQA_WIDTH), lambda b, i: (b, i, 0)),
        scratch_shapes=[
            pltpu.VMEM((GQA_HEADS, LANES, tq), BF16),
            pltpu.VMEM((GQA_HEADS, kc, tq), F32),
            pltpu.VMEM((GQA_HEADS, kc, tq), F32),
            pltpu.VMEM((GQA_HEADS, 1, tq), F32),
            pltpu.VMEM((GQA_HEADS, V_AUG, tq), F32),
        ],
        compiler_params=pltpu.CompilerParams(
            dimension_semantics=("arbitrary", "arbitrary"), vmem_limit_bytes=VMEM_LIMIT),
        name="gqa_attn",
    )(qg, ka, kb, vt)


def _post_kernel(x_ref, ona_ref, og_ref, mod_ref, wo_ref, gf_ref, wg_ref, wu_ref, wd_ref,
                 gfin_ref, o_ref, *, tm):
    gate_a, shift_f, gain_f, gate_f = (mod_ref[2:3, :], mod_ref[3:4, :], 1.0 + mod_ref[4:5, :],
                                       mod_ref[5:6, :])

    def rms(x):
        return x * lax.rsqrt(jnp.mean(x * x, axis=-1, keepdims=True) + EPS)

    assert tm % (2 * ROWBLOCK) == 0
    for first in range(0, tm, 2 * ROWBLOCK):
        subs = [pl.ds(first + r * ROWBLOCK, ROWBLOCK) for r in range(2)]
        attn = [jnp.dot(ona_ref[rows, :], wo_ref[0:NA_WIDTH, :], preferred_element_type=F32)
                + jnp.dot(og_ref[rows, :], wo_ref[NA_WIDTH:, :], preferred_element_type=F32)
                for rows in subs]
        x1 = [x_ref[rows, :] + gate_a * a for rows, a in zip(subs, attn)]
        hb = [((rms(v) * gf_ref[...]) * gain_f + shift_f).astype(BF16) for v in x1]
        gu = [(jnp.dot(h, wg_ref[...], preferred_element_type=F32),
               jnp.dot(h, wu_ref[...], preferred_element_type=F32)) for h in hb]
        ff = [(_silu(g) * u).astype(BF16) for g, u in gu]
        x2 = [v + gate_f * jnp.dot(f, wd_ref[...], preferred_element_type=F32)
              for v, f in zip(x1, ff)]
        for rows, v in zip(subs, x2):
            o_ref[rows, :] = rms(v) * gfin_ref[...]


def _post(x, o_na, o_g, mod, w_o, g_ffn, w_gate, w_up, w_down, g_final, tm):
    B, T, _ = x.shape
    d_ff = w_gate.shape[1]
    row = lambda b, i: (b, i, 0)
    const2 = lambda b, i: (0, 0)
    resident = pl.Buffered(1)
    return pl.pallas_call(
        functools.partial(_post_kernel, tm=tm),
        out_shape=jax.ShapeDtypeStruct((B, T, D_MODEL), F32),
        grid=(B, T // tm),
        in_specs=[
            pl.BlockSpec((None, tm, D_MODEL), row),
            pl.BlockSpec((None, tm, NA_WIDTH), row),
            pl.BlockSpec((None, tm, GQA_WIDTH), row),
            pl.BlockSpec((None, N_MOD, D_MODEL), lambda b, i: (b, 0, 0)),
            pl.BlockSpec((D_MODEL, D_MODEL), const2, pipeline_mode=resident),
            pl.BlockSpec((1, D_MODEL), const2),
            pl.BlockSpec((D_MODEL, d_ff), const2, pipeline_mode=resident),
            pl.BlockSpec((D_MODEL, d_ff), const2, pipeline_mode=resident),
            pl.BlockSpec((d_ff, D_MODEL), const2, pipeline_mode=resident),
            pl.BlockSpec((1, D_MODEL), const2),
        ],
        out_specs=pl.BlockSpec((None, tm, D_MODEL), row),
        compiler_params=pltpu.CompilerParams(
            dimension_semantics=("arbitrary", "arbitrary"), vmem_limit_bytes=VMEM_LIMIT),
        name="post",
    )(x, o_na, o_g, mod, w_o, g_ffn, w_gate, w_up, w_down, g_final)


def _rope_tables(n_tokens):
    t = np.arange(n_tokens)
    row = (t // GRID_W).astype(np.float64)
    col = (t % GRID_W).astype(np.float64)

    def angles(pos, dims):
        inv = ROPE_THETA ** (-np.arange(0, dims, 2, dtype=np.float64) / dims)
        return pos[:, None] * inv[None, :]

    ang = np.concatenate([angles(row, HEAD_DIM // 2), angles(col, HEAD_DIM // 2)], axis=-1)
    cos, sin = np.cos(ang), np.sin(ang)
    reps = LANES // HEAD_DIM
    return (jnp.asarray(np.tile(np.concatenate([cos, cos], axis=-1), (1, reps)), F32),
            jnp.asarray(np.tile(np.concatenate([-sin, sin], axis=-1), (1, reps)), F32))


def _permute_in_columns(w):
    q_na, k_na, v_na, q_g, k_g, v_g = jnp.split(
        w, (NA_WIDTH, 2 * NA_WIDTH, 3 * NA_WIDTH, 3 * NA_WIDTH + GQA_WIDTH,
            3 * NA_WIDTH + GQA_WIDTH + KV_WIDTH), axis=-1)
    return jnp.concatenate([q_g, k_g, v_g, v_na, q_na, k_na], axis=-1)


def kernel(x, c, w_ada, b_ada, g_attn, w_in, g_q, g_k, rpb, w_o, g_ffn, w_gate, w_up, w_down, g_final):
    B, T, _ = x.shape
    assert w_ada.shape[0] == 1, "single-layer block: the final norm is fused into the layer"
    cos_t, sin_t = _rope_tables(T)
    reps = LANES // HEAD_DIM
    mod = _ada(c, w_ada[0], b_ada[0]).reshape(B, N_MOD, D_MODEL)
    qn, kn, qg, ka, kb, vt = _in_proj(
        x, mod, g_attn[0].reshape(1, D_MODEL), _permute_in_columns(w_in[0]).astype(BF16),
        cos_t, sin_t,
        jnp.tile(g_q[0], reps).reshape(1, LANES), jnp.tile(g_k[0], reps).reshape(1, LANES),
        tm=1024)
    o_na = _na_attention(rpb[0], qn, kn, vt)
    o_g = _gqa_attention(qg, ka, kb, vt, tq=512, kc=256, unroll=4)
    return _post(x, o_na, o_g, mod, w_o[0].astype(BF16), g_ffn[0].reshape(1, D_MODEL),
                 w_gate[0].astype(BF16), w_up[0].astype(BF16), w_down[0].astype(BF16),
                 g_final.reshape(1, D_MODEL), tm=1024)
```

```python
import functools

import jax
import jax.numpy as jnp
import numpy as np
from jax import lax
from jax.experimental import pallas as pl
from jax.experimental.pallas import tpu as pltpu

F32 = jnp.float32
BF16 = jnp.bfloat16

D_MODEL = 1024
HEAD_DIM = 64
NA_HEADS = 8
GQA_HEADS = 8
GQA_KV_HEADS = 2
GQA_GROUP = GQA_HEADS // GQA_KV_HEADS
NA_WIDTH = NA_HEADS * HEAD_DIM
GQA_WIDTH = GQA_HEADS * HEAD_DIM
KV_WIDTH = GQA_KV_HEADS * HEAD_DIM
IN_WIDTH = 3 * NA_WIDTH + GQA_WIDTH + 2 * KV_WIDTH
V_WIDTH = NA_WIDTH + KV_WIDTH
GRID_W = 64
NA_WIN_H = 8
NA_WIN_W = 16
ROPE_THETA = 10000.0
N_MOD = 6
EPS = 1e-6

LANES = 128
BF16_SUBLANES = 16
ROWBLOCK_ROWS = 4
ROWBLOCK = ROWBLOCK_ROWS * GRID_W
BAND_BLOCKS = 3
BAND = BAND_BLOCKS * ROWBLOCK
V_AUG = HEAD_DIM + BF16_SUBLANES
MASKED = -1e30
LOG2E = 1.4426950408889634
GQA_SCORE_LIMIT = 60.0
VMEM_LIMIT = 56 * 1024 * 1024

COL_QG, COL_KG, COL_VG, COL_VN, COL_QN, COL_KN = 0, 512, 640, 768, 1280, 1792
IN_GROUPS = ((0, COL_VN), (COL_VN, COL_QN), (COL_QN, IN_WIDTH))


def _silu(x):
    return x / (1.0 + jnp.exp(-x))


def _with_ones(vt):
    return jnp.concatenate([vt, jnp.ones((BF16_SUBLANES, vt.shape[1]), vt.dtype)], axis=0)


def _ada_kernel(c_ref, w_ref, b_ref, o_ref):
    ca = _silu(c_ref[...])
    o_ref[...] = jnp.dot(ca.astype(BF16), w_ref[...].astype(BF16),
                         preferred_element_type=F32) + b_ref[...]


def _ada(c, w_ada, b_ada):
    B = c.shape[0]
    n = w_ada.shape[1]
    return pl.pallas_call(
        _ada_kernel,
        out_shape=jax.ShapeDtypeStruct((B, n), F32),
        grid=(n // D_MODEL,),
        in_specs=[pl.BlockSpec((B, D_MODEL), lambda j: (0, 0)),
                  pl.BlockSpec((D_MODEL, D_MODEL), lambda j: (0, j)),
                  pl.BlockSpec((1, D_MODEL), lambda j: (0, j))],
        out_specs=pl.BlockSpec((B, D_MODEL), lambda j: (0, j)),
        compiler_params=pltpu.CompilerParams(dimension_semantics=("arbitrary",)),
        name="ada",
    )(c, w_ada, b_ada.reshape(1, n))


def _in_kernel(x_ref, mod_ref, g_ref, w_ref, cos_ref, sin_ref, gq_ref, gk_ref,
               qn_ref, kn_ref, qg_ref, ka_ref, kb_ref, vt_ref, *, tm):
    scale = HEAD_DIM ** -0.5 * LOG2E
    lane = lax.broadcasted_iota(jnp.int32, (ROWBLOCK, LANES), 1)
    low_head = lane < HEAD_DIM
    first_half = (lane & (HEAD_DIM - 1)) < HEAD_DIM // 2
    shift, gain = mod_ref[0:1, :], 1.0 + mod_ref[1:2, :]

    def hidden(rows):
        x = x_ref[rows, :]
        ms = jnp.mean(x * x, axis=-1, keepdims=True)
        return ((x * lax.rsqrt(ms + EPS) * g_ref[...]) * gain + shift).astype(BF16)

    def norm_rope(xg, head_gain, cos, sin):
        sq = xg * xg
        s_lo = jnp.sum(jnp.where(low_head, sq, 0.0), axis=-1, keepdims=True)
        s_hi = jnp.sum(jnp.where(low_head, 0.0, sq), axis=-1, keepdims=True)
        inv = lax.rsqrt(jnp.where(low_head, s_lo, s_hi) * (1.0 / HEAD_DIM) + EPS)
        yg = xg * inv * head_gain
        rot = jnp.where(first_half,
                        pltpu.roll(yg, LANES - HEAD_DIM // 2, 1),
                        pltpu.roll(yg, HEAD_DIM // 2, 1))
        return yg * cos + rot * sin

    subs = [pl.ds(r * ROWBLOCK, ROWBLOCK) for r in range(tm // ROWBLOCK)]
    hs = [hidden(rows) for rows in subs]
    accs = [[jnp.dot(hb, w_ref[:, c0:c1], preferred_element_type=F32) for c0, c1 in IN_GROUPS]
            for hb in hs]
    for r, rows in enumerate(subs):
        a_gqa, a_vn, a_na = accs[r]
        cos, sin = cos_ref[rows, :], sin_ref[rows, :]
        for g in range(GQA_WIDTH // LANES):
            cols = slice(COL_QG + g * LANES, COL_QG + (g + 1) * LANES)
            qg_ref[rows, g * LANES:(g + 1) * LANES] = (
                norm_rope(a_gqa[:, cols], gq_ref[...], cos, sin) * scale).astype(BF16)
        kr = norm_rope(a_gqa[:, COL_KG:COL_KG + KV_WIDTH], gk_ref[...], cos, sin)
        ka_ref[rows, :] = kr.astype(BF16)
        kb_ref[rows, :] = pltpu.roll(kr, HEAD_DIM, 1).astype(BF16)
        for j in range(NA_WIDTH // LANES):
            vt_ref[r, j * LANES:(j + 1) * LANES, :] = a_vn[:, j * LANES:(j + 1) * LANES].T.astype(BF16)
        vt_ref[r, NA_WIDTH:V_WIDTH, :] = a_gqa[:, COL_VG:COL_VG + KV_WIDTH].T.astype(BF16)
        qn_ref[rows, :] = (a_na[:, 0:NA_WIDTH] * scale).astype(BF16)
        kn_ref[rows, :] = a_na[:, NA_WIDTH:2 * NA_WIDTH].astype(BF16)


def _in_proj(x, mod, g_attn, w_in, cos_t, sin_t, gq_t, gk_t, tm):
    B, T, _ = x.shape
    nrb = T // ROWBLOCK
    row = lambda b, i: (b, i, 0)
    const2 = lambda b, i: (0, 0)
    out_shape = (
        jax.ShapeDtypeStruct((B, T, NA_WIDTH), BF16),
        jax.ShapeDtypeStruct((B, T, NA_WIDTH), BF16),
        jax.ShapeDtypeStruct((B, T, GQA_WIDTH), BF16),
        jax.ShapeDtypeStruct((B, T, KV_WIDTH), BF16),
        jax.ShapeDtypeStruct((B, T, KV_WIDTH), BF16),
        jax.ShapeDtypeStruct((B, nrb, V_WIDTH, ROWBLOCK), BF16),
    )
    return pl.pallas_call(
        functools.partial(_in_kernel, tm=tm),
        out_shape=out_shape,
        grid=(B, T // tm),
        in_specs=[
            pl.BlockSpec((None, tm, D_MODEL), row),
            pl.BlockSpec((None, N_MOD, D_MODEL), lambda b, i: (b, 0, 0)),
            pl.BlockSpec((1, D_MODEL), const2),
            pl.BlockSpec((D_MODEL, IN_WIDTH), const2, pipeline_mode=pl.Buffered(1)),
            pl.BlockSpec((tm, LANES), lambda b, i: (i, 0)),
            pl.BlockSpec((tm, LANES), lambda b, i: (i, 0)),
            pl.BlockSpec((1, LANES), const2),
            pl.BlockSpec((1, LANES), const2),
        ],
        out_specs=(
            pl.BlockSpec((None, tm, NA_WIDTH), row),
            pl.BlockSpec((None, tm, NA_WIDTH), row),
            pl.BlockSpec((None, tm, GQA_WIDTH), row),
            pl.BlockSpec((None, tm, KV_WIDTH), row),
            pl.BlockSpec((None, tm, KV_WIDTH), row),
            pl.BlockSpec((None, tm // ROWBLOCK, V_WIDTH, ROWBLOCK), lambda b, i: (b, i, 0, 0)),
        ),
        compiler_params=pltpu.CompilerParams(
            dimension_semantics=("arbitrary", "arbitrary"), vmem_limit_bytes=VMEM_LIMIT),
        name="in_proj",
    )(x, mod, g_attn, w_in, cos_t, sin_t, gq_t, gk_t)


def _na_variant(variant, i, qr):
    if variant == 0:
        valid, di = i < NA_WIN_H, i - qr
    elif variant == 1:
        valid, di = 0 <= i - qr < NA_WIN_H, i - qr - ROWBLOCK_ROWS
    else:
        valid, di = ROWBLOCK_ROWS <= i < ROWBLOCK_ROWS + NA_WIN_H, i - qr - 2 * ROWBLOCK_ROWS
    return di + NA_WIN_H - 1 if valid else None


def _na_key_ranges(variant):
    n_rows = BAND_BLOCKS * ROWBLOCK_ROWS
    halves = []
    for e in range(ROWBLOCK_ROWS // 2):
        rows = [i for i in range(n_rows)
                if any(_na_variant(variant, i, 2 * e + d) is not None for d in range(2))]
        halves.append((rows[0] * GRID_W, (rows[-1] + 1) * GRID_W))
    return (min(r0 for r0, _ in halves), max(r1 for _, r1 in halves)), halves


def _na_kernel(rpb_ref, q_ref, k_ref, vt_ref, o_ref, cols_ref, tbl_ref, s0_ref, s1_ref, *, n_rb):
    n_row_off, n_col_off = 2 * NA_WIN_H - 1, 2 * NA_WIN_W - 1

    @pl.when(pl.program_id(1) == 0)
    def _build_bias():
        lane = lax.broadcasted_iota(jnp.int32, (GRID_W, LANES), 1)
        kc = lax.broadcasted_iota(jnp.int32, (GRID_W, LANES), 0)
        qc = lane & (GRID_W - 1)
        win = jnp.clip(qc - NA_WIN_W // 2, 0, GRID_W - NA_WIN_W)
        col_off = jnp.where((kc >= win) & (kc < win + NA_WIN_W), kc - qc + NA_WIN_W - 1, -1)
        hits = [col_off == o for o in range(n_col_off)]
        for hh in range(2):
            head = 2 * pl.program_id(0) + hh
            for r in range(n_row_off):
                t = jnp.full((GRID_W, LANES), MASKED, F32)
                for o in range(n_col_off):
                    t = jnp.where(hits[o], rpb_ref[head, r * n_col_off + o], t)
                cols_ref[hh, r] = t * LOG2E
        left = lane < GRID_W
        masked = jnp.full((GRID_W, LANES), MASKED, F32)
        for variant in range(3):
            for hh in range(2):
                for i in range(BAND_BLOCKS * ROWBLOCK_ROWS):
                    for qp in range(ROWBLOCK_ROWS // 2):
                        ro = [_na_variant(variant, i, 2 * qp + e) for e in range(2)]
                        t = [masked if r is None else cols_ref[hh, r] for r in ro]
                        tbl_ref[variant, hh, i * GRID_W:(i + 1) * GRID_W,
                                qp * LANES:(qp + 1) * LANES] = jnp.where(left, t[0], t[1])

    feat = lax.broadcasted_iota(jnp.int32, (LANES, ROWBLOCK), 0)
    s_refs = (s0_ref, s1_ref)

    def band_start(rb):
        return min(max(rb - 1, 0), n_rb - BAND_BLOCKS)

    def variant_of(rb):
        return 0 if rb == 0 else (2 if rb == n_rb - 1 else 1)

    def scores(rb, slot):
        variant = variant_of(rb)
        (u0, u1), _ = _na_key_ranges(variant)
        k0 = band_start(rb) * ROWBLOCK
        q = q_ref[rb * ROWBLOCK:(rb + 1) * ROWBLOCK, :]
        kband = k_ref[k0 + u0:k0 + u1, :]
        qt = q.astype(F32).T
        for hh in range(2):
            in_head = (feat < HEAD_DIM) if hh == 0 else (feat >= HEAD_DIM)
            qh = jnp.where(in_head, qt, 0.0).astype(BF16)
            s_refs[slot][hh, 0:u1 - u0] = (jnp.dot(kband, qh, preferred_element_type=F32)
                                           + tbl_ref[variant, hh, u0:u1])

    def softmax_pv(rb, slot):
        bs = band_start(rb)
        (u0, u1), halves = _na_key_ranges(variant_of(rb))
        outs = []
        for hh in range(2):
            cols = []
            for e, (r0, r1) in enumerate(halves):
                sl = (hh, slice(r0 - u0, r1 - u0), slice(e * LANES, (e + 1) * LANES))
                m = jnp.max(s_refs[slot][sl], axis=0, keepdims=True)
                p = jnp.exp2(s_refs[slot][sl] - m).astype(BF16)
                pieces = [jnp.zeros((r0, LANES), BF16), p, jnp.zeros((BAND - r1, LANES), BF16)]
                cols.append(jnp.concatenate([x for x in pieces if x.shape[0]], axis=0))
            pb = jnp.concatenate(cols, axis=1)
            o_aug = jnp.zeros((V_AUG, ROWBLOCK), F32)
            for j in range(u0 // ROWBLOCK, pl.cdiv(u1, ROWBLOCK)):
                vt = _with_ones(vt_ref[bs + j, hh * HEAD_DIM:(hh + 1) * HEAD_DIM, :])
                o_aug = o_aug + jnp.dot(vt, pb[j * ROWBLOCK:(j + 1) * ROWBLOCK, :],
                                        preferred_element_type=F32)
            outs.append(o_aug[0:HEAD_DIM] / o_aug[HEAD_DIM:HEAD_DIM + 1])
        o_pair = jnp.concatenate(outs, axis=0)
        o_ref[rb * ROWBLOCK:(rb + 1) * ROWBLOCK, :] = o_pair.T.astype(BF16)

    scores(0, 0)
    for rb in range(n_rb):
        if rb + 1 < n_rb:
            scores(rb + 1, (rb + 1) % 2)
        softmax_pv(rb, rb % 2)


def _na_attention(rpb, qn, kn, vt):
    B, T, _ = qn.shape
    n_rb = T // ROWBLOCK
    n_pairs = NA_HEADS // 2
    col = lambda p, b: (b, 0, p)
    return pl.pallas_call(
        functools.partial(_na_kernel, n_rb=n_rb),
        out_shape=jax.ShapeDtypeStruct((B, T, NA_WIDTH), BF16),
        grid=(n_pairs, B),
        in_specs=[
            pl.BlockSpec(memory_space=pltpu.SMEM),
            pl.BlockSpec((None, T, LANES), col),
            pl.BlockSpec((None, T, LANES), col),
            pl.BlockSpec((None, n_rb, LANES, ROWBLOCK), lambda p, b: (b, 0, p, 0)),
        ],
        out_specs=pl.BlockSpec((None, T, LANES), col),
        scratch_shapes=[pltpu.VMEM((2, 2 * NA_WIN_H - 1, GRID_W, LANES), F32),
                        pltpu.VMEM((3, 2, BAND, ROWBLOCK), F32),
                        pltpu.VMEM((2, BAND, ROWBLOCK), F32),
                        pltpu.VMEM((2, BAND, ROWBLOCK), F32)],
        compiler_params=pltpu.CompilerParams(
            dimension_semantics=("arbitrary", "arbitrary"), vmem_limit_bytes=VMEM_LIMIT),
        name="na_attn",
    )(rpb.reshape(NA_HEADS, -1), qn, kn, vt)


def _gqa_stage_queries(q_ref, qm_ref, tq):
    feat = lax.broadcasted_iota(jnp.int32, (LANES, tq), 0)
    for g in range(GQA_HEADS // 2):
        q2t = q_ref[:, g * LANES:(g + 1) * LANES].astype(F32).T
        for half in range(2):
            in_head = (feat < HEAD_DIM) if half == 0 else (feat >= HEAD_DIM)
            qm_ref[2 * g + half] = jnp.where(in_head, q2t, 0.0).astype(BF16)


def _gqa_write_output(acc_ref, o_ref):
    for g in range(GQA_HEADS // 2):
        pair = []
        for e in range(2):
            a = acc_ref[2 * g + e]
            pair.append(a[0:HEAD_DIM] / a[HEAD_DIM:HEAD_DIM + 1])
        o_ref[:, g * LANES:(g + 1) * LANES] = jnp.concatenate(pair, axis=0).T.astype(BF16)


def _gqa_bounded_kernel(q_ref, ka_ref, kb_ref, vt_ref, o_ref, qm_ref, acc_ref, *, tq, n_chunks,
                        unroll, lookahead):
    _gqa_stage_queries(q_ref, qm_ref, tq)
    acc_ref[...] = jnp.zeros(acc_ref.shape, F32)

    def scores(c, h):
        kvh, half = h // GQA_GROUP, h % 2
        k_ref = ka_ref if half == kvh else kb_ref
        rows = pl.ds(pl.multiple_of(c * ROWBLOCK, ROWBLOCK), ROWBLOCK)
        return jnp.dot(k_ref[rows, :], qm_ref[h], preferred_element_type=F32)

    def chunk_group(j, carry):
        accs = [acc_ref[h] for h in range(GQA_HEADS)]
        items = [(unroll * j + i, h) for i in range(unroll) for h in range(GQA_HEADS)]
        pending = [scores(c, h) for c, h in items[:lookahead]]
        vt_aug = {}
        for n, (c, h) in enumerate(items):
            if n + lookahead < len(items):
                pending.append(scores(*items[n + lookahead]))
            kvh = h // GQA_GROUP
            if (n // GQA_HEADS, kvh) not in vt_aug:
                vt_aug[(n // GQA_HEADS, kvh)] = _with_ones(
                    vt_ref[c, kvh * HEAD_DIM:(kvh + 1) * HEAD_DIM, :])
            pb = jnp.exp2(pending.pop(0)).astype(BF16)
            accs[h] = accs[h] + jnp.dot(vt_aug[(n // GQA_HEADS, kvh)], pb,
                                        preferred_element_type=F32)
        for h in range(GQA_HEADS):
            acc_ref[h] = accs[h]
        return carry

    assert n_chunks % unroll == 0
    lax.fori_loop(0, n_chunks // unroll, chunk_group, 0)
    _gqa_write_output(acc_ref, o_ref)


def _gqa_kernel(q_ref, ka_ref, kb_ref, vt_ref, o_ref, qm_ref, s0_ref, s1_ref, m_ref, acc_ref, *,
                tq, kc, n_chunks, unroll):
    _gqa_stage_queries(q_ref, qm_ref, tq)
    m_ref[...] = jnp.full(m_ref.shape, MASKED, F32)
    acc_ref[...] = jnp.zeros(acc_ref.shape, F32)

    s_refs = (s0_ref, s1_ref)

    def scores(h, c, slot):
        kvh, half = h // GQA_GROUP, h % 2
        k_ref = ka_ref if half == kvh else kb_ref
        rows = pl.ds(pl.multiple_of(c * kc, kc), kc)
        s_refs[slot][h] = jnp.dot(k_ref[rows, :], qm_ref[h], preferred_element_type=F32)

    def softmax_pv(h, slot, vt_augs):
        m_old = m_ref[h]
        m_new = jnp.maximum(m_old, jnp.max(s_refs[slot][h], axis=0, keepdims=True))
        alpha = jnp.exp2(m_old - m_new)
        pb = jnp.exp2(s_refs[slot][h] - m_new).astype(BF16)
        pv = sum(jnp.dot(vt, pb[j * ROWBLOCK:(j + 1) * ROWBLOCK, :], preferred_element_type=F32)
                 for j, vt in enumerate(vt_augs))
        acc_ref[h] = alpha * acc_ref[h] + pv
        m_ref[h] = m_new

    def step(c, slot, prefetch):
        blocks = kc // ROWBLOCK
        vt_augs = [[_with_ones(vt_ref[c * blocks + j, kvh * HEAD_DIM:(kvh + 1) * HEAD_DIM, :])
                    for j in range(blocks)] for kvh in range(GQA_KV_HEADS)]
        if prefetch:
            scores(0, c + 1, 1 - slot)
        for h in range(GQA_HEADS):
            if prefetch and h + 1 < GQA_HEADS:
                scores(h + 1, c + 1, 1 - slot)
            softmax_pv(h, slot, vt_augs[h // GQA_GROUP])

    for h in range(GQA_HEADS):
        scores(h, 0, 0)

    def chunk_group(j, carry):
        for i in range(unroll):
            step(unroll * j + i, i % 2, True)
        return carry

    assert n_chunks % unroll == 0 and unroll % 2 == 0 and n_chunks >= 2 * unroll
    chunk_group(0, 0)
    lax.fori_loop(1, n_chunks // unroll - 1, chunk_group, 0)
    for c in range(n_chunks - unroll, n_chunks):
        step(c, c % 2, c + 1 < n_chunks)

    _gqa_write_output(acc_ref, o_ref)


def _gqa_call(kernel_fn, scratch_shapes, name, qg, ka, kb, vt, tq):
    B, T, _ = qg.shape
    n_rb = T // ROWBLOCK
    kv_block = NA_WIDTH // KV_WIDTH
    return pl.pallas_call(
        kernel_fn,
        out_shape=jax.ShapeDtypeStruct((B, T, GQA_WIDTH), BF16),
        grid=(B, T // tq),
        in_specs=[
            pl.BlockSpec((None, tq, GQA_WIDTH), lambda b, i: (b, i, 0)),
            pl.BlockSpec((None, T, KV_WIDTH), lambda b, i: (b, 0, 0)),
            pl.BlockSpec((None, T, KV_WIDTH), lambda b, i: (b, 0, 0)),
            pl.BlockSpec((None, n_rb, KV_WIDTH, ROWBLOCK), lambda b, i: (b, 0, kv_block, 0)),
        ],
        out_specs=pl.BlockSpec((None, tq, GQA_WIDTH), lambda b, i: (b, i, 0)),
        scratch_shapes=scratch_shapes,
        compiler_params=pltpu.CompilerParams(
            dimension_semantics=("arbitrary", "arbitrary"), vmem_limit_bytes=VMEM_LIMIT),
        name=name,
    )(qg, ka, kb, vt)


def _gqa_attention(qg, ka, kb, vt, tq, kc, unroll):
    T = qg.shape[1]
    return _gqa_call(
        functools.partial(_gqa_kernel, tq=tq, kc=kc, n_chunks=T // kc, unroll=unroll),
        [pltpu.VMEM((GQA_HEADS, LANES, tq), BF16),
         pltpu.VMEM((GQA_HEADS, kc, tq), F32),
         pltpu.VMEM((GQA_HEADS, kc, tq), F32),
         pltpu.VMEM((GQA_HEADS, 1, tq), F32),
         pltpu.VMEM((GQA_HEADS, V_AUG, tq), F32)],
        "gqa_attn", qg, ka, kb, vt, tq)


def _gqa_attention_bounded(qg, ka, kb, vt, tq, unroll):
    T = qg.shape[1]
    return _gqa_call(
        functools.partial(_gqa_bounded_kernel, tq=tq, n_chunks=T // ROWBLOCK, unroll=unroll,
                          lookahead=3),
        [pltpu.VMEM((GQA_HEADS, LANES, tq), BF16),
         pltpu.VMEM((GQA_HEADS, V_AUG, tq), F32)],
        "gqa_attn_bounded", qg, ka, kb, vt, tq)


def _post_kernel(x_ref, ona_ref, og_ref, mod_ref, wo_ref, gf_ref, wg_ref, wu_ref, wd_ref,
                 gfin_ref, o_ref, *, tm):
    gate_a, shift_f, gain_f, gate_f = (mod_ref[2:3, :], mod_ref[3:4, :], 1.0 + mod_ref[4:5, :],
                                       mod_ref[5:6, :])

    def rms(x):
        return x * lax.rsqrt(jnp.mean(x * x, axis=-1, keepdims=True) + EPS)

    assert tm % (2 * ROWBLOCK) == 0
    for first in range(0, tm, 2 * ROWBLOCK):
        subs = [pl.ds(first + r * ROWBLOCK, ROWBLOCK) for r in range(2)]
        attn = [jnp.dot(ona_ref[rows, :], wo_ref[0:NA_WIDTH, :], preferred_element_type=F32)
                + jnp.dot(og_ref[rows, :], wo_ref[NA_WIDTH:, :], preferred_element_type=F32)
                for rows in subs]
        x1 = [x_ref[rows, :] + gate_a * a for rows, a in zip(subs, attn)]
        hb = [((rms(v) * gf_ref[...]) * gain_f + shift_f).astype(BF16) for v in x1]
        gu = [(jnp.dot(h, wg_ref[...], preferred_element_type=F32),
               jnp.dot(h, wu_ref[...], preferred_element_type=F32)) for h in hb]
        ff = [(_silu(g) * u).astype(BF16) for g, u in gu]
        x2 = [v + gate_f * jnp.dot(f, wd_ref[...], preferred_element_type=F32)
              for v, f in zip(x1, ff)]
        for rows, v in zip(subs, x2):
            o_ref[rows, :] = rms(v) * gfin_ref[...]


def _post(x, o_na, o_g, mod, w_o, g_ffn, w_gate, w_up, w_down, g_final, tm):
    B, T, _ = x.shape
    d_ff = w_gate.shape[1]
    row = lambda b, i: (b, i, 0)
    const2 = lambda b, i: (0, 0)
    resident = pl.Buffered(1)
    return pl.pallas_call(
        functools.partial(_post_kernel, tm=tm),
        out_shape=jax.ShapeDtypeStruct((B, T, D_MODEL), F32),
        grid=(B, T // tm),
        in_specs=[
            pl.BlockSpec((None, tm, D_MODEL), row),
            pl.BlockSpec((None, tm, NA_WIDTH), row),
            pl.BlockSpec((None, tm, GQA_WIDTH), row),
            pl.BlockSpec((None, N_MOD, D_MODEL), lambda b, i: (b, 0, 0)),
            pl.BlockSpec((D_MODEL, D_MODEL), const2, pipeline_mode=resident),
            pl.BlockSpec((1, D_MODEL), const2),
            pl.BlockSpec((D_MODEL, d_ff), const2, pipeline_mode=resident),
            pl.BlockSpec((D_MODEL, d_ff), const2, pipeline_mode=resident),
            pl.BlockSpec((d_ff, D_MODEL), const2, pipeline_mode=resident),
            pl.BlockSpec((1, D_MODEL), const2),
        ],
        out_specs=pl.BlockSpec((None, tm, D_MODEL), row),
        compiler_params=pltpu.CompilerParams(
            dimension_semantics=("arbitrary", "arbitrary"), vmem_limit_bytes=VMEM_LIMIT),
        name="post",
    )(x, o_na, o_g, mod, w_o, g_ffn, w_gate, w_up, w_down, g_final)


def _rope_tables(n_tokens):
    t = np.arange(n_tokens)
    row = (t // GRID_W).astype(np.float64)
    col = (t % GRID_W).astype(np.float64)

    def angles(pos, dims):
        inv = ROPE_THETA ** (-np.arange(0, dims, 2, dtype=np.float64) / dims)
        return pos[:, None] * inv[None, :]

    ang = np.concatenate([angles(row, HEAD_DIM // 2), angles(col, HEAD_DIM // 2)], axis=-1)
    cos, sin = np.cos(ang), np.sin(ang)
    reps = LANES // HEAD_DIM
    return (jnp.asarray(np.tile(np.concatenate([cos, cos], axis=-1), (1, reps)), F32),
            jnp.asarray(np.tile(np.concatenate([-sin, sin], axis=-1), (1, reps)), F32))


def _permute_in_columns(w):
    q_na, k_na, v_na, q_g, k_g, v_g = jnp.split(
        w, (NA_WIDTH, 2 * NA_WIDTH, 3 * NA_WIDTH, 3 * NA_WIDTH + GQA_WIDTH,
            3 * NA_WIDTH + GQA_WIDTH + KV_WIDTH), axis=-1)
    return jnp.concatenate([q_g, k_g, v_g, v_na, q_na, k_na], axis=-1)


def kernel(x, c, w_ada, b_ada, g_attn, w_in, g_q, g_k, rpb, w_o, g_ffn, w_gate, w_up, w_down, g_final):
    B, T, _ = x.shape
    assert w_ada.shape[0] == 1, "single-layer block: the final norm is fused into the layer"
    cos_t, sin_t = _rope_tables(T)
    reps = LANES // HEAD_DIM
    mod = _ada(c, w_ada[0], b_ada[0]).reshape(B, N_MOD, D_MODEL)
    qn, kn, qg, ka, kb, vt = _in_proj(
        x, mod, g_attn[0].reshape(1, D_MODEL), _permute_in_columns(w_in[0]).astype(BF16),
        cos_t, sin_t,
        jnp.tile(g_q[0], reps).reshape(1, LANES), jnp.tile(g_k[0], reps).reshape(1, LANES),
        tm=1024)
    o_na = _na_attention(rpb[0], qn, kn, vt)
    score_bound = (jnp.max(jnp.abs(g_q[0])) * jnp.max(jnp.abs(g_k[0]))
                   * (HEAD_DIM * HEAD_DIM ** -0.5 * LOG2E * 1.02))
    o_g = lax.cond(
        score_bound <= GQA_SCORE_LIMIT,
        lambda *a: _gqa_attention_bounded(*a, tq=512, unroll=4),
        lambda *a: _gqa_attention(*a, tq=512, kc=256, unroll=4),
        qg, ka, kb, vt)
    return _post(x, o_na, o_g, mod, w_o[0].astype(BF16), g_ffn[0].reshape(1, D_MODEL),
                 w_gate[0].astype(BF16), w_up[0].astype(BF16), w_down[0].astype(BF16),
                 g_final.reshape(1, D_MODEL), tm=1024)
```

```python
import functools

import jax
import jax.numpy as jnp
import numpy as np
from jax import lax
from jax.experimental import pallas as pl
from jax.experimental.pallas import tpu as pltpu

F32 = jnp.float32
BF16 = jnp.bfloat16

D_MODEL = 1024
HEAD_DIM = 64
NA_HEADS = 8
GQA_HEADS = 8
GQA_KV_HEADS = 2
GQA_GROUP = GQA_HEADS // GQA_KV_HEADS
NA_WIDTH = NA_HEADS * HEAD_DIM
GQA_WIDTH = GQA_HEADS * HEAD_DIM
KV_WIDTH = GQA_KV_HEADS * HEAD_DIM
IN_WIDTH = 3 * NA_WIDTH + GQA_WIDTH + 2 * KV_WIDTH
V_WIDTH = NA_WIDTH + KV_WIDTH
GRID_W = 64
NA_WIN_H = 8
NA_WIN_W = 16
ROPE_THETA = 10000.0
N_MOD = 6
EPS = 1e-6

LANES = 128
BF16_SUBLANES = 16
ROWBLOCK_ROWS = 4
ROWBLOCK = ROWBLOCK_ROWS * GRID_W
BAND_BLOCKS = 3
BAND = BAND_BLOCKS * ROWBLOCK
V_AUG = HEAD_DIM + BF16_SUBLANES
MASKED = -1e30
LOG2E = 1.4426950408889634
GQA_SCORE_LIMIT = 60.0
VMEM_LIMIT = 56 * 1024 * 1024

IN_ROWS = 1024
POST_ROWS = 1024
GQA_QUERIES = 512
GQA_KEYS = ROWBLOCK
GQA_UNROLL_ONLINE = 4
GQA_UNROLL_BOUNDED = 8
GQA_LOOKAHEAD = 4

COL_QG, COL_KG, COL_VG, COL_VN, COL_QN, COL_KN = 0, 512, 640, 768, 1280, 1792
IN_GROUPS = ((0, COL_VN), (COL_VN, COL_QN), (COL_QN, IN_WIDTH))


def _silu(x):
    return x / (1.0 + jnp.exp(-x))


def _with_ones(vt):
    return jnp.concatenate([vt, jnp.ones((BF16_SUBLANES, vt.shape[1]), vt.dtype)], axis=0)


def _ada_kernel(c_ref, w_ref, b_ref, o_ref):
    ca = _silu(c_ref[...])
    o_ref[...] = jnp.dot(ca.astype(BF16), w_ref[...].astype(BF16),
                         preferred_element_type=F32) + b_ref[...]


def _ada(c, w_ada, b_ada):
    B = c.shape[0]
    n = w_ada.shape[1]
    return pl.pallas_call(
        _ada_kernel,
        out_shape=jax.ShapeDtypeStruct((B, n), F32),
        grid=(n // D_MODEL,),
        in_specs=[pl.BlockSpec((B, D_MODEL), lambda j: (0, 0)),
                  pl.BlockSpec((D_MODEL, D_MODEL), lambda j: (0, j)),
                  pl.BlockSpec((1, D_MODEL), lambda j: (0, j))],
        out_specs=pl.BlockSpec((B, D_MODEL), lambda j: (0, j)),
        compiler_params=pltpu.CompilerParams(dimension_semantics=("arbitrary",)),
        name="ada",
    )(c, w_ada, b_ada.reshape(1, n))


def _in_kernel(x_ref, mod_ref, g_ref, w_ref, cos_ref, sin_ref, gq_ref, gk_ref,
               qn_ref, kn_ref, qg_ref, ka_ref, kb_ref, vt_ref, *, tm):
    scale = HEAD_DIM ** -0.5 * LOG2E
    lane = lax.broadcasted_iota(jnp.int32, (ROWBLOCK, LANES), 1)
    low_head = lane < HEAD_DIM
    first_half = (lane & (HEAD_DIM - 1)) < HEAD_DIM // 2
    shift, gain = mod_ref[0:1, :], 1.0 + mod_ref[1:2, :]

    def hidden(rows):
        x = x_ref[rows, :]
        ms = jnp.mean(x * x, axis=-1, keepdims=True)
        return ((x * lax.rsqrt(ms + EPS) * g_ref[...]) * gain + shift).astype(BF16)

    def norm_rope(xg, head_gain, cos, sin):
        sq = xg * xg
        s_lo = jnp.sum(jnp.where(low_head, sq, 0.0), axis=-1, keepdims=True)
        s_hi = jnp.sum(jnp.where(low_head, 0.0, sq), axis=-1, keepdims=True)
        inv = lax.rsqrt(jnp.where(low_head, s_lo, s_hi) * (1.0 / HEAD_DIM) + EPS)
        yg = xg * inv * head_gain
        rot = jnp.where(first_half,
                        pltpu.roll(yg, LANES - HEAD_DIM // 2, 1),
                        pltpu.roll(yg, HEAD_DIM // 2, 1))
        return yg * cos + rot * sin

    subs = [pl.ds(r * ROWBLOCK, ROWBLOCK) for r in range(tm // ROWBLOCK)]
    hs = [hidden(rows) for rows in subs]
    accs = [[jnp.dot(hb, w_ref[:, c0:c1], preferred_element_type=F32) for c0, c1 in IN_GROUPS]
            for hb in hs]
    for r, rows in enumerate(subs):
        a_gqa, a_vn, a_na = accs[r]
        cos, sin = cos_ref[rows, :], sin_ref[rows, :]
        for g in range(GQA_WIDTH // LANES):
            cols = slice(COL_QG + g * LANES, COL_QG + (g + 1) * LANES)
            qg_ref[rows, g * LANES:(g + 1) * LANES] = (
                norm_rope(a_gqa[:, cols], gq_ref[...], cos, sin) * scale).astype(BF16)
        kr = norm_rope(a_gqa[:, COL_KG:COL_KG + KV_WIDTH], gk_ref[...], cos, sin)
        ka_ref[rows, :] = kr.astype(BF16)
        kb_ref[rows, :] = pltpu.roll(kr, HEAD_DIM, 1).astype(BF16)
        for j in range(NA_WIDTH // LANES):
            vt_ref[r, j * LANES:(j + 1) * LANES, :] = a_vn[:, j * LANES:(j + 1) * LANES].T.astype(BF16)
        vt_ref[r, NA_WIDTH:V_WIDTH, :] = a_gqa[:, COL_VG:COL_VG + KV_WIDTH].T.astype(BF16)
        qn_ref[rows, :] = (a_na[:, 0:NA_WIDTH] * scale).astype(BF16)
        kn_ref[rows, :] = a_na[:, NA_WIDTH:2 * NA_WIDTH].astype(BF16)


def _in_proj(x, mod, g_attn, w_in, cos_t, sin_t, gq_t, gk_t, tm):
    B, T, _ = x.shape
    nrb = T // ROWBLOCK
    row = lambda b, i: (b, i, 0)
    const2 = lambda b, i: (0, 0)
    out_shape = (
        jax.ShapeDtypeStruct((B, T, NA_WIDTH), BF16),
        jax.ShapeDtypeStruct((B, T, NA_WIDTH), BF16),
        jax.ShapeDtypeStruct((B, T, GQA_WIDTH), BF16),
        jax.ShapeDtypeStruct((B, T, KV_WIDTH), BF16),
        jax.ShapeDtypeStruct((B, T, KV_WIDTH), BF16),
        jax.ShapeDtypeStruct((B, nrb, V_WIDTH, ROWBLOCK), BF16),
    )
    return pl.pallas_call(
        functools.partial(_in_kernel, tm=tm),
        out_shape=out_shape,
        grid=(B, T // tm),
        in_specs=[
            pl.BlockSpec((None, tm, D_MODEL), row),
            pl.BlockSpec((None, N_MOD, D_MODEL), lambda b, i: (b, 0, 0)),
            pl.BlockSpec((1, D_MODEL), const2),
            pl.BlockSpec((D_MODEL, IN_WIDTH), const2, pipeline_mode=pl.Buffered(1)),
            pl.BlockSpec((tm, LANES), lambda b, i: (i, 0)),
            pl.BlockSpec((tm, LANES), lambda b, i: (i, 0)),
            pl.BlockSpec((1, LANES), const2),
            pl.BlockSpec((1, LANES), const2),
        ],
        out_specs=(
            pl.BlockSpec((None, tm, NA_WIDTH), row),
            pl.BlockSpec((None, tm, NA_WIDTH), row),
            pl.BlockSpec((None, tm, GQA_WIDTH), row),
            pl.BlockSpec((None, tm, KV_WIDTH), row),
            pl.BlockSpec((None, tm, KV_WIDTH), row),
            pl.BlockSpec((None, tm // ROWBLOCK, V_WIDTH, ROWBLOCK), lambda b, i: (b, i, 0, 0)),
        ),
        compiler_params=pltpu.CompilerParams(
            dimension_semantics=("arbitrary", "arbitrary"), vmem_limit_bytes=VMEM_LIMIT),
        name="in_proj",
    )(x, mod, g_attn, w_in, cos_t, sin_t, gq_t, gk_t)


def _na_variant(variant, i, qr):
    if variant == 0:
        valid, di = i < NA_WIN_H, i - qr
    elif variant == 1:
        valid, di = 0 <= i - qr < NA_WIN_H, i - qr - ROWBLOCK_ROWS
    else:
        valid, di = ROWBLOCK_ROWS <= i < ROWBLOCK_ROWS + NA_WIN_H, i - qr - 2 * ROWBLOCK_ROWS
    return di + NA_WIN_H - 1 if valid else None


def _na_key_ranges(variant):
    n_rows = BAND_BLOCKS * ROWBLOCK_ROWS
    halves = []
    for e in range(ROWBLOCK_ROWS // 2):
        rows = [i for i in range(n_rows)
                if any(_na_variant(variant, i, 2 * e + d) is not None for d in range(2))]
        halves.append((rows[0] * GRID_W, (rows[-1] + 1) * GRID_W))
    return (min(r0 for r0, _ in halves), max(r1 for _, r1 in halves)), halves


def _na_kernel(rpb_ref, q_ref, k_ref, vt_ref, o_ref, cols_ref, tbl_ref, s0_ref, s1_ref, *, n_rb):
    n_row_off, n_col_off = 2 * NA_WIN_H - 1, 2 * NA_WIN_W - 1

    @pl.when(pl.program_id(1) == 0)
    def _build_bias():
        lane = lax.broadcasted_iota(jnp.int32, (GRID_W, LANES), 1)
        kc = lax.broadcasted_iota(jnp.int32, (GRID_W, LANES), 0)
        qc = lane & (GRID_W - 1)
        win = jnp.clip(qc - NA_WIN_W // 2, 0, GRID_W - NA_WIN_W)
        col_off = jnp.where((kc >= win) & (kc < win + NA_WIN_W), kc - qc + NA_WIN_W - 1, -1)
        hits = [col_off == o for o in range(n_col_off)]
        for hh in range(2):
            head = 2 * pl.program_id(0) + hh
            for r in range(n_row_off):
                t = jnp.full((GRID_W, LANES), MASKED, F32)
                for o in range(n_col_off):
                    t = jnp.where(hits[o], rpb_ref[head, r * n_col_off + o], t)
                cols_ref[hh, r] = t * LOG2E
        left = lane < GRID_W
        masked = jnp.full((GRID_W, LANES), MASKED, F32)
        for variant in range(3):
            for hh in range(2):
                for i in range(BAND_BLOCKS * ROWBLOCK_ROWS):
                    for qp in range(ROWBLOCK_ROWS // 2):
                        ro = [_na_variant(variant, i, 2 * qp + e) for e in range(2)]
                        t = [masked if r is None else cols_ref[hh, r] for r in ro]
                        tbl_ref[variant, hh, i * GRID_W:(i + 1) * GRID_W,
                                qp * LANES:(qp + 1) * LANES] = jnp.where(left, t[0], t[1])

    s_refs = (s0_ref, s1_ref)

    def band_start(rb):
        return min(max(rb - 1, 0), n_rb - BAND_BLOCKS)

    def variant_of(rb):
        return 0 if rb == 0 else (2 if rb == n_rb - 1 else 1)

    def scores(rb, slot):
        variant = variant_of(rb)
        (u0, u1), halves = _na_key_ranges(variant)
        k0 = band_start(rb) * ROWBLOCK
        q = q_ref[rb * ROWBLOCK:(rb + 1) * ROWBLOCK, :]
        kband = k_ref[k0 + u0:k0 + u1, :]
        qt = q.astype(F32).T.astype(BF16)
        other = jnp.zeros((HEAD_DIM, ROWBLOCK), BF16)
        for hh in range(2):
            qh = jnp.concatenate([qt[0:HEAD_DIM], other] if hh == 0 else [other, qt[HEAD_DIM:]],
                                 axis=0)
            d = jnp.dot(kband, qh, preferred_element_type=F32)
            for e, (r0, r1) in enumerate(halves):
                rows, lanes = slice(r0 - u0, r1 - u0), slice(e * LANES, (e + 1) * LANES)
                s_refs[slot][hh, rows, lanes] = d[rows, lanes] + tbl_ref[variant, hh, r0:r1, lanes]

    def softmax_pv(rb, slot):
        bs = band_start(rb)
        (u0, u1), halves = _na_key_ranges(variant_of(rb))
        outs = []
        for hh in range(2):
            cols = []
            for e, (r0, r1) in enumerate(halves):
                sl = (hh, slice(r0 - u0, r1 - u0), slice(e * LANES, (e + 1) * LANES))
                m = jnp.max(s_refs[slot][sl], axis=0, keepdims=True)
                p = jnp.exp2(s_refs[slot][sl] - m).astype(BF16)
                pieces = [jnp.zeros((r0, LANES), BF16), p, jnp.zeros((BAND - r1, LANES), BF16)]
                cols.append(jnp.concatenate([x for x in pieces if x.shape[0]], axis=0))
            pb = jnp.concatenate(cols, axis=1)
            o_aug = jnp.zeros((V_AUG, ROWBLOCK), F32)
            for j in range(u0 // ROWBLOCK, pl.cdiv(u1, ROWBLOCK)):
                vt = _with_ones(vt_ref[bs + j, hh * HEAD_DIM:(hh + 1) * HEAD_DIM, :])
                o_aug = o_aug + jnp.dot(vt, pb[j * ROWBLOCK:(j + 1) * ROWBLOCK, :],
                                        preferred_element_type=F32)
            outs.append(o_aug[0:HEAD_DIM] / o_aug[HEAD_DIM:HEAD_DIM + 1])
        o_pair = jnp.concatenate(outs, axis=0)
        o_ref[rb * ROWBLOCK:(rb + 1) * ROWBLOCK, :] = o_pair.T.astype(BF16)

    scores(0, 0)
    for rb in range(n_rb):
        if rb + 1 < n_rb:
            scores(rb + 1, (rb + 1) % 2)
        softmax_pv(rb, rb % 2)


def _na_attention(rpb, qn, kn, vt):
    B, T, _ = qn.shape
    n_rb = T // ROWBLOCK
    n_pairs = NA_HEADS // 2
    col = lambda p, b: (b, 0, p)
    return pl.pallas_call(
        functools.partial(_na_kernel, n_rb=n_rb),
        out_shape=jax.ShapeDtypeStruct((B, T, NA_WIDTH), BF16),
        grid=(n_pairs, B),
        in_specs=[
            pl.BlockSpec(memory_space=pltpu.SMEM),
            pl.BlockSpec((None, T, LANES), col),
            pl.BlockSpec((None, T, LANES), col),
            pl.BlockSpec((None, n_rb, LANES, ROWBLOCK), lambda p, b: (b, 0, p, 0)),
        ],
        out_specs=pl.BlockSpec((None, T, LANES), col),
        scratch_shapes=[pltpu.VMEM((2, 2 * NA_WIN_H - 1, GRID_W, LANES), F32),
                        pltpu.VMEM((3, 2, BAND, ROWBLOCK), F32),
                        pltpu.VMEM((2, BAND, ROWBLOCK), F32),
                        pltpu.VMEM((2, BAND, ROWBLOCK), F32)],
        compiler_params=pltpu.CompilerParams(
            dimension_semantics=("arbitrary", "arbitrary"), vmem_limit_bytes=VMEM_LIMIT),
        name="na_attn",
    )(rpb.reshape(NA_HEADS, -1), qn, kn, vt)


def _gqa_stage_queries(q_ref, qm_ref, tq):
    other = jnp.zeros((HEAD_DIM, tq), BF16)
    for g in range(GQA_HEADS // 2):
        q2t = q_ref[:, g * LANES:(g + 1) * LANES].astype(F32).T.astype(BF16)
        qm_ref[2 * g] = jnp.concatenate([q2t[0:HEAD_DIM], other], axis=0)
        qm_ref[2 * g + 1] = jnp.concatenate([other, q2t[HEAD_DIM:]], axis=0)


def _gqa_write_output(acc_ref, o_ref):
    for g in range(GQA_HEADS // 2):
        pair = []
        for e in range(2):
            a = acc_ref[2 * g + e]
            pair.append(a[0:HEAD_DIM] / a[HEAD_DIM:HEAD_DIM + 1])
        o_ref[:, g * LANES:(g + 1) * LANES] = jnp.concatenate(pair, axis=0).T.astype(BF16)


def _gqa_bounded_kernel(q_ref, ka_ref, kb_ref, vt_ref, o_ref, qm_ref, acc_ref, *, tq, n_chunks,
                        unroll, lookahead):
    _gqa_stage_queries(q_ref, qm_ref, tq)
    acc_ref[...] = jnp.zeros(acc_ref.shape, F32)

    def scores(c, h):
        kvh, half = h // GQA_GROUP, h % 2
        k_ref = ka_ref if half == kvh else kb_ref
        rows = pl.ds(pl.multiple_of(c * ROWBLOCK, ROWBLOCK), ROWBLOCK)
        return jnp.dot(k_ref[rows, :], qm_ref[h], preferred_element_type=F32)

    def chunk_group(j, carry):
        accs = [acc_ref[h] for h in range(GQA_HEADS)]
        items = [(unroll * j + i, h) for i in range(unroll) for h in range(GQA_HEADS)]
        pending = [scores(c, h) for c, h in items[:lookahead]]
        vt_aug = {}
        for n, (c, h) in enumerate(items):
            if n + lookahead < len(items):
                pending.append(scores(*items[n + lookahead]))
            kvh = h // GQA_GROUP
            if (n // GQA_HEADS, kvh) not in vt_aug:
                vt_aug[(n // GQA_HEADS, kvh)] = _with_ones(
                    vt_ref[c, kvh * HEAD_DIM:(kvh + 1) * HEAD_DIM, :])
            pb = jnp.exp2(pending.pop(0)).astype(BF16)
            accs[h] = accs[h] + jnp.dot(vt_aug[(n // GQA_HEADS, kvh)], pb,
                                        preferred_element_type=F32)
        for h in range(GQA_HEADS):
            acc_ref[h] = accs[h]
        return carry

    assert n_chunks % unroll == 0
    lax.fori_loop(0, n_chunks // unroll, chunk_group, 0)
    _gqa_write_output(acc_ref, o_ref)


def _gqa_kernel(q_ref, ka_ref, kb_ref, vt_ref, o_ref, qm_ref, s0_ref, s1_ref, m_ref, acc_ref, *,
                tq, kc, n_chunks, unroll):
    _gqa_stage_queries(q_ref, qm_ref, tq)
    m_ref[...] = jnp.full(m_ref.shape, MASKED, F32)
    acc_ref[...] = jnp.zeros(acc_ref.shape, F32)

    s_refs = (s0_ref, s1_ref)

    def scores(h, c, slot):
        kvh, half = h // GQA_GROUP, h % 2
        k_ref = ka_ref if half == kvh else kb_ref
        rows = pl.ds(pl.multiple_of(c * kc, kc), kc)
        s_refs[slot][h] = jnp.dot(k_ref[rows, :], qm_ref[h], preferred_element_type=F32)

    def softmax_pv(h, slot, vt_augs):
        m_old = m_ref[h]
        m_new = jnp.maximum(m_old, jnp.max(s_refs[slot][h], axis=0, keepdims=True))
        alpha = jnp.exp2(m_old - m_new)
        pb = jnp.exp2(s_refs[slot][h] - m_new).astype(BF16)
        pv = sum(jnp.dot(vt, pb[j * ROWBLOCK:(j + 1) * ROWBLOCK, :], preferred_element_type=F32)
                 for j, vt in enumerate(vt_augs))
        acc_ref[h] = alpha * acc_ref[h] + pv
        m_ref[h] = m_new

    def step(c, slot, prefetch):
        blocks = kc // ROWBLOCK
        vt_augs = [[_with_ones(vt_ref[c * blocks + j, kvh * HEAD_DIM:(kvh + 1) * HEAD_DIM, :])
                    for j in range(blocks)] for kvh in range(GQA_KV_HEADS)]
        if prefetch:
            scores(0, c + 1, 1 - slot)
        for h in range(GQA_HEADS):
            if prefetch and h + 1 < GQA_HEADS:
                scores(h + 1, c + 1, 1 - slot)
            softmax_pv(h, slot, vt_augs[h // GQA_GROUP])

    for h in range(GQA_HEADS):
        scores(h, 0, 0)

    def chunk_group(j, carry):
        for i in range(unroll):
            step(unroll * j + i, i % 2, True)
        return carry

    assert n_chunks % unroll == 0 and unroll % 2 == 0 and n_chunks >= 2 * unroll
    chunk_group(0, 0)
    lax.fori_loop(1, n_chunks // unroll - 1, chunk_group, 0)
    for c in range(n_chunks - unroll, n_chunks):
        step(c, c % 2, c + 1 < n_chunks)

    _gqa_write_output(acc_ref, o_ref)


def _gqa_call(kernel_fn, scratch_shapes, name, qg, ka, kb, vt, tq):
    B, T, _ = qg.shape
    n_rb = T // ROWBLOCK
    kv_block = NA_WIDTH // KV_WIDTH
    return pl.pallas_call(
        kernel_fn,
        out_shape=jax.ShapeDtypeStruct((B, T, GQA_WIDTH), BF16),
        grid=(B, T // tq),
        in_specs=[
            pl.BlockSpec((None, tq, GQA_WIDTH), lambda b, i: (b, i, 0)),
            pl.BlockSpec((None, T, KV_WIDTH), lambda b, i: (b, 0, 0)),
            pl.BlockSpec((None, T, KV_WIDTH), lambda b, i: (b, 0, 0)),
            pl.BlockSpec((None, n_rb, KV_WIDTH, ROWBLOCK), lambda b, i: (b, 0, kv_block, 0)),
        ],
        out_specs=pl.BlockSpec((None, tq, GQA_WIDTH), lambda b, i: (b, i, 0)),
        scratch_shapes=scratch_shapes,
        compiler_params=pltpu.CompilerParams(
            dimension_semantics=("arbitrary", "arbitrary"), vmem_limit_bytes=VMEM_LIMIT),
        name=name,
    )(qg, ka, kb, vt)


def _gqa_attention(qg, ka, kb, vt, tq, kc, unroll):
    T = qg.shape[1]
    return _gqa_call(
        functools.partial(_gqa_kernel, tq=tq, kc=kc, n_chunks=T // kc, unroll=unroll),
        [pltpu.VMEM((GQA_HEADS, LANES, tq), BF16),
         pltpu.VMEM((GQA_HEADS, kc, tq), F32),
         pltpu.VMEM((GQA_HEADS, kc, tq), F32),
         pltpu.VMEM((GQA_HEADS, 1, tq), F32),
         pltpu.VMEM((GQA_HEADS, V_AUG, tq), F32)],
        "gqa_attn", qg, ka, kb, vt, tq)


def _gqa_attention_bounded(qg, ka, kb, vt, tq, unroll):
    T = qg.shape[1]
    return _gqa_call(
        functools.partial(_gqa_bounded_kernel, tq=tq, n_chunks=T // ROWBLOCK, unroll=unroll,
                          lookahead=GQA_LOOKAHEAD),
        [pltpu.VMEM((GQA_HEADS, LANES, tq), BF16),
         pltpu.VMEM((GQA_HEADS, V_AUG, tq), F32)],
        "gqa_attn_bounded", qg, ka, kb, vt, tq)


def _post_kernel(x_ref, ona_ref, og_ref, mod_ref, wo_ref, gf_ref, wg_ref, wu_ref, wd_ref,
                 gfin_ref, o_ref, *, tm):
    gate_a, shift_f, gain_f, gate_f = (mod_ref[2:3, :], mod_ref[3:4, :], 1.0 + mod_ref[4:5, :],
                                       mod_ref[5:6, :])

    def rms(x):
        return x * lax.rsqrt(jnp.mean(x * x, axis=-1, keepdims=True) + EPS)

    assert tm % (2 * ROWBLOCK) == 0
    for first in range(0, tm, 2 * ROWBLOCK):
        subs = [pl.ds(first + r * ROWBLOCK, ROWBLOCK) for r in range(2)]
        attn = [jnp.dot(ona_ref[rows, :], wo_ref[0:NA_WIDTH, :], preferred_element_type=F32)
                + jnp.dot(og_ref[rows, :], wo_ref[NA_WIDTH:, :], preferred_element_type=F32)
                for rows in subs]
        x1 = [x_ref[rows, :] + gate_a * a for rows, a in zip(subs, attn)]
        hb = [((rms(v) * gf_ref[...]) * gain_f + shift_f).astype(BF16) for v in x1]
        gu = [(jnp.dot(h, wg_ref[...], preferred_element_type=F32),
               jnp.dot(h, wu_ref[...], preferred_element_type=F32)) for h in hb]
        ff = [(_silu(g) * u).astype(BF16) for g, u in gu]
        x2 = [v + gate_f * jnp.dot(f, wd_ref[...], preferred_element_type=F32)
              for v, f in zip(x1, ff)]
        for rows, v in zip(subs, x2):
            o_ref[rows, :] = rms(v) * gfin_ref[...]


def _post(x, o_na, o_g, mod, w_o, g_ffn, w_gate, w_up, w_down, g_final, tm):
    B, T, _ = x.shape
    d_ff = w_gate.shape[1]
    row = lambda b, i: (b, i, 0)
    const2 = lambda b, i: (0, 0)
    resident = pl.Buffered(1)
    return pl.pallas_call(
        functools.partial(_post_kernel, tm=tm),
        out_shape=jax.ShapeDtypeStruct((B, T, D_MODEL), F32),
        grid=(B, T // tm),
        in_specs=[
            pl.BlockSpec((None, tm, D_MODEL), row),
            pl.BlockSpec((None, tm, NA_WIDTH), row),
            pl.BlockSpec((None, tm, GQA_WIDTH), row),
            pl.BlockSpec((None, N_MOD, D_MODEL), lambda b, i: (b, 0, 0)),
            pl.BlockSpec((D_MODEL, D_MODEL), const2, pipeline_mode=resident),
            pl.BlockSpec((1, D_MODEL), const2),
            pl.BlockSpec((D_MODEL, d_ff), const2, pipeline_mode=resident),
            pl.BlockSpec((D_MODEL, d_ff), const2, pipeline_mode=resident),
            pl.BlockSpec((d_ff, D_MODEL), const2, pipeline_mode=resident),
            pl.BlockSpec((1, D_MODEL), const2),
        ],
        out_specs=pl.BlockSpec((None, tm, D_MODEL), row),
        compiler_params=pltpu.CompilerParams(
            dimension_semantics=("arbitrary", "arbitrary"), vmem_limit_bytes=VMEM_LIMIT),
        name="post",
    )(x, o_na, o_g, mod, w_o, g_ffn, w_gate, w_up, w_down, g_final)


def _rope_tables(n_tokens):
    t = np.arange(n_tokens)
    row = (t // GRID_W).astype(np.float64)
    col = (t % GRID_W).astype(np.float64)

    def angles(pos, dims):
        inv = ROPE_THETA ** (-np.arange(0, dims, 2, dtype=np.float64) / dims)
        return pos[:, None] * inv[None, :]

    ang = np.concatenate([angles(row, HEAD_DIM // 2), angles(col, HEAD_DIM // 2)], axis=-1)
    cos, sin = np.cos(ang), np.sin(ang)
    reps = LANES // HEAD_DIM
    return (jnp.asarray(np.tile(np.concatenate([cos, cos], axis=-1), (1, reps)), F32),
            jnp.asarray(np.tile(np.concatenate([-sin, sin], axis=-1), (1, reps)), F32))


def _permute_in_columns(w):
    q_na, k_na, v_na, q_g, k_g, v_g = jnp.split(
        w, (NA_WIDTH, 2 * NA_WIDTH, 3 * NA_WIDTH, 3 * NA_WIDTH + GQA_WIDTH,
            3 * NA_WIDTH + GQA_WIDTH + KV_WIDTH), axis=-1)
    return jnp.concatenate([q_g, k_g, v_g, v_na, q_na, k_na], axis=-1)


def kernel(x, c, w_ada, b_ada, g_attn, w_in, g_q, g_k, rpb, w_o, g_ffn, w_gate, w_up, w_down, g_final):
    B, T, _ = x.shape
    assert w_ada.shape[0] == 1, "single-layer block: the final norm is fused into the layer"
    cos_t, sin_t = _rope_tables(T)
    reps = LANES // HEAD_DIM
    mod = _ada(c, w_ada[0], b_ada[0]).reshape(B, N_MOD, D_MODEL)
    qn, kn, qg, ka, kb, vt = _in_proj(
        x, mod, g_attn[0].reshape(1, D_MODEL), _permute_in_columns(w_in[0]).astype(BF16),
        cos_t, sin_t,
        jnp.tile(g_q[0], reps).reshape(1, LANES), jnp.tile(g_k[0], reps).reshape(1, LANES),
        tm=IN_ROWS)
    o_na = _na_attention(rpb[0], qn, kn, vt)
    score_bound = (jnp.max(jnp.abs(g_q[0])) * jnp.max(jnp.abs(g_k[0]))
                   * (HEAD_DIM * HEAD_DIM ** -0.5 * LOG2E * 1.02))
    o_g = lax.cond(
        score_bound <= GQA_SCORE_LIMIT,
        lambda *a: _gqa_attention_bounded(*a, tq=GQA_QUERIES, unroll=GQA_UNROLL_BOUNDED),
        lambda *a: _gqa_attention(*a, tq=GQA_QUERIES, kc=GQA_KEYS, unroll=GQA_UNROLL_ONLINE),
        qg, ka, kb, vt)
    return _post(x, o_na, o_g, mod, w_o[0].astype(BF16), g_ffn[0].reshape(1, D_MODEL),
                 w_gate[0].astype(BF16), w_up[0].astype(BF16), w_down[0].astype(BF16),
                 g_final.reshape(1, D_MODEL), tm=POST_ROWS)
```

```python
import functools

import jax
import jax.numpy as jnp
import numpy as np
from jax import lax
from jax.experimental import pallas as pl
from jax.experimental.pallas import tpu as pltpu

F32 = jnp.float32
BF16 = jnp.bfloat16

D_MODEL = 1024
HEAD_DIM = 64
NA_HEADS = 8
GQA_HEADS = 8
GQA_KV_HEADS = 2
GQA_GROUP = GQA_HEADS // GQA_KV_HEADS
NA_WIDTH = NA_HEADS * HEAD_DIM
GQA_WIDTH = GQA_HEADS * HEAD_DIM
KV_WIDTH = GQA_KV_HEADS * HEAD_DIM
IN_WIDTH = 3 * NA_WIDTH + GQA_WIDTH + 2 * KV_WIDTH
V_WIDTH = NA_WIDTH + KV_WIDTH
GRID_W = 64
NA_WIN_H = 8
NA_WIN_W = 16
ROPE_THETA = 10000.0
N_MOD = 6
EPS = 1e-6

LANES = 128
SUBLANES = 8
BF16_SUBLANES = 16
ROWBLOCK_ROWS = 4
ROWBLOCK = ROWBLOCK_ROWS * GRID_W
BAND_BLOCKS = 3
BAND = BAND_BLOCKS * ROWBLOCK
V_AUG = HEAD_DIM + BF16_SUBLANES
MASKED = -1e30
LOG2E = 1.4426950408889634
GQA_SCORE_LIMIT = 60.0
NA_DENOM_RANGE = (2.0 ** -60, 2.0 ** 100)
VMEM_LIMIT = 56 * 1024 * 1024

IN_ROWS = 1024
POST_ROWS = 1024
GQA_QUERIES = 512
GQA_KEYS = ROWBLOCK
GQA_UNROLL_ONLINE = 4
GQA_UNROLL_BOUNDED = 8
GQA_LOOKAHEAD = 4
NA_LOOKAHEAD = 2

COL_QG, COL_KG, COL_VG, COL_VN, COL_QN, COL_KN = 0, 512, 640, 768, 1280, 1792
IN_GROUPS = ((0, COL_VN), (COL_VN, COL_QN), (COL_QN, IN_WIDTH))


def _silu(x):
    return x / (1.0 + jnp.exp(-x))


def _with_ones(vt):
    return jnp.concatenate([vt, jnp.ones((BF16_SUBLANES, vt.shape[1]), vt.dtype)], axis=0)


def _ada_kernel(c_ref, w_ref, b_ref, o_ref):
    ca = _silu(c_ref[...])
    o_ref[...] = jnp.dot(ca.astype(BF16), w_ref[...].astype(BF16),
                         preferred_element_type=F32) + b_ref[...]


def _ada(c, w_ada, b_ada):
    B = c.shape[0]
    n = w_ada.shape[1]
    return pl.pallas_call(
        _ada_kernel,
        out_shape=jax.ShapeDtypeStruct((B, n), F32),
        grid=(n // D_MODEL,),
        in_specs=[pl.BlockSpec((B, D_MODEL), lambda j: (0, 0)),
                  pl.BlockSpec((D_MODEL, D_MODEL), lambda j: (0, j)),
                  pl.BlockSpec((1, D_MODEL), lambda j: (0, j))],
        out_specs=pl.BlockSpec((B, D_MODEL), lambda j: (0, j)),
        compiler_params=pltpu.CompilerParams(dimension_semantics=("arbitrary",)),
        name="ada",
    )(c, w_ada, b_ada.reshape(1, n))


def _in_kernel(x_ref, mod_ref, g_ref, w_ref, cos_ref, sin_ref, gq_ref, gk_ref,
               qn_ref, kn_ref, qg_ref, ka_ref, kb_ref, vt_ref, *, tm):
    scale = HEAD_DIM ** -0.5 * LOG2E
    lane = lax.broadcasted_iota(jnp.int32, (ROWBLOCK, LANES), 1)
    low_head = lane < HEAD_DIM
    first_half = (lane & (HEAD_DIM - 1)) < HEAD_DIM // 2
    shift, gain = mod_ref[0:1, :], 1.0 + mod_ref[1:2, :]

    def hidden(rows):
        x = x_ref[rows, :]
        ms = jnp.mean(x * x, axis=-1, keepdims=True)
        return ((x * lax.rsqrt(ms + EPS) * g_ref[...]) * gain + shift).astype(BF16)

    def norm_rope(xg, head_gain, cos, sin):
        sq = xg * xg
        s_lo = jnp.sum(jnp.where(low_head, sq, 0.0), axis=-1, keepdims=True)
        s_hi = jnp.sum(jnp.where(low_head, 0.0, sq), axis=-1, keepdims=True)
        inv = lax.rsqrt(jnp.where(low_head, s_lo, s_hi) * (1.0 / HEAD_DIM) + EPS)
        yg = xg * inv * head_gain
        rot = jnp.where(first_half,
                        pltpu.roll(yg, LANES - HEAD_DIM // 2, 1),
                        pltpu.roll(yg, HEAD_DIM // 2, 1))
        return yg * cos + rot * sin

    subs = [pl.ds(r * ROWBLOCK, ROWBLOCK) for r in range(tm // ROWBLOCK)]
    hs = [hidden(rows) for rows in subs]
    accs = [[jnp.dot(hb, w_ref[:, c0:c1], preferred_element_type=F32) for c0, c1 in IN_GROUPS]
            for hb in hs]
    for r, rows in enumerate(subs):
        a_gqa, a_vn, a_na = accs[r]
        cos, sin = cos_ref[rows, :], sin_ref[rows, :]
        for g in range(GQA_WIDTH // LANES):
            cols = slice(COL_QG + g * LANES, COL_QG + (g + 1) * LANES)
            qg_ref[rows, g * LANES:(g + 1) * LANES] = (
                norm_rope(a_gqa[:, cols], gq_ref[...], cos, sin) * scale).astype(BF16)
        kr = norm_rope(a_gqa[:, COL_KG:COL_KG + KV_WIDTH], gk_ref[...], cos, sin)
        ka_ref[rows, :] = kr.astype(BF16)
        kb_ref[rows, :] = pltpu.roll(kr, HEAD_DIM, 1).astype(BF16)
        for j in range(NA_WIDTH // LANES):
            vt_ref[r, j * LANES:(j + 1) * LANES, :] = a_vn[:, j * LANES:(j + 1) * LANES].T.astype(BF16)
        vt_ref[r, NA_WIDTH:V_WIDTH, :] = a_gqa[:, COL_VG:COL_VG + KV_WIDTH].T.astype(BF16)
        qn_ref[rows, :] = (a_na[:, 0:NA_WIDTH] * scale).astype(BF16)
        kn_ref[rows, :] = a_na[:, NA_WIDTH:2 * NA_WIDTH].astype(BF16)


def _in_proj(x, mod, g_attn, w_in, cos_t, sin_t, gq_t, gk_t, tm):
    B, T, _ = x.shape
    nrb = T // ROWBLOCK
    row = lambda b, i: (b, i, 0)
    const2 = lambda b, i: (0, 0)
    out_shape = (
        jax.ShapeDtypeStruct((B, T, NA_WIDTH), BF16),
        jax.ShapeDtypeStruct((B, T, NA_WIDTH), BF16),
        jax.ShapeDtypeStruct((B, T, GQA_WIDTH), BF16),
        jax.ShapeDtypeStruct((B, T, KV_WIDTH), BF16),
        jax.ShapeDtypeStruct((B, T, KV_WIDTH), BF16),
        jax.ShapeDtypeStruct((B, nrb, V_WIDTH, ROWBLOCK), BF16),
    )
    return pl.pallas_call(
        functools.partial(_in_kernel, tm=tm),
        out_shape=out_shape,
        grid=(B, T // tm),
        in_specs=[
            pl.BlockSpec((None, tm, D_MODEL), row),
            pl.BlockSpec((None, N_MOD, D_MODEL), lambda b, i: (b, 0, 0)),
            pl.BlockSpec((1, D_MODEL), const2),
            pl.BlockSpec((D_MODEL, IN_WIDTH), const2, pipeline_mode=pl.Buffered(1)),
            pl.BlockSpec((tm, LANES), lambda b, i: (i, 0)),
            pl.BlockSpec((tm, LANES), lambda b, i: (i, 0)),
            pl.BlockSpec((1, LANES), const2),
            pl.BlockSpec((1, LANES), const2),
        ],
        out_specs=(
            pl.BlockSpec((None, tm, NA_WIDTH), row),
            pl.BlockSpec((None, tm, NA_WIDTH), row),
            pl.BlockSpec((None, tm, GQA_WIDTH), row),
            pl.BlockSpec((None, tm, KV_WIDTH), row),
            pl.BlockSpec((None, tm, KV_WIDTH), row),
            pl.BlockSpec((None, tm // ROWBLOCK, V_WIDTH, ROWBLOCK), lambda b, i: (b, i, 0, 0)),
        ),
        compiler_params=pltpu.CompilerParams(
            dimension_semantics=("arbitrary", "arbitrary"), vmem_limit_bytes=VMEM_LIMIT),
        name="in_proj",
    )(x, mod, g_attn, w_in, cos_t, sin_t, gq_t, gk_t)


def _na_variant(variant, i, qr):
    if variant == 0:
        valid, di = i < NA_WIN_H, i - qr
    elif variant == 1:
        valid, di = 0 <= i - qr < NA_WIN_H, i - qr - ROWBLOCK_ROWS
    else:
        valid, di = ROWBLOCK_ROWS <= i < ROWBLOCK_ROWS + NA_WIN_H, i - qr - 2 * ROWBLOCK_ROWS
    return di + NA_WIN_H - 1 if valid else None


def _na_key_ranges(variant):
    n_rows = BAND_BLOCKS * ROWBLOCK_ROWS
    halves = []
    for e in range(ROWBLOCK_ROWS // 2):
        rows = [i for i in range(n_rows)
                if any(_na_variant(variant, i, 2 * e + d) is not None for d in range(2))]
        halves.append((rows[0] * GRID_W, (rows[-1] + 1) * GRID_W))
    return (min(r0 for r0, _ in halves), max(r1 for _, r1 in halves)), halves


def _na_build_bias(rpb_ref, cols_ref, tbl_ref):
    n_row_off, n_col_off = 2 * NA_WIN_H - 1, 2 * NA_WIN_W - 1

    @pl.when(pl.program_id(1) == 0)
    def _build_bias():
        lane = lax.broadcasted_iota(jnp.int32, (GRID_W, LANES), 1)
        kc = lax.broadcasted_iota(jnp.int32, (GRID_W, LANES), 0)
        qc = lane & (GRID_W - 1)
        win = jnp.clip(qc - NA_WIN_W // 2, 0, GRID_W - NA_WIN_W)
        col_off = jnp.where((kc >= win) & (kc < win + NA_WIN_W), kc - qc + NA_WIN_W - 1, -1)
        hits = [col_off == o for o in range(n_col_off)]
        for hh in range(2):
            head = 2 * pl.program_id(0) + hh
            for r in range(n_row_off):
                t = jnp.full((GRID_W, LANES), MASKED, F32)
                for o in range(n_col_off):
                    t = jnp.where(hits[o], rpb_ref[head, r * n_col_off + o], t)
                cols_ref[hh, r] = t * LOG2E
        left = lane < GRID_W
        masked = jnp.full((GRID_W, LANES), MASKED, F32)
        for variant in range(3):
            for hh in range(2):
                for i in range(BAND_BLOCKS * ROWBLOCK_ROWS):
                    for qp in range(ROWBLOCK_ROWS // 2):
                        ro = [_na_variant(variant, i, 2 * qp + e) for e in range(2)]
                        t = [masked if r is None else cols_ref[hh, r] for r in ro]
                        tbl_ref[variant, hh, i * GRID_W:(i + 1) * GRID_W,
                                qp * LANES:(qp + 1) * LANES] = jnp.where(left, t[0], t[1])


def _na_band_start(rb, n_rb):
    return min(max(rb - 1, 0), n_rb - BAND_BLOCKS)


def _na_variant_of(rb, n_rb):
    return 0 if rb == 0 else (2 if rb == n_rb - 1 else 1)


def _na_products(q_ref, k_ref, rb, n_rb):
    (u0, u1), _ = _na_key_ranges(_na_variant_of(rb, n_rb))
    k0 = _na_band_start(rb, n_rb) * ROWBLOCK
    q = q_ref[rb * ROWBLOCK:(rb + 1) * ROWBLOCK, :]
    kband = k_ref[k0 + u0:k0 + u1, :]
    qt = q.astype(F32).T.astype(BF16)
    other = jnp.zeros((HEAD_DIM, ROWBLOCK), BF16)
    prods = []
    for hh in range(2):
        qh = jnp.concatenate([qt[0:HEAD_DIM], other] if hh == 0 else [other, qt[HEAD_DIM:]], axis=0)
        prods.append(jnp.dot(kband, qh, preferred_element_type=F32))
    return prods


def _na_weighted_values(vt_ref, cols, rb, hh, n_rb):
    (u0, u1), _ = _na_key_ranges(_na_variant_of(rb, n_rb))
    bs = _na_band_start(rb, n_rb)
    padded = []
    for r0, r1, p in cols:
        pieces = [jnp.zeros((r0, LANES), BF16), p, jnp.zeros((BAND - r1, LANES), BF16)]
        padded.append(jnp.concatenate([x for x in pieces if x.shape[0]], axis=0))
    pb = jnp.concatenate(padded, axis=1)
    o_aug = jnp.zeros((V_AUG, ROWBLOCK), F32)
    for j in range(u0 // ROWBLOCK, pl.cdiv(u1, ROWBLOCK)):
        vt = _with_ones(vt_ref[bs + j, hh * HEAD_DIM:(hh + 1) * HEAD_DIM, :])
        o_aug = o_aug + jnp.dot(vt, pb[j * ROWBLOCK:(j + 1) * ROWBLOCK, :],
                                preferred_element_type=F32)
    denom = o_aug[HEAD_DIM:HEAD_DIM + 1]
    return o_aug[0:HEAD_DIM] / denom, denom


def _na_bounded_kernel(rpb_ref, q_ref, k_ref, vt_ref, o_ref, stat_ref, cols_ref, tbl_ref, *, n_rb,
                       lookahead):
    _na_build_bias(rpb_ref, cols_ref, tbl_ref)
    pending = [_na_products(q_ref, k_ref, rb, n_rb) for rb in range(min(lookahead, n_rb))]
    denoms = []
    for rb in range(n_rb):
        if rb + lookahead < n_rb:
            pending.append(_na_products(q_ref, k_ref, rb + lookahead, n_rb))
        variant = _na_variant_of(rb, n_rb)
        (u0, u1), halves = _na_key_ranges(variant)
        outs = []
        for hh, d in enumerate(pending.pop(0)):
            cols = []
            for e, (r0, r1) in enumerate(halves):
                rows, lanes = slice(r0 - u0, r1 - u0), slice(e * LANES, (e + 1) * LANES)
                p = jnp.exp2(d[rows, lanes] + tbl_ref[variant, hh, r0:r1, lanes]).astype(BF16)
                cols.append((r0, r1, p))
            out, denom = _na_weighted_values(vt_ref, cols, rb, hh, n_rb)
            outs.append(out)
            denoms.append(denom)
        o_pair = jnp.concatenate(outs, axis=0)
        o_ref[rb * ROWBLOCK:(rb + 1) * ROWBLOCK, :] = o_pair.T.astype(BF16)
    stat_ref[...] = jnp.concatenate(
        [functools.reduce(jnp.minimum, denoms), functools.reduce(jnp.maximum, denoms),
         jnp.ones((SUBLANES - 2, ROWBLOCK), F32)], axis=0)


def _na_kernel(rpb_ref, q_ref, k_ref, vt_ref, o_ref, cols_ref, tbl_ref, s0_ref, s1_ref, *, n_rb):
    _na_build_bias(rpb_ref, cols_ref, tbl_ref)
    s_refs = (s0_ref, s1_ref)

    def scores(rb, slot):
        variant = _na_variant_of(rb, n_rb)
        (u0, u1), halves = _na_key_ranges(variant)
        for hh, d in enumerate(_na_products(q_ref, k_ref, rb, n_rb)):
            for e, (r0, r1) in enumerate(halves):
                rows, lanes = slice(r0 - u0, r1 - u0), slice(e * LANES, (e + 1) * LANES)
                s_refs[slot][hh, rows, lanes] = d[rows, lanes] + tbl_ref[variant, hh, r0:r1, lanes]

    def softmax_pv(rb, slot):
        (u0, u1), halves = _na_key_ranges(_na_variant_of(rb, n_rb))
        outs = []
        for hh in range(2):
            cols = []
            for e, (r0, r1) in enumerate(halves):
                sl = (hh, slice(r0 - u0, r1 - u0), slice(e * LANES, (e + 1) * LANES))
                m = jnp.max(s_refs[slot][sl], axis=0, keepdims=True)
                cols.append((r0, r1, jnp.exp2(s_refs[slot][sl] - m).astype(BF16)))
            outs.append(_na_weighted_values(vt_ref, cols, rb, hh, n_rb)[0])
        o_pair = jnp.concatenate(outs, axis=0)
        o_ref[rb * ROWBLOCK:(rb + 1) * ROWBLOCK, :] = o_pair.T.astype(BF16)

    scores(0, 0)
    for rb in range(n_rb):
        if rb + 1 < n_rb:
            scores(rb + 1, (rb + 1) % 2)
        softmax_pv(rb, rb % 2)


def _na_call(kernel_fn, with_stats, extra_scratch, name, rpb, qn, kn, vt):
    B, T, _ = qn.shape
    n_rb = T // ROWBLOCK
    n_pairs = NA_HEADS // 2
    col = lambda p, b: (b, 0, p)
    out_shape = [jax.ShapeDtypeStruct((B, T, NA_WIDTH), BF16)]
    out_specs = [pl.BlockSpec((None, T, LANES), col)]
    if with_stats:
        out_shape.append(jax.ShapeDtypeStruct((n_pairs, B, SUBLANES, ROWBLOCK), F32))
        out_specs.append(pl.BlockSpec((None, None, SUBLANES, ROWBLOCK), lambda p, b: (p, b, 0, 0)))
    return pl.pallas_call(
        functools.partial(kernel_fn, n_rb=n_rb),
        out_shape=tuple(out_shape),
        grid=(n_pairs, B),
        in_specs=[
            pl.BlockSpec(memory_space=pltpu.SMEM),
            pl.BlockSpec((None, T, LANES), col),
            pl.BlockSpec((None, T, LANES), col),
            pl.BlockSpec((None, n_rb, LANES, ROWBLOCK), lambda p, b: (b, 0, p, 0)),
        ],
        out_specs=tuple(out_specs),
        scratch_shapes=[pltpu.VMEM((2, 2 * NA_WIN_H - 1, GRID_W, LANES), F32),
                        pltpu.VMEM((3, 2, BAND, ROWBLOCK), F32)] + extra_scratch,
        compiler_params=pltpu.CompilerParams(
            dimension_semantics=("arbitrary", "arbitrary"), vmem_limit_bytes=VMEM_LIMIT),
        name=name,
    )(rpb.reshape(NA_HEADS, -1), qn, kn, vt)


def _na_attention(rpb, qn, kn, vt):
    scores = [pltpu.VMEM((2, BAND, ROWBLOCK), F32), pltpu.VMEM((2, BAND, ROWBLOCK), F32)]
    return _na_call(_na_kernel, False, scores, "na_attn", rpb, qn, kn, vt)[0]


def _na_attention_unshifted(rpb, qn, kn, vt):
    return _na_call(functools.partial(_na_bounded_kernel, lookahead=NA_LOOKAHEAD), True, [],
                    "na_attn_unshifted", rpb, qn, kn, vt)


def _gqa_stage_queries(q_ref, qm_ref, tq):
    other = jnp.zeros((HEAD_DIM, tq), BF16)
    for g in range(GQA_HEADS // 2):
        q2t = q_ref[:, g * LANES:(g + 1) * LANES].astype(F32).T.astype(BF16)
        qm_ref[2 * g] = jnp.concatenate([q2t[0:HEAD_DIM], other], axis=0)
        qm_ref[2 * g + 1] = jnp.concatenate([other, q2t[HEAD_DIM:]], axis=0)


def _gqa_write_output(acc_ref, o_ref):
    for g in range(GQA_HEADS // 2):
        pair = []
        for e in range(2):
            a = acc_ref[2 * g + e]
            pair.append(a[0:HEAD_DIM] / a[HEAD_DIM:HEAD_DIM + 1])
        o_ref[:, g * LANES:(g + 1) * LANES] = jnp.concatenate(pair, axis=0).T.astype(BF16)


def _gqa_bounded_kernel(q_ref, ka_ref, kb_ref, vt_ref, o_ref, qm_ref, acc_ref, *, tq, n_chunks,
                        unroll, lookahead):
    _gqa_stage_queries(q_ref, qm_ref, tq)
    acc_ref[...] = jnp.zeros(acc_ref.shape, F32)

    def scores(c, h):
        kvh, half = h // GQA_GROUP, h % 2
        k_ref = ka_ref if half == kvh else kb_ref
        rows = pl.ds(pl.multiple_of(c * ROWBLOCK, ROWBLOCK), ROWBLOCK)
        return jnp.dot(k_ref[rows, :], qm_ref[h], preferred_element_type=F32)

    def chunk_group(j, carry):
        accs = [acc_ref[h] for h in range(GQA_HEADS)]
        items = [(unroll * j + i, h) for i in range(unroll) for h in range(GQA_HEADS)]
        pending = [scores(c, h) for c, h in items[:lookahead]]
        vt_aug = {}
        for n, (c, h) in enumerate(items):
            if n + lookahead < len(items):
                pending.append(scores(*items[n + lookahead]))
            kvh = h // GQA_GROUP
            if (n // GQA_HEADS, kvh) not in vt_aug:
                vt_aug[(n // GQA_HEADS, kvh)] = _with_ones(
                    vt_ref[c, kvh * HEAD_DIM:(kvh + 1) * HEAD_DIM, :])
            pb = jnp.exp2(pending.pop(0)).astype(BF16)
            accs[h] = accs[h] + jnp.dot(vt_aug[(n // GQA_HEADS, kvh)], pb,
                                        preferred_element_type=F32)
        for h in range(GQA_HEADS):
            acc_ref[h] = accs[h]
        return carry

    assert n_chunks % unroll == 0
    lax.fori_loop(0, n_chunks // unroll, chunk_group, 0)
    _gqa_write_output(acc_ref, o_ref)


def _gqa_kernel(q_ref, ka_ref, kb_ref, vt_ref, o_ref, qm_ref, s0_ref, s1_ref, m_ref, acc_ref, *,
                tq, kc, n_chunks, unroll):
    _gqa_stage_queries(q_ref, qm_ref, tq)
    m_ref[...] = jnp.full(m_ref.shape, MASKED, F32)
    acc_ref[...] = jnp.zeros(acc_ref.shape, F32)

    s_refs = (s0_ref, s1_ref)

    def scores(h, c, slot):
        kvh, half = h // GQA_GROUP, h % 2
        k_ref = ka_ref if half == kvh else kb_ref
        rows = pl.ds(pl.multiple_of(c * kc, kc), kc)
        s_refs[slot][h] = jnp.dot(k_ref[rows, :], qm_ref[h], preferred_element_type=F32)

    def softmax_pv(h, slot, vt_augs):
        m_old = m_ref[h]
        m_new = jnp.maximum(m_old, jnp.max(s_refs[slot][h], axis=0, keepdims=True))
        alpha = jnp.exp2(m_old - m_new)
        pb = jnp.exp2(s_refs[slot][h] - m_new).astype(BF16)
        pv = sum(jnp.dot(vt, pb[j * ROWBLOCK:(j + 1) * ROWBLOCK, :], preferred_element_type=F32)
                 for j, vt in enumerate(vt_augs))
        acc_ref[h] = alpha * acc_ref[h] + pv
        m_ref[h] = m_new

    def step(c, slot, prefetch):
        blocks = kc // ROWBLOCK
        vt_augs = [[_with_ones(vt_ref[c * blocks + j, kvh * HEAD_DIM:(kvh + 1) * HEAD_DIM, :])
                    for j in range(blocks)] for kvh in range(GQA_KV_HEADS)]
        if prefetch:
            scores(0, c + 1, 1 - slot)
        for h in range(GQA_HEADS):
            if prefetch and h + 1 < GQA_HEADS:
                scores(h + 1, c + 1, 1 - slot)
            softmax_pv(h, slot, vt_augs[h // GQA_GROUP])

    for h in range(GQA_HEADS):
        scores(h, 0, 0)

    def chunk_group(j, carry):
        for i in range(unroll):
            step(unroll * j + i, i % 2, True)
        return carry

    assert n_chunks % unroll == 0 and unroll % 2 == 0 and n_chunks >= 2 * unroll
    chunk_group(0, 0)
    lax.fori_loop(1, n_chunks // unroll - 1, chunk_group, 0)
    for c in range(n_chunks - unroll, n_chunks):
        step(c, c % 2, c + 1 < n_chunks)

    _gqa_write_output(acc_ref, o_ref)


def _gqa_call(kernel_fn, scratch_shapes, name, qg, ka, kb, vt, tq):
    B, T, _ = qg.shape
    n_rb = T // ROWBLOCK
    kv_block = NA_WIDTH // KV_WIDTH
    return pl.pallas_call(
        kernel_fn,
        out_shape=jax.ShapeDtypeStruct((B, T, GQA_WIDTH), BF16),
        grid=(B, T // tq),
        in_specs=[
            pl.BlockSpec((None, tq, GQA_WIDTH), lambda b, i: (b, i, 0)),
            pl.BlockSpec((None, T, KV_WIDTH), lambda b, i: (b, 0, 0)),
            pl.BlockSpec((None, T, KV_WIDTH), lambda b, i: (b, 0, 0)),
            pl.BlockSpec((None, n_rb, KV_WIDTH, ROWBLOCK), lambda b, i: (b, 0, kv_block, 0)),
        ],
        out_specs=pl.BlockSpec((None, tq, GQA_WIDTH), lambda b, i: (b, i, 0)),
        scratch_shapes=scratch_shapes,
        compiler_params=pltpu.CompilerParams(
            dimension_semantics=("arbitrary", "arbitrary"), vmem_limit_bytes=VMEM_LIMIT),
        name=name,
    )(qg, ka, kb, vt)


def _gqa_attention(qg, ka, kb, vt, tq, kc, unroll):
    T = qg.shape[1]
    return _gqa_call(
        functools.partial(_gqa_kernel, tq=tq, kc=kc, n_chunks=T // kc, unroll=unroll),
        [pltpu.VMEM((GQA_HEADS, LANES, tq), BF16),
         pltpu.VMEM((GQA_HEADS, kc, tq), F32),
         pltpu.VMEM((GQA_HEADS, kc, tq), F32),
         pltpu.VMEM((GQA_HEADS, 1, tq), F32),
         pltpu.VMEM((GQA_HEADS, V_AUG, tq), F32)],
        "gqa_attn", qg, ka, kb, vt, tq)


def _gqa_attention_bounded(qg, ka, kb, vt, tq, unroll):
    T = qg.shape[1]
    return _gqa_call(
        functools.partial(_gqa_bounded_kernel, tq=tq, n_chunks=T // ROWBLOCK, unroll=unroll,
                          lookahead=GQA_LOOKAHEAD),
        [pltpu.VMEM((GQA_HEADS, LANES, tq), BF16),
         pltpu.VMEM((GQA_HEADS, V_AUG, tq), F32)],
        "gqa_attn_bounded", qg, ka, kb, vt, tq)


def _post_kernel(x_ref, ona_ref, og_ref, mod_ref, wo_ref, gf_ref, wg_ref, wu_ref, wd_ref,
                 gfin_ref, o_ref, *, tm):
    gate_a, shift_f, gain_f, gate_f = (mod_ref[2:3, :], mod_ref[3:4, :], 1.0 + mod_ref[4:5, :],
                                       mod_ref[5:6, :])

    def rms(x):
        return x * lax.rsqrt(jnp.mean(x * x, axis=-1, keepdims=True) + EPS)

    assert tm % (2 * ROWBLOCK) == 0
    for first in range(0, tm, 2 * ROWBLOCK):
        subs = [pl.ds(first + r * ROWBLOCK, ROWBLOCK) for r in range(2)]
        attn = [jnp.dot(ona_ref[rows, :], wo_ref[0:NA_WIDTH, :], preferred_element_type=F32)
                + jnp.dot(og_ref[rows, :], wo_ref[NA_WIDTH:, :], preferred_element_type=F32)
                for rows in subs]
        x1 = [x_ref[rows, :] + gate_a * a for rows, a in zip(subs, attn)]
        hb = [((rms(v) * gf_ref[...]) * gain_f + shift_f).astype(BF16) for v in x1]
        gu = [(jnp.dot(h, wg_ref[...], preferred_element_type=F32),
               jnp.dot(h, wu_ref[...], preferred_element_type=F32)) for h in hb]
        ff = [(_silu(g) * u).astype(BF16) for g, u in gu]
        x2 = [v + gate_f * jnp.dot(f, wd_ref[...], preferred_element_type=F32)
              for v, f in zip(x1, ff)]
        for rows, v in zip(subs, x2):
            o_ref[rows, :] = rms(v) * gfin_ref[...]


def _post(x, o_na, o_g, mod, w_o, g_ffn, w_gate, w_up, w_down, g_final, tm):
    B, T, _ = x.shape
    d_ff = w_gate.shape[1]
    row = lambda b, i: (b, i, 0)
    const2 = lambda b, i: (0, 0)
    resident = pl.Buffered(1)
    return pl.pallas_call(
        functools.partial(_post_kernel, tm=tm),
        out_shape=jax.ShapeDtypeStruct((B, T, D_MODEL), F32),
        grid=(B, T // tm),
        in_specs=[
            pl.BlockSpec((None, tm, D_MODEL), row),
            pl.BlockSpec((None, tm, NA_WIDTH), row),
            pl.BlockSpec((None, tm, GQA_WIDTH), row),
            pl.BlockSpec((None, N_MOD, D_MODEL), lambda b, i: (b, 0, 0)),
            pl.BlockSpec((D_MODEL, D_MODEL), const2, pipeline_mode=resident),
            pl.BlockSpec((1, D_MODEL), const2),
            pl.BlockSpec((D_MODEL, d_ff), const2, pipeline_mode=resident),
            pl.BlockSpec((D_MODEL, d_ff), const2, pipeline_mode=resident),
            pl.BlockSpec((d_ff, D_MODEL), const2, pipeline_mode=resident),
            pl.BlockSpec((1, D_MODEL), const2),
        ],
        out_specs=pl.BlockSpec((None, tm, D_MODEL), row),
        compiler_params=pltpu.CompilerParams(
            dimension_semantics=("arbitrary", "arbitrary"), vmem_limit_bytes=VMEM_LIMIT),
        name="post",
    )(x, o_na, o_g, mod, w_o, g_ffn, w_gate, w_up, w_down, g_final)


def _rope_tables(n_tokens):
    t = np.arange(n_tokens)
    row = (t // GRID_W).astype(np.float64)
    col = (t % GRID_W).astype(np.float64)

    def angles(pos, dims):
        inv = ROPE_THETA ** (-np.arange(0, dims, 2, dtype=np.float64) / dims)
        return pos[:, None] * inv[None, :]

    ang = np.concatenate([angles(row, HEAD_DIM // 2), angles(col, HEAD_DIM // 2)], axis=-1)
    cos, sin = np.cos(ang), np.sin(ang)
    reps = LANES // HEAD_DIM
    return (jnp.asarray(np.tile(np.concatenate([cos, cos], axis=-1), (1, reps)), F32),
            jnp.asarray(np.tile(np.concatenate([-sin, sin], axis=-1), (1, reps)), F32))


def _permute_in_columns(w):
    q_na, k_na, v_na, q_g, k_g, v_g = jnp.split(
        w, (NA_WIDTH, 2 * NA_WIDTH, 3 * NA_WIDTH, 3 * NA_WIDTH + GQA_WIDTH,
            3 * NA_WIDTH + GQA_WIDTH + KV_WIDTH), axis=-1)
    return jnp.concatenate([q_g, k_g, v_g, v_na, q_na, k_na], axis=-1)


def kernel(x, c, w_ada, b_ada, g_attn, w_in, g_q, g_k, rpb, w_o, g_ffn, w_gate, w_up, w_down, g_final):
    B, T, _ = x.shape
    assert w_ada.shape[0] == 1, "single-layer block: the final norm is fused into the layer"
    cos_t, sin_t = _rope_tables(T)
    reps = LANES // HEAD_DIM
    mod = _ada(c, w_ada[0], b_ada[0]).reshape(B, N_MOD, D_MODEL)
    qn, kn, qg, ka, kb, vt = _in_proj(
        x, mod, g_attn[0].reshape(1, D_MODEL), _permute_in_columns(w_in[0]).astype(BF16),
        cos_t, sin_t,
        jnp.tile(g_q[0], reps).reshape(1, LANES), jnp.tile(g_k[0], reps).reshape(1, LANES),
        tm=IN_ROWS)
    o_fast, denom_stats = _na_attention_unshifted(rpb[0], qn, kn, vt)
    verified = ((jnp.min(denom_stats[:, :, 0]) >= NA_DENOM_RANGE[0])
                & (jnp.max(denom_stats[:, :, 1]) <= NA_DENOM_RANGE[1]))
    o_na = lax.cond(verified, lambda o, *a: o, lambda o, *a: _na_attention(*a),
                    o_fast, rpb[0], qn, kn, vt)
    score_bound = (jnp.max(jnp.abs(g_q[0])) * jnp.max(jnp.abs(g_k[0]))
                   * (HEAD_DIM * HEAD_DIM ** -0.5 * LOG2E * 1.02))
    o_g = lax.cond(
        score_bound <= GQA_SCORE_LIMIT,
        lambda *a: _gqa_attention_bounded(*a, tq=GQA_QUERIES, unroll=GQA_UNROLL_BOUNDED),
        lambda *a: _gqa_attention(*a, tq=GQA_QUERIES, kc=GQA_KEYS, unroll=GQA_UNROLL_ONLINE),
        qg, ka, kb, vt)
    return _post(x, o_na, o_g, mod, w_o[0].astype(BF16), g_ffn[0].reshape(1, D_MODEL),
                 w_gate[0].astype(BF16), w_up[0].astype(BF16), w_down[0].astype(BF16),
                 g_final.reshape(1, D_MODEL), tm=POST_ROWS)
```

```python
import functools

import jax
import jax.numpy as jnp
import numpy as np
from jax import lax
from jax.experimental import pallas as pl
from jax.experimental.pallas import tpu as pltpu

F32 = jnp.float32
BF16 = jnp.bfloat16

D_MODEL = 1024
HEAD_DIM = 64
NA_HEADS = 8
GQA_HEADS = 8
GQA_KV_HEADS = 2
GQA_GROUP = GQA_HEADS // GQA_KV_HEADS
NA_WIDTH = NA_HEADS * HEAD_DIM
GQA_WIDTH = GQA_HEADS * HEAD_DIM
KV_WIDTH = GQA_KV_HEADS * HEAD_DIM
IN_WIDTH = 3 * NA_WIDTH + GQA_WIDTH + 2 * KV_WIDTH
V_WIDTH = NA_WIDTH + KV_WIDTH
GRID_W = 64
NA_WIN_H = 8
NA_WIN_W = 16
ROPE_THETA = 10000.0
N_MOD = 6
EPS = 1e-6

LANES = 128
SUBLANES = 8
BF16_SUBLANES = 16
ROWBLOCK_ROWS = 4
ROWBLOCK = ROWBLOCK_ROWS * GRID_W
BAND_BLOCKS = 3
BAND = BAND_BLOCKS * ROWBLOCK
V_AUG = HEAD_DIM + BF16_SUBLANES
MASKED = -1e30
LOG2E = 1.4426950408889634
GQA_SCORE_LIMIT = 60.0
NA_DENOM_RANGE = (2.0 ** -60, 2.0 ** 100)
VMEM_LIMIT = 56 * 1024 * 1024

IN_ROWS = 1024
POST_ROWS = 1024
GQA_QUERIES = 512
GQA_KEYS = ROWBLOCK
GQA_UNROLL_ONLINE = 4
GQA_UNROLL_BOUNDED = 8
GQA_LOOKAHEAD = 4
NA_LOOKAHEAD = 2

COL_QG, COL_KG, COL_VG, COL_VN, COL_QN, COL_KN = 0, 512, 640, 768, 1280, 1792
IN_GROUPS = ((0, COL_VN), (COL_VN, COL_QN), (COL_QN, IN_WIDTH))


def _silu(x):
    return x / (1.0 + jnp.exp(-x))


def _with_ones(vt):
    return jnp.concatenate([vt, jnp.ones((BF16_SUBLANES, vt.shape[1]), vt.dtype)], axis=0)


def _ada_kernel(c_ref, w_ref, b_ref, o_ref):
    ca = _silu(c_ref[...])
    o_ref[...] = jnp.dot(ca.astype(BF16), w_ref[...].astype(BF16),
                         preferred_element_type=F32) + b_ref[...]


def _ada(c, w_ada, b_ada):
    B = c.shape[0]
    n = w_ada.shape[1]
    return pl.pallas_call(
        _ada_kernel,
        out_shape=jax.ShapeDtypeStruct((B, n), F32),
        grid=(n // D_MODEL,),
        in_specs=[pl.BlockSpec((B, D_MODEL), lambda j: (0, 0)),
                  pl.BlockSpec((D_MODEL, D_MODEL), lambda j: (0, j)),
                  pl.BlockSpec((1, D_MODEL), lambda j: (0, j))],
        out_specs=pl.BlockSpec((B, D_MODEL), lambda j: (0, j)),
        compiler_params=pltpu.CompilerParams(dimension_semantics=("arbitrary",)),
        name="ada",
    )(c, w_ada, b_ada.reshape(1, n))


def _in_kernel(x_ref, mod_ref, g_ref, w_ref, cos_ref, sin_ref, gq_ref, gk_ref,
               qn_ref, kn_ref, qg_ref, ka_ref, kb_ref, vt_ref, *, tm):
    scale = HEAD_DIM ** -0.5 * LOG2E
    lane = lax.broadcasted_iota(jnp.int32, (ROWBLOCK, LANES), 1)
    low_head = lane < HEAD_DIM
    first_half = (lane & (HEAD_DIM - 1)) < HEAD_DIM // 2
    shift, gain = mod_ref[0:1, :], 1.0 + mod_ref[1:2, :]

    def hidden(rows):
        x = x_ref[rows, :]
        ms = jnp.mean(x * x, axis=-1, keepdims=True)
        return ((x * lax.rsqrt(ms + EPS) * g_ref[...]) * gain + shift).astype(BF16)

    def norm_rope(xg, head_gain, cos, sin):
        sq = xg * xg
        s_lo = jnp.sum(jnp.where(low_head, sq, 0.0), axis=-1, keepdims=True)
        s_hi = jnp.sum(jnp.where(low_head, 0.0, sq), axis=-1, keepdims=True)
        inv = lax.rsqrt(jnp.where(low_head, s_lo, s_hi) * (1.0 / HEAD_DIM) + EPS)
        yg = xg * inv * head_gain
        rot = jnp.where(first_half,
                        pltpu.roll(yg, LANES - HEAD_DIM // 2, 1),
                        pltpu.roll(yg, HEAD_DIM // 2, 1))
        return yg * cos + rot * sin

    subs = [pl.ds(r * ROWBLOCK, ROWBLOCK) for r in range(tm // ROWBLOCK)]
    hs = [hidden(rows) for rows in subs]
    accs = [[jnp.dot(hb, w_ref[:, c0:c1], preferred_element_type=F32) for c0, c1 in IN_GROUPS]
            for hb in hs]
    for r, rows in enumerate(subs):
        a_gqa, a_vn, a_na = accs[r]
        cos, sin = cos_ref[rows, :], sin_ref[rows, :]
        for g in range(GQA_WIDTH // LANES):
            cols = slice(COL_QG + g * LANES, COL_QG + (g + 1) * LANES)
            qg_ref[rows, g * LANES:(g + 1) * LANES] = (
                norm_rope(a_gqa[:, cols], gq_ref[...], cos, sin) * scale).astype(BF16)
        kr = norm_rope(a_gqa[:, COL_KG:COL_KG + KV_WIDTH], gk_ref[...], cos, sin)
        ka_ref[rows, :] = kr.astype(BF16)
        kb_ref[rows, :] = pltpu.roll(kr, HEAD_DIM, 1).astype(BF16)
        for j in range(NA_WIDTH // LANES):
            vt_ref[r, j * LANES:(j + 1) * LANES, :] = a_vn[:, j * LANES:(j + 1) * LANES].T.astype(BF16)
        vt_ref[r, NA_WIDTH:V_WIDTH, :] = a_gqa[:, COL_VG:COL_VG + KV_WIDTH].T.astype(BF16)
        qn_ref[rows, :] = (a_na[:, 0:NA_WIDTH] * scale).astype(BF16)
        kn_ref[rows, :] = a_na[:, NA_WIDTH:2 * NA_WIDTH].astype(BF16)


def _in_proj(x, mod, g_attn, w_in, cos_t, sin_t, gq_t, gk_t, tm):
    B, T, _ = x.shape
    nrb = T // ROWBLOCK
    row = lambda b, i: (b, i, 0)
    const2 = lambda b, i: (0, 0)
    out_shape = (
        jax.ShapeDtypeStruct((B, T, NA_WIDTH), BF16),
        jax.ShapeDtypeStruct((B, T, NA_WIDTH), BF16),
        jax.ShapeDtypeStruct((B, T, GQA_WIDTH), BF16),
        jax.ShapeDtypeStruct((B, T, KV_WIDTH), BF16),
        jax.ShapeDtypeStruct((B, T, KV_WIDTH), BF16),
        jax.ShapeDtypeStruct((B, nrb, V_WIDTH, ROWBLOCK), BF16),
    )
    return pl.pallas_call(
        functools.partial(_in_kernel, tm=tm),
        out_shape=out_shape,
        grid=(B, T // tm),
        in_specs=[
            pl.BlockSpec((None, tm, D_MODEL), row),
            pl.BlockSpec((None, N_MOD, D_MODEL), lambda b, i: (b, 0, 0)),
            pl.BlockSpec((1, D_MODEL), const2),
            pl.BlockSpec((D_MODEL, IN_WIDTH), const2, pipeline_mode=pl.Buffered(1)),
            pl.BlockSpec((tm, LANES), lambda b, i: (i, 0)),
            pl.BlockSpec((tm, LANES), lambda b, i: (i, 0)),
            pl.BlockSpec((1, LANES), const2),
            pl.BlockSpec((1, LANES), const2),
        ],
        out_specs=(
            pl.BlockSpec((None, tm, NA_WIDTH), row),
            pl.BlockSpec((None, tm, NA_WIDTH), row),
            pl.BlockSpec((None, tm, GQA_WIDTH), row),
            pl.BlockSpec((None, tm, KV_WIDTH), row),
            pl.BlockSpec((None, tm, KV_WIDTH), row),
            pl.BlockSpec((None, tm // ROWBLOCK, V_WIDTH, ROWBLOCK), lambda b, i: (b, i, 0, 0)),
        ),
        compiler_params=pltpu.CompilerParams(
            dimension_semantics=("arbitrary", "arbitrary"), vmem_limit_bytes=VMEM_LIMIT),
        name="in_proj",
    )(x, mod, g_attn, w_in, cos_t, sin_t, gq_t, gk_t)


def _na_variant(variant, i, qr):
    if variant == 0:
        valid, di = i < NA_WIN_H, i - qr
    elif variant == 1:
        valid, di = 0 <= i - qr < NA_WIN_H, i - qr - ROWBLOCK_ROWS
    else:
        valid, di = ROWBLOCK_ROWS <= i < ROWBLOCK_ROWS + NA_WIN_H, i - qr - 2 * ROWBLOCK_ROWS
    return di + NA_WIN_H - 1 if valid else None


def _na_key_ranges(variant):
    n_rows = BAND_BLOCKS * ROWBLOCK_ROWS
    halves = []
    for e in range(ROWBLOCK_ROWS // 2):
        rows = [i for i in range(n_rows)
                if any(_na_variant(variant, i, 2 * e + d) is not None for d in range(2))]
        halves.append((rows[0] * GRID_W, (rows[-1] + 1) * GRID_W))
    return (min(r0 for r0, _ in halves), max(r1 for _, r1 in halves)), halves


def _na_build_bias(rpb_ref, cols_ref, tbl_ref):
    n_row_off, n_col_off = 2 * NA_WIN_H - 1, 2 * NA_WIN_W - 1

    @pl.when(pl.program_id(1) == 0)
    def _build_bias():
        lane = lax.broadcasted_iota(jnp.int32, (GRID_W, LANES), 1)
        kc = lax.broadcasted_iota(jnp.int32, (GRID_W, LANES), 0)
        qc = lane & (GRID_W - 1)
        win = jnp.clip(qc - NA_WIN_W // 2, 0, GRID_W - NA_WIN_W)
        col_off = jnp.where((kc >= win) & (kc < win + NA_WIN_W), kc - qc + NA_WIN_W - 1, -1)
        hits = [col_off == o for o in range(n_col_off)]
        for hh in range(2):
            head = 2 * pl.program_id(0) + hh
            for r in range(n_row_off):
                t = jnp.full((GRID_W, LANES), MASKED, F32)
                for o in range(n_col_off):
                    t = jnp.where(hits[o], rpb_ref[head, r * n_col_off + o], t)
                cols_ref[hh, r] = t * LOG2E
        left = lane < GRID_W
        masked = jnp.full((GRID_W, LANES), MASKED, F32)
        for variant in range(3):
            for hh in range(2):
                for i in range(BAND_BLOCKS * ROWBLOCK_ROWS):
                    for qp in range(ROWBLOCK_ROWS // 2):
                        ro = [_na_variant(variant, i, 2 * qp + e) for e in range(2)]
                        t = [masked if r is None else cols_ref[hh, r] for r in ro]
                        tbl_ref[variant, hh, i * GRID_W:(i + 1) * GRID_W,
                                qp * LANES:(qp + 1) * LANES] = jnp.where(left, t[0], t[1])


def _na_band_start(rb, n_rb):
    return min(max(rb - 1, 0), n_rb - BAND_BLOCKS)


def _na_variant_of(rb, n_rb):
    return 0 if rb == 0 else (2 if rb == n_rb - 1 else 1)


def _na_products(q_ref, k_ref, rb, n_rb):
    (u0, u1), _ = _na_key_ranges(_na_variant_of(rb, n_rb))
    k0 = _na_band_start(rb, n_rb) * ROWBLOCK
    q = q_ref[rb * ROWBLOCK:(rb + 1) * ROWBLOCK, :]
    kband = k_ref[k0 + u0:k0 + u1, :]
    qt = q.astype(F32).T.astype(BF16)
    other = jnp.zeros((HEAD_DIM, ROWBLOCK), BF16)
    prods = []
    for hh in range(2):
        qh = jnp.concatenate([qt[0:HEAD_DIM], other] if hh == 0 else [other, qt[HEAD_DIM:]], axis=0)
        prods.append(jnp.dot(kband, qh, preferred_element_type=F32))
    return prods


def _na_weighted_values(vt_ref, cols, rb, hh, n_rb):
    (u0, u1), _ = _na_key_ranges(_na_variant_of(rb, n_rb))
    bs = _na_band_start(rb, n_rb)
    padded = []
    for r0, r1, p in cols:
        pieces = [jnp.zeros((r0, LANES), BF16), p, jnp.zeros((BAND - r1, LANES), BF16)]
        padded.append(jnp.concatenate([x for x in pieces if x.shape[0]], axis=0))
    pb = jnp.concatenate(padded, axis=1)
    o_aug = jnp.zeros((V_AUG, ROWBLOCK), F32)
    for j in range(u0 // ROWBLOCK, pl.cdiv(u1, ROWBLOCK)):
        vt = _with_ones(vt_ref[bs + j, hh * HEAD_DIM:(hh + 1) * HEAD_DIM, :])
        o_aug = o_aug + jnp.dot(vt, pb[j * ROWBLOCK:(j + 1) * ROWBLOCK, :],
                                preferred_element_type=F32)
    denom = o_aug[HEAD_DIM:HEAD_DIM + 1]
    return o_aug[0:HEAD_DIM] / denom, denom


def _na_bounded_kernel(rpb_ref, q_ref, k_ref, vt_ref, o_ref, stat_ref, cols_ref, tbl_ref, *, n_rb,
                       lookahead):
    _na_build_bias(rpb_ref, cols_ref, tbl_ref)
    pending = [_na_products(q_ref, k_ref, rb, n_rb) for rb in range(min(lookahead, n_rb))]
    denoms = []
    for rb in range(n_rb):
        if rb + lookahead < n_rb:
            pending.append(_na_products(q_ref, k_ref, rb + lookahead, n_rb))
        variant = _na_variant_of(rb, n_rb)
        (u0, u1), halves = _na_key_ranges(variant)
        outs = []
        for hh, d in enumerate(pending.pop(0)):
            cols = []
            for e, (r0, r1) in enumerate(halves):
                rows, lanes = slice(r0 - u0, r1 - u0), slice(e * LANES, (e + 1) * LANES)
                p = jnp.exp2(d[rows, lanes] + tbl_ref[variant, hh, r0:r1, lanes]).astype(BF16)
                cols.append((r0, r1, p))
            out, denom = _na_weighted_values(vt_ref, cols, rb, hh, n_rb)
            outs.append(out)
            denoms.append(denom)
        o_pair = jnp.concatenate(outs, axis=0)
        o_ref[rb * ROWBLOCK:(rb + 1) * ROWBLOCK, :] = o_pair.T.astype(BF16)
    stat_ref[...] = jnp.concatenate(
        [functools.reduce(jnp.minimum, denoms), functools.reduce(jnp.maximum, denoms),
         jnp.ones((SUBLANES - 2, ROWBLOCK), F32)], axis=0)


def _na_kernel(rpb_ref, q_ref, k_ref, vt_ref, o_ref, cols_ref, tbl_ref, s0_ref, s1_ref, *, n_rb):
    _na_build_bias(rpb_ref, cols_ref, tbl_ref)
    s_refs = (s0_ref, s1_ref)

    def scores(rb, slot):
        variant = _na_variant_of(rb, n_rb)
        (u0, u1), halves = _na_key_ranges(variant)
        for hh, d in enumerate(_na_products(q_ref, k_ref, rb, n_rb)):
            for e, (r0, r1) in enumerate(halves):
                rows, lanes = slice(r0 - u0, r1 - u0), slice(e * LANES, (e + 1) * LANES)
                s_refs[slot][hh, rows, lanes] = d[rows, lanes] + tbl_ref[variant, hh, r0:r1, lanes]

    def softmax_pv(rb, slot):
        (u0, u1), halves = _na_key_ranges(_na_variant_of(rb, n_rb))
        outs = []
        for hh in range(2):
            cols = []
            for e, (r0, r1) in enumerate(halves):
                sl = (hh, slice(r0 - u0, r1 - u0), slice(e * LANES, (e + 1) * LANES))
                m = jnp.max(s_refs[slot][sl], axis=0, keepdims=True)
                cols.append((r0, r1, jnp.exp2(s_refs[slot][sl] - m).astype(BF16)))
            outs.append(_na_weighted_values(vt_ref, cols, rb, hh, n_rb)[0])
        o_pair = jnp.concatenate(outs, axis=0)
        o_ref[rb * ROWBLOCK:(rb + 1) * ROWBLOCK, :] = o_pair.T.astype(BF16)

    scores(0, 0)
    for rb in range(n_rb):
        if rb + 1 < n_rb:
            scores(rb + 1, (rb + 1) % 2)
        softmax_pv(rb, rb % 2)


def _na_call(kernel_fn, with_stats, extra_scratch, name, rpb, qn, kn, vt):
    B, T, _ = qn.shape
    n_rb = T // ROWBLOCK
    n_pairs = NA_HEADS // 2
    col = lambda p, b: (b, 0, p)
    out_shape = [jax.ShapeDtypeStruct((B, T, NA_WIDTH), BF16)]
    out_specs = [pl.BlockSpec((None, T, LANES), col)]
    if with_stats:
        out_shape.append(jax.ShapeDtypeStruct((n_pairs, B, SUBLANES, ROWBLOCK), F32))
        out_specs.append(pl.BlockSpec((None, None, SUBLANES, ROWBLOCK), lambda p, b: (p, b, 0, 0)))
    return pl.pallas_call(
        functools.partial(kernel_fn, n_rb=n_rb),
        out_shape=tuple(out_shape),
        grid=(n_pairs, B),
        in_specs=[
            pl.BlockSpec(memory_space=pltpu.SMEM),
            pl.BlockSpec((None, T, LANES), col),
            pl.BlockSpec((None, T, LANES), col),
            pl.BlockSpec((None, n_rb, LANES, ROWBLOCK), lambda p, b: (b, 0, p, 0)),
        ],
        out_specs=tuple(out_specs),
        scratch_shapes=[pltpu.VMEM((2, 2 * NA_WIN_H - 1, GRID_W, LANES), F32),
                        pltpu.VMEM((3, 2, BAND, ROWBLOCK), F32)] + extra_scratch,
        compiler_params=pltpu.CompilerParams(
            dimension_semantics=("arbitrary", "arbitrary"), vmem_limit_bytes=VMEM_LIMIT),
        name=name,
    )(rpb.reshape(NA_HEADS, -1), qn, kn, vt)


def _na_attention(rpb, qn, kn, vt):
    scores = [pltpu.VMEM((2, BAND, ROWBLOCK), F32), pltpu.VMEM((2, BAND, ROWBLOCK), F32)]
    return _na_call(_na_kernel, False, scores, "na_attn", rpb, qn, kn, vt)[0]


def _na_attention_unshifted(rpb, qn, kn, vt):
    return _na_call(functools.partial(_na_bounded_kernel, lookahead=NA_LOOKAHEAD), True, [],
                    "na_attn_unshifted", rpb, qn, kn, vt)


def _gqa_stage_queries(q_ref, qm_ref, tq):
    other = jnp.zeros((HEAD_DIM, tq), BF16)
    for g in range(GQA_HEADS // 2):
        q2t = q_ref[:, g * LANES:(g + 1) * LANES].astype(F32).T.astype(BF16)
        qm_ref[2 * g] = jnp.concatenate([q2t[0:HEAD_DIM], other], axis=0)
        qm_ref[2 * g + 1] = jnp.concatenate([other, q2t[HEAD_DIM:]], axis=0)


def _gqa_write_output(acc_ref, o_ref):
    for g in range(GQA_HEADS // 2):
        pair = []
        for e in range(2):
            a = acc_ref[2 * g + e]
            pair.append(a[0:HEAD_DIM] / a[HEAD_DIM:HEAD_DIM + 1])
        o_ref[:, g * LANES:(g + 1) * LANES] = jnp.concatenate(pair, axis=0).T.astype(BF16)


def _gqa_bounded_kernel(q_ref, ka_ref, kb_ref, vt_ref, o_ref, qm_ref, acc_ref, den_ref, *, tq,
                        n_chunks, unroll, lookahead):
    _gqa_stage_queries(q_ref, qm_ref, tq)
    acc_ref[...] = jnp.zeros(acc_ref.shape, F32)
    den_ref[...] = jnp.zeros(den_ref.shape, F32)

    def scores(c, h):
        kvh, half = h // GQA_GROUP, h % 2
        k_ref = ka_ref if half == kvh else kb_ref
        rows = pl.ds(pl.multiple_of(c * ROWBLOCK, ROWBLOCK), ROWBLOCK)
        return jnp.dot(k_ref[rows, :], qm_ref[h], preferred_element_type=F32)

    def chunk_group(j, carry):
        accs = [acc_ref[h] for h in range(GQA_HEADS)]
        dens = [den_ref[h] for h in range(GQA_HEADS)]
        items = [(unroll * j + i, h) for i in range(unroll) for h in range(GQA_HEADS)]
        pending = [scores(c, h) for c, h in items[:lookahead]]
        for n, (c, h) in enumerate(items):
            if n + lookahead < len(items):
                pending.append(scores(*items[n + lookahead]))
            kvh = h // GQA_GROUP
            p = jnp.exp2(pending.pop(0))
            dens[h] = dens[h] + jnp.sum(p.reshape(ROWBLOCK // SUBLANES, SUBLANES, tq), axis=0)
            accs[h] = accs[h] + jnp.dot(vt_ref[c, kvh * HEAD_DIM:(kvh + 1) * HEAD_DIM, :],
                                        p.astype(BF16), preferred_element_type=F32)
        for h in range(GQA_HEADS):
            acc_ref[h] = accs[h]
            den_ref[h] = dens[h]
        return carry

    assert n_chunks % unroll == 0
    lax.fori_loop(0, n_chunks // unroll, chunk_group, 0)
    for g in range(GQA_HEADS // 2):
        pair = [acc_ref[h] / jnp.sum(den_ref[h], axis=0, keepdims=True)
                for h in (2 * g, 2 * g + 1)]
        o_ref[:, g * LANES:(g + 1) * LANES] = jnp.concatenate(pair, axis=0).T.astype(BF16)


def _gqa_kernel(q_ref, ka_ref, kb_ref, vt_ref, o_ref, qm_ref, s0_ref, s1_ref, m_ref, acc_ref, *,
                tq, kc, n_chunks, unroll):
    _gqa_stage_queries(q_ref, qm_ref, tq)
    m_ref[...] = jnp.full(m_ref.shape, MASKED, F32)
    acc_ref[...] = jnp.zeros(acc_ref.shape, F32)

    s_refs = (s0_ref, s1_ref)

    def scores(h, c, slot):
        kvh, half = h // GQA_GROUP, h % 2
        k_ref = ka_ref if half == kvh else kb_ref
        rows = pl.ds(pl.multiple_of(c * kc, kc), kc)
        s_refs[slot][h] = jnp.dot(k_ref[rows, :], qm_ref[h], preferred_element_type=F32)

    def softmax_pv(h, slot, vt_augs):
        m_old = m_ref[h]
        m_new = jnp.maximum(m_old, jnp.max(s_refs[slot][h], axis=0, keepdims=True))
        alpha = jnp.exp2(m_old - m_new)
        pb = jnp.exp2(s_refs[slot][h] - m_new).astype(BF16)
        pv = sum(jnp.dot(vt, pb[j * ROWBLOCK:(j + 1) * ROWBLOCK, :], preferred_element_type=F32)
                 for j, vt in enumerate(vt_augs))
        acc_ref[h] = alpha * acc_ref[h] + pv
        m_ref[h] = m_new

    def step(c, slot, prefetch):
        blocks = kc // ROWBLOCK
        vt_augs = [[_with_ones(vt_ref[c * blocks + j, kvh * HEAD_DIM:(kvh + 1) * HEAD_DIM, :])
                    for j in range(blocks)] for kvh in range(GQA_KV_HEADS)]
        if prefetch:
            scores(0, c + 1, 1 - slot)
        for h in range(GQA_HEADS):
            if prefetch and h + 1 < GQA_HEADS:
                scores(h + 1, c + 1, 1 - slot)
            softmax_pv(h, slot, vt_augs[h // GQA_GROUP])

    for h in range(GQA_HEADS):
        scores(h, 0, 0)

    def chunk_group(j, carry):
        for i in range(unroll):
            step(unroll * j + i, i % 2, True)
        return carry

    assert n_chunks % unroll == 0 and unroll % 2 == 0 and n_chunks >= 2 * unroll
    chunk_group(0, 0)
    lax.fori_loop(1, n_chunks // unroll - 1, chunk_group, 0)
    for c in range(n_chunks - unroll, n_chunks):
        step(c, c % 2, c + 1 < n_chunks)

    _gqa_write_output(acc_ref, o_ref)


def _gqa_call(kernel_fn, scratch_shapes, name, qg, ka, kb, vt, tq):
    B, T, _ = qg.shape
    n_rb = T // ROWBLOCK
    kv_block = NA_WIDTH // KV_WIDTH
    return pl.pallas_call(
        kernel_fn,
        out_shape=jax.ShapeDtypeStruct((B, T, GQA_WIDTH), BF16),
        grid=(B, T // tq),
        in_specs=[
            pl.BlockSpec((None, tq, GQA_WIDTH), lambda b, i: (b, i, 0)),
            pl.BlockSpec((None, T, KV_WIDTH), lambda b, i: (b, 0, 0)),
            pl.BlockSpec((None, T, KV_WIDTH), lambda b, i: (b, 0, 0)),
            pl.BlockSpec((None, n_rb, KV_WIDTH, ROWBLOCK), lambda b, i: (b, 0, kv_block, 0)),
        ],
        out_specs=pl.BlockSpec((None, tq, GQA_WIDTH), lambda b, i: (b, i, 0)),
        scratch_shapes=scratch_shapes,
        compiler_params=pltpu.CompilerParams(
            dimension_semantics=("arbitrary", "arbitrary"), vmem_limit_bytes=VMEM_LIMIT),
        name=name,
    )(qg, ka, kb, vt)


def _gqa_attention(qg, ka, kb, vt, tq, kc, unroll):
    T = qg.shape[1]
    return _gqa_call(
        functools.partial(_gqa_kernel, tq=tq, kc=kc, n_chunks=T // kc, unroll=unroll),
        [pltpu.VMEM((GQA_HEADS, LANES, tq), BF16),
         pltpu.VMEM((GQA_HEADS, kc, tq), F32),
         pltpu.VMEM((GQA_HEADS, kc, tq), F32),
         pltpu.VMEM((GQA_HEADS, 1, tq), F32),
         pltpu.VMEM((GQA_HEADS, V_AUG, tq), F32)],
        "gqa_attn", qg, ka, kb, vt, tq)


def _gqa_attention_bounded(qg, ka, kb, vt, tq, unroll):
    T = qg.shape[1]
    return _gqa_call(
        functools.partial(_gqa_bounded_kernel, tq=tq, n_chunks=T // ROWBLOCK, unroll=unroll,
                          lookahead=GQA_LOOKAHEAD),
        [pltpu.VMEM((GQA_HEADS, LANES, tq), BF16),
         pltpu.VMEM((GQA_HEADS, HEAD_DIM, tq), F32),
         pltpu.VMEM((GQA_HEADS, SUBLANES, tq), F32)],
        "gqa_attn_bounded", qg, ka, kb, vt, tq)


def _post_kernel(x_ref, ona_ref, og_ref, mod_ref, wo_ref, gf_ref, wg_ref, wu_ref, wd_ref,
                 gfin_ref, o_ref, *, tm):
    gate_a, shift_f, gain_f, gate_f = (mod_ref[2:3, :], mod_ref[3:4, :], 1.0 + mod_ref[4:5, :],
                                       mod_ref[5:6, :])

    def rms(x):
        return x * lax.rsqrt(jnp.mean(x * x, axis=-1, keepdims=True) + EPS)

    assert tm % (2 * ROWBLOCK) == 0
    for first in range(0, tm, 2 * ROWBLOCK):
        subs = [pl.ds(first + r * ROWBLOCK, ROWBLOCK) for r in range(2)]
        attn = [jnp.dot(ona_ref[rows, :], wo_ref[0:NA_WIDTH, :], preferred_element_type=F32)
                + jnp.dot(og_ref[rows, :], wo_ref[NA_WIDTH:, :], preferred_element_type=F32)
                for rows in subs]
        x1 = [x_ref[rows, :] + gate_a * a for rows, a in zip(subs, attn)]
        hb = [((rms(v) * gf_ref[...]) * gain_f + shift_f).astype(BF16) for v in x1]
        gu = [(jnp.dot(h, wg_ref[...], preferred_element_type=F32),
               jnp.dot(h, wu_ref[...], preferred_element_type=F32)) for h in hb]
        ff = [(_silu(g) * u).astype(BF16) for g, u in gu]
        x2 = [v + gate_f * jnp.dot(f, wd_ref[...], preferred_element_type=F32)
              for v, f in zip(x1, ff)]
        for rows, v in zip(subs, x2):
            o_ref[rows, :] = rms(v) * gfin_ref[...]


def _post(x, o_na, o_g, mod, w_o, g_ffn, w_gate, w_up, w_down, g_final, tm):
    B, T, _ = x.shape
    d_ff = w_gate.shape[1]
    row = lambda b, i: (b, i, 0)
    const2 = lambda b, i: (0, 0)
    resident = pl.Buffered(1)
    return pl.pallas_call(
        functools.partial(_post_kernel, tm=tm),
        out_shape=jax.ShapeDtypeStruct((B, T, D_MODEL), F32),
        grid=(B, T // tm),
        in_specs=[
            pl.BlockSpec((None, tm, D_MODEL), row),
            pl.BlockSpec((None, tm, NA_WIDTH), row),
            pl.BlockSpec((None, tm, GQA_WIDTH), row),
            pl.BlockSpec((None, N_MOD, D_MODEL), lambda b, i: (b, 0, 0)),
            pl.BlockSpec((D_MODEL, D_MODEL), const2, pipeline_mode=resident),
            pl.BlockSpec((1, D_MODEL), const2),
            pl.BlockSpec((D_MODEL, d_ff), const2, pipeline_mode=resident),
            pl.BlockSpec((D_MODEL, d_ff), const2, pipeline_mode=resident),
            pl.BlockSpec((d_ff, D_MODEL), const2, pipeline_mode=resident),
            pl.BlockSpec((1, D_MODEL), const2),
        ],
        out_specs=pl.BlockSpec((None, tm, D_MODEL), row),
        compiler_params=pltpu.CompilerParams(
            dimension_semantics=("arbitrary", "arbitrary"), vmem_limit_bytes=VMEM_LIMIT),
        name="post",
    )(x, o_na, o_g, mod, w_o, g_ffn, w_gate, w_up, w_down, g_final)


def _rope_tables(n_tokens):
    t = np.arange(n_tokens)
    row = (t // GRID_W).astype(np.float64)
    col = (t % GRID_W).astype(np.float64)

    def angles(pos, dims):
        inv = ROPE_THETA ** (-np.arange(0, dims, 2, dtype=np.float64) / dims)
        return pos[:, None] * inv[None, :]

    ang = np.concatenate([angles(row, HEAD_DIM // 2), angles(col, HEAD_DIM // 2)], axis=-1)
    cos, sin = np.cos(ang), np.sin(ang)
    reps = LANES // HEAD_DIM
    return (jnp.asarray(np.tile(np.concatenate([cos, cos], axis=-1), (1, reps)), F32),
            jnp.asarray(np.tile(np.concatenate([-sin, sin], axis=-1), (1, reps)), F32))


def _permute_in_columns(w):
    q_na, k_na, v_na, q_g, k_g, v_g = jnp.split(
        w, (NA_WIDTH, 2 * NA_WIDTH, 3 * NA_WIDTH, 3 * NA_WIDTH + GQA_WIDTH,
            3 * NA_WIDTH + GQA_WIDTH + KV_WIDTH), axis=-1)
    return jnp.concatenate([q_g, k_g, v_g, v_na, q_na, k_na], axis=-1)


def kernel(x, c, w_ada, b_ada, g_attn, w_in, g_q, g_k, rpb, w_o, g_ffn, w_gate, w_up, w_down, g_final):
    B, T, _ = x.shape
    assert w_ada.shape[0] == 1, "single-layer block: the final norm is fused into the layer"
    cos_t, sin_t = _rope_tables(T)
    reps = LANES // HEAD_DIM
    mod = _ada(c, w_ada[0], b_ada[0]).reshape(B, N_MOD, D_MODEL)
    qn, kn, qg, ka, kb, vt = _in_proj(
        x, mod, g_attn[0].reshape(1, D_MODEL), _permute_in_columns(w_in[0]).astype(BF16),
        cos_t, sin_t,
        jnp.tile(g_q[0], reps).reshape(1, LANES), jnp.tile(g_k[0], reps).reshape(1, LANES),
        tm=IN_ROWS)
    o_fast, denom_stats = _na_attention_unshifted(rpb[0], qn, kn, vt)
    verified = ((jnp.min(denom_stats[:, :, 0]) >= NA_DENOM_RANGE[0])
                & (jnp.max(denom_stats[:, :, 1]) <= NA_DENOM_RANGE[1]))
    o_na = lax.cond(verified, lambda o, *a: o, lambda o, *a: _na_attention(*a),
                    o_fast, rpb[0], qn, kn, vt)
    score_bound = (jnp.max(jnp.abs(g_q[0])) * jnp.max(jnp.abs(g_k[0]))
                   * (HEAD_DIM * HEAD_DIM ** -0.5 * LOG2E * 1.02))
    o_g = lax.cond(
        score_bound <= GQA_SCORE_LIMIT,
        lambda *a: _gqa_attention_bounded(*a, tq=GQA_QUERIES, unroll=GQA_UNROLL_BOUNDED),
        lambda *a: _gqa_attention(*a, tq=GQA_QUERIES, kc=GQA_KEYS, unroll=GQA_UNROLL_ONLINE),
        qg, ka, kb, vt)
    return _post(x, o_na, o_g, mod, w_o[0].astype(BF16), g_ffn[0].reshape(1, D_MODEL),
                 w_gate[0].astype(BF16), w_up[0].astype(BF16), w_down[0].astype(BF16),
                 g_final.reshape(1, D_MODEL), tm=POST_ROWS)
```

```python
import functools

import jax
import jax.numpy as jnp
import numpy as np
from jax import lax
from jax.experimental import pallas as pl
from jax.experimental.pallas import tpu as pltpu

F32 = jnp.float32
BF16 = jnp.bfloat16

D_MODEL = 1024
HEAD_DIM = 64
NA_HEADS = 8
GQA_HEADS = 8
GQA_KV_HEADS = 2
GQA_GROUP = GQA_HEADS // GQA_KV_HEADS
NA_WIDTH = NA_HEADS * HEAD_DIM
GQA_WIDTH = GQA_HEADS * HEAD_DIM
KV_WIDTH = GQA_KV_HEADS * HEAD_DIM
IN_WIDTH = 3 * NA_WIDTH + GQA_WIDTH + 2 * KV_WIDTH
V_WIDTH = NA_WIDTH + KV_WIDTH
GRID_W = 64
NA_WIN_H = 8
NA_WIN_W = 16
ROPE_THETA = 10000.0
N_MOD = 6
EPS = 1e-6

LANES = 128
SUBLANES = 8
BF16_SUBLANES = 16
ROWBLOCK_ROWS = 4
ROWBLOCK = ROWBLOCK_ROWS * GRID_W
BAND_BLOCKS = 3
BAND = BAND_BLOCKS * ROWBLOCK
V_AUG = HEAD_DIM + BF16_SUBLANES
MASKED = -1e30
LOG2E = 1.4426950408889634
GQA_SCORE_LIMIT = 60.0
NA_DENOM_RANGE = (2.0 ** -60, 2.0 ** 100)
VMEM_LIMIT = 56 * 1024 * 1024

IN_ROWS = 1024
POST_ROWS = 1024
GQA_QUERIES = 512
GQA_KEYS = ROWBLOCK
GQA_UNROLL_ONLINE = 4
GQA_UNROLL_BOUNDED = 16
GQA_LOOKAHEAD = 4
NA_LOOKAHEAD = 2

COL_QG, COL_KG, COL_VG, COL_VN, COL_QN, COL_KN = 0, 512, 640, 768, 1280, 1792
IN_GROUPS = ((0, COL_VN), (COL_VN, COL_QN), (COL_QN, IN_WIDTH))


def _silu(x):
    return x / (1.0 + jnp.exp(-x))


def _with_ones(vt):
    return jnp.concatenate([vt, jnp.ones((BF16_SUBLANES, vt.shape[1]), vt.dtype)], axis=0)


def _ada_kernel(c_ref, w_ref, b_ref, o_ref):
    ca = _silu(c_ref[...])
    o_ref[...] = jnp.dot(ca.astype(BF16), w_ref[...].astype(BF16),
                         preferred_element_type=F32) + b_ref[...]


def _ada(c, w_ada, b_ada):
    B = c.shape[0]
    n = w_ada.shape[1]
    return pl.pallas_call(
        _ada_kernel,
        out_shape=jax.ShapeDtypeStruct((B, n), F32),
        grid=(n // D_MODEL,),
        in_specs=[pl.BlockSpec((B, D_MODEL), lambda j: (0, 0)),
                  pl.BlockSpec((D_MODEL, D_MODEL), lambda j: (0, j)),
                  pl.BlockSpec((1, D_MODEL), lambda j: (0, j))],
        out_specs=pl.BlockSpec((B, D_MODEL), lambda j: (0, j)),
        compiler_params=pltpu.CompilerParams(dimension_semantics=("arbitrary",)),
        name="ada",
    )(c, w_ada, b_ada.reshape(1, n))


def _in_kernel(x_ref, mod_ref, g_ref, w_ref, cos_ref, sin_ref, gq_ref, gk_ref,
               qn_ref, kn_ref, qg_ref, ka_ref, kb_ref, vt_ref, *, tm):
    scale = HEAD_DIM ** -0.5 * LOG2E
    lane = lax.broadcasted_iota(jnp.int32, (ROWBLOCK, LANES), 1)
    low_head = lane < HEAD_DIM
    first_half = (lane & (HEAD_DIM - 1)) < HEAD_DIM // 2
    shift, gain = mod_ref[0:1, :], 1.0 + mod_ref[1:2, :]

    def hidden(rows):
        x = x_ref[rows, :]
        ms = jnp.mean(x * x, axis=-1, keepdims=True)
        return ((x * lax.rsqrt(ms + EPS) * g_ref[...]) * gain + shift).astype(BF16)

    def norm_rope(xg, head_gain, cos, sin):
        sq = xg * xg
        s_lo = jnp.sum(jnp.where(low_head, sq, 0.0), axis=-1, keepdims=True)
        s_hi = jnp.sum(jnp.where(low_head, 0.0, sq), axis=-1, keepdims=True)
        inv = lax.rsqrt(jnp.where(low_head, s_lo, s_hi) * (1.0 / HEAD_DIM) + EPS)
        yg = xg * inv * head_gain
        rot = jnp.where(first_half,
                        pltpu.roll(yg, LANES - HEAD_DIM // 2, 1),
                        pltpu.roll(yg, HEAD_DIM // 2, 1))
        return yg * cos + rot * sin

    subs = [pl.ds(r * ROWBLOCK, ROWBLOCK) for r in range(tm // ROWBLOCK)]
    hs = [hidden(rows) for rows in subs]
    accs = [[jnp.dot(hb, w_ref[:, c0:c1], preferred_element_type=F32) for c0, c1 in IN_GROUPS]
            for hb in hs]
    for r, rows in enumerate(subs):
        a_gqa, a_vn, a_na = accs[r]
        cos, sin = cos_ref[rows, :], sin_ref[rows, :]
        for g in range(GQA_WIDTH // LANES):
            cols = slice(COL_QG + g * LANES, COL_QG + (g + 1) * LANES)
            qg_ref[rows, g * LANES:(g + 1) * LANES] = (
                norm_rope(a_gqa[:, cols], gq_ref[...], cos, sin) * scale).astype(BF16)
        kr = norm_rope(a_gqa[:, COL_KG:COL_KG + KV_WIDTH], gk_ref[...], cos, sin)
        ka_ref[rows, :] = kr.astype(BF16)
        kb_ref[rows, :] = pltpu.roll(kr, HEAD_DIM, 1).astype(BF16)
        for j in range(NA_WIDTH // LANES):
            vt_ref[r, j * LANES:(j + 1) * LANES, :] = a_vn[:, j * LANES:(j + 1) * LANES].T.astype(BF16)
        vt_ref[r, NA_WIDTH:V_WIDTH, :] = a_gqa[:, COL_VG:COL_VG + KV_WIDTH].T.astype(BF16)
        qn_ref[rows, :] = (a_na[:, 0:NA_WIDTH] * scale).astype(BF16)
        kn_ref[rows, :] = a_na[:, NA_WIDTH:2 * NA_WIDTH].astype(BF16)


def _in_proj(x, mod, g_attn, w_in, cos_t, sin_t, gq_t, gk_t, tm):
    B, T, _ = x.shape
    nrb = T // ROWBLOCK
    row = lambda b, i: (b, i, 0)
    const2 = lambda b, i: (0, 0)
    out_shape = (
        jax.ShapeDtypeStruct((B, T, NA_WIDTH), BF16),
        jax.ShapeDtypeStruct((B, T, NA_WIDTH), BF16),
        jax.ShapeDtypeStruct((B, T, GQA_WIDTH), BF16),
        jax.ShapeDtypeStruct((B, T, KV_WIDTH), BF16),
        jax.ShapeDtypeStruct((B, T, KV_WIDTH), BF16),
        jax.ShapeDtypeStruct((B, nrb, V_WIDTH, ROWBLOCK), BF16),
    )
    return pl.pallas_call(
        functools.partial(_in_kernel, tm=tm),
        out_shape=out_shape,
        grid=(B, T // tm),
        in_specs=[
            pl.BlockSpec((None, tm, D_MODEL), row),
            pl.BlockSpec((None, N_MOD, D_MODEL), lambda b, i: (b, 0, 0)),
            pl.BlockSpec((1, D_MODEL), const2),
            pl.BlockSpec((D_MODEL, IN_WIDTH), const2, pipeline_mode=pl.Buffered(1)),
            pl.BlockSpec((tm, LANES), lambda b, i: (i, 0)),
            pl.BlockSpec((tm, LANES), lambda b, i: (i, 0)),
            pl.BlockSpec((1, LANES), const2),
            pl.BlockSpec((1, LANES), const2),
        ],
        out_specs=(
            pl.BlockSpec((None, tm, NA_WIDTH), row),
            pl.BlockSpec((None, tm, NA_WIDTH), row),
            pl.BlockSpec((None, tm, GQA_WIDTH), row),
            pl.BlockSpec((None, tm, KV_WIDTH), row),
            pl.BlockSpec((None, tm, KV_WIDTH), row),
            pl.BlockSpec((None, tm // ROWBLOCK, V_WIDTH, ROWBLOCK), lambda b, i: (b, i, 0, 0)),
        ),
        compiler_params=pltpu.CompilerParams(
            dimension_semantics=("arbitrary", "arbitrary"), vmem_limit_bytes=VMEM_LIMIT),
        name="in_proj",
    )(x, mod, g_attn, w_in, cos_t, sin_t, gq_t, gk_t)


def _na_variant(variant, i, qr):
    if variant == 0:
        valid, di = i < NA_WIN_H, i - qr
    elif variant == 1:
        valid, di = 0 <= i - qr < NA_WIN_H, i - qr - ROWBLOCK_ROWS
    else:
        valid, di = ROWBLOCK_ROWS <= i < ROWBLOCK_ROWS + NA_WIN_H, i - qr - 2 * ROWBLOCK_ROWS
    return di + NA_WIN_H - 1 if valid else None


def _na_key_ranges(variant):
    n_rows = BAND_BLOCKS * ROWBLOCK_ROWS
    halves = []
    for e in range(ROWBLOCK_ROWS // 2):
        rows = [i for i in range(n_rows)
                if any(_na_variant(variant, i, 2 * e + d) is not None for d in range(2))]
        halves.append((rows[0] * GRID_W, (rows[-1] + 1) * GRID_W))
    return (min(r0 for r0, _ in halves), max(r1 for _, r1 in halves)), halves


def _na_build_bias(rpb_ref, cols_ref, tbl_ref):
    n_row_off, n_col_off = 2 * NA_WIN_H - 1, 2 * NA_WIN_W - 1

    @pl.when(pl.program_id(1) == 0)
    def _build_bias():
        lane = lax.broadcasted_iota(jnp.int32, (GRID_W, LANES), 1)
        kc = lax.broadcasted_iota(jnp.int32, (GRID_W, LANES), 0)
        qc = lane & (GRID_W - 1)
        win = jnp.clip(qc - NA_WIN_W // 2, 0, GRID_W - NA_WIN_W)
        col_off = jnp.where((kc >= win) & (kc < win + NA_WIN_W), kc - qc + NA_WIN_W - 1, -1)
        hits = [col_off == o for o in range(n_col_off)]
        for hh in range(2):
            head = 2 * pl.program_id(0) + hh
            for r in range(n_row_off):
                t = jnp.full((GRID_W, LANES), MASKED, F32)
                for o in range(n_col_off):
                    t = jnp.where(hits[o], rpb_ref[head, r * n_col_off + o], t)
                cols_ref[hh, r] = t * LOG2E
        left = lane < GRID_W
        masked = jnp.full((GRID_W, LANES), MASKED, F32)
        for variant in range(3):
            for hh in range(2):
                for i in range(BAND_BLOCKS * ROWBLOCK_ROWS):
                    for qp in range(ROWBLOCK_ROWS // 2):
                        ro = [_na_variant(variant, i, 2 * qp + e) for e in range(2)]
                        t = [masked if r is None else cols_ref[hh, r] for r in ro]
                        tbl_ref[variant, hh, i * GRID_W:(i + 1) * GRID_W,
                                qp * LANES:(qp + 1) * LANES] = jnp.where(left, t[0], t[1])


def _na_band_start(rb, n_rb):
    return min(max(rb - 1, 0), n_rb - BAND_BLOCKS)


def _na_variant_of(rb, n_rb):
    return 0 if rb == 0 else (2 if rb == n_rb - 1 else 1)


def _na_products(q_ref, k_ref, rb, n_rb):
    (u0, u1), _ = _na_key_ranges(_na_variant_of(rb, n_rb))
    k0 = _na_band_start(rb, n_rb) * ROWBLOCK
    q = q_ref[rb * ROWBLOCK:(rb + 1) * ROWBLOCK, :]
    kband = k_ref[k0 + u0:k0 + u1, :]
    qt = q.astype(F32).T.astype(BF16)
    other = jnp.zeros((HEAD_DIM, ROWBLOCK), BF16)
    prods = []
    for hh in range(2):
        qh = jnp.concatenate([qt[0:HEAD_DIM], other] if hh == 0 else [other, qt[HEAD_DIM:]], axis=0)
        prods.append(jnp.dot(kband, qh, preferred_element_type=F32))
    return prods


def _na_weighted_values(vt_ref, cols, rb, hh, n_rb):
    (u0, u1), _ = _na_key_ranges(_na_variant_of(rb, n_rb))
    bs = _na_band_start(rb, n_rb)
    padded = []
    for r0, r1, p in cols:
        pieces = [jnp.zeros((r0, LANES), BF16), p, jnp.zeros((BAND - r1, LANES), BF16)]
        padded.append(jnp.concatenate([x for x in pieces if x.shape[0]], axis=0))
    pb = jnp.concatenate(padded, axis=1)
    o_aug = jnp.zeros((V_AUG, ROWBLOCK), F32)
    for j in range(u0 // ROWBLOCK, pl.cdiv(u1, ROWBLOCK)):
        vt = _with_ones(vt_ref[bs + j, hh * HEAD_DIM:(hh + 1) * HEAD_DIM, :])
        o_aug = o_aug + jnp.dot(vt, pb[j * ROWBLOCK:(j + 1) * ROWBLOCK, :],
                                preferred_element_type=F32)
    denom = o_aug[HEAD_DIM:HEAD_DIM + 1]
    return o_aug[0:HEAD_DIM] / denom, denom


def _na_bounded_kernel(rpb_ref, q_ref, k_ref, vt_ref, o_ref, stat_ref, cols_ref, tbl_ref, *, n_rb,
                       lookahead):
    _na_build_bias(rpb_ref, cols_ref, tbl_ref)
    pending = [_na_products(q_ref, k_ref, rb, n_rb) for rb in range(min(lookahead, n_rb))]
    denoms = []
    for rb in range(n_rb):
        if rb + lookahead < n_rb:
            pending.append(_na_products(q_ref, k_ref, rb + lookahead, n_rb))
        variant = _na_variant_of(rb, n_rb)
        (u0, u1), halves = _na_key_ranges(variant)
        outs = []
        for hh, d in enumerate(pending.pop(0)):
            cols = []
            for e, (r0, r1) in enumerate(halves):
                rows, lanes = slice(r0 - u0, r1 - u0), slice(e * LANES, (e + 1) * LANES)
                p = jnp.exp2(d[rows, lanes] + tbl_ref[variant, hh, r0:r1, lanes]).astype(BF16)
                cols.append((r0, r1, p))
            out, denom = _na_weighted_values(vt_ref, cols, rb, hh, n_rb)
            outs.append(out)
            denoms.append(denom)
        o_pair = jnp.concatenate(outs, axis=0)
        o_ref[rb * ROWBLOCK:(rb + 1) * ROWBLOCK, :] = o_pair.T.astype(BF16)
    stat_ref[...] = jnp.concatenate(
        [functools.reduce(jnp.minimum, denoms), functools.reduce(jnp.maximum, denoms),
         jnp.ones((SUBLANES - 2, ROWBLOCK), F32)], axis=0)


def _na_kernel(rpb_ref, q_ref, k_ref, vt_ref, o_ref, cols_ref, tbl_ref, s0_ref, s1_ref, *, n_rb):
    _na_build_bias(rpb_ref, cols_ref, tbl_ref)
    s_refs = (s0_ref, s1_ref)

    def scores(rb, slot):
        variant = _na_variant_of(rb, n_rb)
        (u0, u1), halves = _na_key_ranges(variant)
        for hh, d in enumerate(_na_products(q_ref, k_ref, rb, n_rb)):
            for e, (r0, r1) in enumerate(halves):
                rows, lanes = slice(r0 - u0, r1 - u0), slice(e * LANES, (e + 1) * LANES)
                s_refs[slot][hh, rows, lanes] = d[rows, lanes] + tbl_ref[variant, hh, r0:r1, lanes]

    def softmax_pv(rb, slot):
        (u0, u1), halves = _na_key_ranges(_na_variant_of(rb, n_rb))
        outs = []
        for hh in range(2):
            cols = []
            for e, (r0, r1) in enumerate(halves):
                sl = (hh, slice(r0 - u0, r1 - u0), slice(e * LANES, (e + 1) * LANES))
                m = jnp.max(s_refs[slot][sl], axis=0, keepdims=True)
                cols.append((r0, r1, jnp.exp2(s_refs[slot][sl] - m).astype(BF16)))
            outs.append(_na_weighted_values(vt_ref, cols, rb, hh, n_rb)[0])
        o_pair = jnp.concatenate(outs, axis=0)
        o_ref[rb * ROWBLOCK:(rb + 1) * ROWBLOCK, :] = o_pair.T.astype(BF16)

    scores(0, 0)
    for rb in range(n_rb):
        if rb + 1 < n_rb:
            scores(rb + 1, (rb + 1) % 2)
        softmax_pv(rb, rb % 2)


def _na_call(kernel_fn, with_stats, extra_scratch, name, rpb, qn, kn, vt):
    B, T, _ = qn.shape
    n_rb = T // ROWBLOCK
    n_pairs = NA_HEADS // 2
    col = lambda p, b: (b, 0, p)
    out_shape = [jax.ShapeDtypeStruct((B, T, NA_WIDTH), BF16)]
    out_specs = [pl.BlockSpec((None, T, LANES), col)]
    if with_stats:
        out_shape.append(jax.ShapeDtypeStruct((n_pairs, B, SUBLANES, ROWBLOCK), F32))
        out_specs.append(pl.BlockSpec((None, None, SUBLANES, ROWBLOCK), lambda p, b: (p, b, 0, 0)))
    return pl.pallas_call(
        functools.partial(kernel_fn, n_rb=n_rb),
        out_shape=tuple(out_shape),
        grid=(n_pairs, B),
        in_specs=[
            pl.BlockSpec(memory_space=pltpu.SMEM),
            pl.BlockSpec((None, T, LANES), col),
            pl.BlockSpec((None, T, LANES), col),
            pl.BlockSpec((None, n_rb, LANES, ROWBLOCK), lambda p, b: (b, 0, p, 0)),
        ],
        out_specs=tuple(out_specs),
        scratch_shapes=[pltpu.VMEM((2, 2 * NA_WIN_H - 1, GRID_W, LANES), F32),
                        pltpu.VMEM((3, 2, BAND, ROWBLOCK), F32)] + extra_scratch,
        compiler_params=pltpu.CompilerParams(
            dimension_semantics=("arbitrary", "arbitrary"), vmem_limit_bytes=VMEM_LIMIT),
        name=name,
    )(rpb.reshape(NA_HEADS, -1), qn, kn, vt)


def _na_attention(rpb, qn, kn, vt):
    scores = [pltpu.VMEM((2, BAND, ROWBLOCK), F32), pltpu.VMEM((2, BAND, ROWBLOCK), F32)]
    return _na_call(_na_kernel, False, scores, "na_attn", rpb, qn, kn, vt)[0]


def _na_attention_unshifted(rpb, qn, kn, vt):
    return _na_call(functools.partial(_na_bounded_kernel, lookahead=NA_LOOKAHEAD), True, [],
                    "na_attn_unshifted", rpb, qn, kn, vt)


def _gqa_stage_queries(q_ref, qm_ref, tq):
    other = jnp.zeros((HEAD_DIM, tq), BF16)
    for g in range(GQA_HEADS // 2):
        q2t = q_ref[:, g * LANES:(g + 1) * LANES].astype(F32).T.astype(BF16)
        qm_ref[2 * g] = jnp.concatenate([q2t[0:HEAD_DIM], other], axis=0)
        qm_ref[2 * g + 1] = jnp.concatenate([other, q2t[HEAD_DIM:]], axis=0)


def _gqa_write_output(acc_ref, o_ref):
    for g in range(GQA_HEADS // 2):
        pair = []
        for e in range(2):
            a = acc_ref[2 * g + e]
            pair.append(a[0:HEAD_DIM] / a[HEAD_DIM:HEAD_DIM + 1])
        o_ref[:, g * LANES:(g + 1) * LANES] = jnp.concatenate(pair, axis=0).T.astype(BF16)


def _gqa_bounded_kernel(q_ref, ka_ref, kb_ref, vt_ref, o_ref, qm_ref, acc_ref, den_ref, *, tq,
                        n_chunks, unroll, lookahead):
    _gqa_stage_queries(q_ref, qm_ref, tq)
    acc_ref[...] = jnp.zeros(acc_ref.shape, F32)
    den_ref[...] = jnp.zeros(den_ref.shape, F32)

    def scores(c, h):
        kvh, half = h // GQA_GROUP, h % 2
        k_ref = ka_ref if half == kvh else kb_ref
        rows = pl.ds(pl.multiple_of(c * ROWBLOCK, ROWBLOCK), ROWBLOCK)
        return jnp.dot(k_ref[rows, :], qm_ref[h], preferred_element_type=F32)

    def chunk_group(j, carry):
        accs = [acc_ref[h] for h in range(GQA_HEADS)]
        dens = [den_ref[h] for h in range(GQA_HEADS)]
        items = [(unroll * j + i, h) for i in range(unroll) for h in range(GQA_HEADS)]
        pending = [scores(c, h) for c, h in items[:lookahead]]
        for n, (c, h) in enumerate(items):
            if n + lookahead < len(items):
                pending.append(scores(*items[n + lookahead]))
            kvh = h // GQA_GROUP
            p = jnp.exp2(pending.pop(0))
            dens[h] = dens[h] + jnp.sum(p.reshape(ROWBLOCK // SUBLANES, SUBLANES, tq), axis=0)
            accs[h] = accs[h] + jnp.dot(vt_ref[c, kvh * HEAD_DIM:(kvh + 1) * HEAD_DIM, :],
                                        p.astype(BF16), preferred_element_type=F32)
        for h in range(GQA_HEADS):
            acc_ref[h] = accs[h]
            den_ref[h] = dens[h]
        return carry

    assert n_chunks % unroll == 0
    lax.fori_loop(0, n_chunks // unroll, chunk_group, 0)
    for g in range(GQA_HEADS // 2):
        pair = [acc_ref[h] / jnp.sum(den_ref[h], axis=0, keepdims=True)
                for h in (2 * g, 2 * g + 1)]
        o_ref[:, g * LANES:(g + 1) * LANES] = jnp.concatenate(pair, axis=0).T.astype(BF16)


def _gqa_kernel(q_ref, ka_ref, kb_ref, vt_ref, o_ref, qm_ref, s0_ref, s1_ref, m_ref, acc_ref, *,
                tq, kc, n_chunks, unroll):
    _gqa_stage_queries(q_ref, qm_ref, tq)
    m_ref[...] = jnp.full(m_ref.shape, MASKED, F32)
    acc_ref[...] = jnp.zeros(acc_ref.shape, F32)

    s_refs = (s0_ref, s1_ref)

    def scores(h, c, slot):
        kvh, half = h // GQA_GROUP, h % 2
        k_ref = ka_ref if half == kvh else kb_ref
        rows = pl.ds(pl.multiple_of(c * kc, kc), kc)
        s_refs[slot][h] = jnp.dot(k_ref[rows, :], qm_ref[h], preferred_element_type=F32)

    def softmax_pv(h, slot, vt_augs):
        m_old = m_ref[h]
        m_new = jnp.maximum(m_old, jnp.max(s_refs[slot][h], axis=0, keepdims=True))
        alpha = jnp.exp2(m_old - m_new)
        pb = jnp.exp2(s_refs[slot][h] - m_new).astype(BF16)
        pv = sum(jnp.dot(vt, pb[j * ROWBLOCK:(j + 1) * ROWBLOCK, :], preferred_element_type=F32)
                 for j, vt in enumerate(vt_augs))
        acc_ref[h] = alpha * acc_ref[h] + pv
        m_ref[h] = m_new

    def step(c, slot, prefetch):
        blocks = kc // ROWBLOCK
        vt_augs = [[_with_ones(vt_ref[c * blocks + j, kvh * HEAD_DIM:(kvh + 1) * HEAD_DIM, :])
                    for j in range(blocks)] for kvh in range(GQA_KV_HEADS)]
        if prefetch:
            scores(0, c + 1, 1 - slot)
        for h in range(GQA_HEADS):
            if prefetch and h + 1 < GQA_HEADS:
                scores(h + 1, c + 1, 1 - slot)
            softmax_pv(h, slot, vt_augs[h // GQA_GROUP])

    for h in range(GQA_HEADS):
        scores(h, 0, 0)

    def chunk_group(j, carry):
        for i in range(unroll):
            step(unroll * j + i, i % 2, True)
        return carry

    assert n_chunks % unroll == 0 and unroll % 2 == 0 and n_chunks >= 2 * unroll
    chunk_group(0, 0)
    lax.fori_loop(1, n_chunks // unroll - 1, chunk_group, 0)
    for c in range(n_chunks - unroll, n_chunks):
        step(c, c % 2, c + 1 < n_chunks)

    _gqa_write_output(acc_ref, o_ref)


def _gqa_call(kernel_fn, scratch_shapes, name, qg, ka, kb, vt, tq):
    B, T, _ = qg.shape
    n_rb = T // ROWBLOCK
    kv_block = NA_WIDTH // KV_WIDTH
    return pl.pallas_call(
        kernel_fn,
        out_shape=jax.ShapeDtypeStruct((B, T, GQA_WIDTH), BF16),
        grid=(B, T // tq),
        in_specs=[
            pl.BlockSpec((None, tq, GQA_WIDTH), lambda b, i: (b, i, 0)),
            pl.BlockSpec((None, T, KV_WIDTH), lambda b, i: (b, 0, 0)),
            pl.BlockSpec((None, T, KV_WIDTH), lambda b, i: (b, 0, 0)),
            pl.BlockSpec((None, n_rb, KV_WIDTH, ROWBLOCK), lambda b, i: (b, 0, kv_block, 0)),
        ],
        out_specs=pl.BlockSpec((None, tq, GQA_WIDTH), lambda b, i: (b, i, 0)),
        scratch_shapes=scratch_shapes,
        compiler_params=pltpu.CompilerParams(
            dimension_semantics=("arbitrary", "arbitrary"), vmem_limit_bytes=VMEM_LIMIT),
        name=name,
    )(qg, ka, kb, vt)


def _gqa_attention(qg, ka, kb, vt, tq, kc, unroll):
    T = qg.shape[1]
    return _gqa_call(
        functools.partial(_gqa_kernel, tq=tq, kc=kc, n_chunks=T // kc, unroll=unroll),
        [pltpu.VMEM((GQA_HEADS, LANES, tq), BF16),
         pltpu.VMEM((GQA_HEADS, kc, tq), F32),
         pltpu.VMEM((GQA_HEADS, kc, tq), F32),
         pltpu.VMEM((GQA_HEADS, 1, tq), F32),
         pltpu.VMEM((GQA_HEADS, V_AUG, tq), F32)],
        "gqa_attn", qg, ka, kb, vt, tq)


def _gqa_attention_bounded(qg, ka, kb, vt, tq, unroll):
    T = qg.shape[1]
    return _gqa_call(
        functools.partial(_gqa_bounded_kernel, tq=tq, n_chunks=T // ROWBLOCK, unroll=unroll,
                          lookahead=GQA_LOOKAHEAD),
        [pltpu.VMEM((GQA_HEADS, LANES, tq), BF16),
         pltpu.VMEM((GQA_HEADS, HEAD_DIM, tq), F32),
         pltpu.VMEM((GQA_HEADS, SUBLANES, tq), F32)],
        "gqa_attn_bounded", qg, ka, kb, vt, tq)


def _post_kernel(x_ref, ona_ref, og_ref, mod_ref, wo_ref, gf_ref, wg_ref, wu_ref, wd_ref,
                 gfin_ref, o_ref, *, tm):
    gate_a, shift_f, gain_f, gate_f = (mod_ref[2:3, :], mod_ref[3:4, :], 1.0 + mod_ref[4:5, :],
                                       mod_ref[5:6, :])

    def rms(x):
        return x * lax.rsqrt(jnp.mean(x * x, axis=-1, keepdims=True) + EPS)

    assert tm % (2 * ROWBLOCK) == 0
    for first in range(0, tm, 2 * ROWBLOCK):
        subs = [pl.ds(first + r * ROWBLOCK, ROWBLOCK) for r in range(2)]
        attn = [jnp.dot(ona_ref[rows, :], wo_ref[0:NA_WIDTH, :], preferred_element_type=F32)
                + jnp.dot(og_ref[rows, :], wo_ref[NA_WIDTH:, :], preferred_element_type=F32)
                for rows in subs]
        x1 = [x_ref[rows, :] + gate_a * a for rows, a in zip(subs, attn)]
        hb = [((rms(v) * gf_ref[...]) * gain_f + shift_f).astype(BF16) for v in x1]
        gu = [(jnp.dot(h, wg_ref[...], preferred_element_type=F32),
               jnp.dot(h, wu_ref[...], preferred_element_type=F32)) for h in hb]
        ff = [(_silu(g) * u).astype(BF16) for g, u in gu]
        x2 = [v + gate_f * jnp.dot(f, wd_ref[...], preferred_element_type=F32)
              for v, f in zip(x1, ff)]
        for rows, v in zip(subs, x2):
            o_ref[rows, :] = rms(v) * gfin_ref[...]


def _post(x, o_na, o_g, mod, w_o, g_ffn, w_gate, w_up, w_down, g_final, tm):
    B, T, _ = x.shape
    d_ff = w_gate.shape[1]
    row = lambda b, i: (b, i, 0)
    const2 = lambda b, i: (0, 0)
    resident = pl.Buffered(1)
    return pl.pallas_call(
        functools.partial(_post_kernel, tm=tm),
        out_shape=jax.ShapeDtypeStruct((B, T, D_MODEL), F32),
        grid=(B, T // tm),
        in_specs=[
            pl.BlockSpec((None, tm, D_MODEL), row),
            pl.BlockSpec((None, tm, NA_WIDTH), row),
            pl.BlockSpec((None, tm, GQA_WIDTH), row),
            pl.BlockSpec((None, N_MOD, D_MODEL), lambda b, i: (b, 0, 0)),
            pl.BlockSpec((D_MODEL, D_MODEL), const2, pipeline_mode=resident),
            pl.BlockSpec((1, D_MODEL), const2),
            pl.BlockSpec((D_MODEL, d_ff), const2, pipeline_mode=resident),
            pl.BlockSpec((D_MODEL, d_ff), const2, pipeline_mode=resident),
            pl.BlockSpec((d_ff, D_MODEL), const2, pipeline_mode=resident),
            pl.BlockSpec((1, D_MODEL), const2),
        ],
        out_specs=pl.BlockSpec((None, tm, D_MODEL), row),
        compiler_params=pltpu.CompilerParams(
            dimension_semantics=("arbitrary", "arbitrary"), vmem_limit_bytes=VMEM_LIMIT),
        name="post",
    )(x, o_na, o_g, mod, w_o, g_ffn, w_gate, w_up, w_down, g_final)


def _rope_tables(n_tokens):
    t = np.arange(n_tokens)
    row = (t // GRID_W).astype(np.float64)
    col = (t % GRID_W).astype(np.float64)

    def angles(pos, dims):
        inv = ROPE_THETA ** (-np.arange(0, dims, 2, dtype=np.float64) / dims)
        return pos[:, None] * inv[None, :]

    ang = np.concatenate([angles(row, HEAD_DIM // 2), angles(col, HEAD_DIM // 2)], axis=-1)
    cos, sin = np.cos(ang), np.sin(ang)
    reps = LANES // HEAD_DIM
    return (jnp.asarray(np.tile(np.concatenate([cos, cos], axis=-1), (1, reps)), F32),
            jnp.asarray(np.tile(np.concatenate([-sin, sin], axis=-1), (1, reps)), F32))


def _permute_in_columns(w):
    q_na, k_na, v_na, q_g, k_g, v_g = jnp.split(
        w, (NA_WIDTH, 2 * NA_WIDTH, 3 * NA_WIDTH, 3 * NA_WIDTH + GQA_WIDTH,
            3 * NA_WIDTH + GQA_WIDTH + KV_WIDTH), axis=-1)
    return jnp.concatenate([q_g, k_g, v_g, v_na, q_na, k_na], axis=-1)


def kernel(x, c, w_ada, b_ada, g_attn, w_in, g_q, g_k, rpb, w_o, g_ffn, w_gate, w_up, w_down, g_final):
    B, T, _ = x.shape
    assert w_ada.shape[0] == 1, "single-layer block: the final norm is fused into the layer"
    cos_t, sin_t = _rope_tables(T)
    reps = LANES // HEAD_DIM
    mod = _ada(c, w_ada[0], b_ada[0]).reshape(B, N_MOD, D_MODEL)
    qn, kn, qg, ka, kb, vt = _in_proj(
        x, mod, g_attn[0].reshape(1, D_MODEL), _permute_in_columns(w_in[0]).astype(BF16),
        cos_t, sin_t,
        jnp.tile(g_q[0], reps).reshape(1, LANES), jnp.tile(g_k[0], reps).reshape(1, LANES),
        tm=IN_ROWS)
    o_fast, denom_stats = _na_attention_unshifted(rpb[0], qn, kn, vt)
    verified = ((jnp.min(denom_stats[:, :, 0]) >= NA_DENOM_RANGE[0])
                & (jnp.max(denom_stats[:, :, 1]) <= NA_DENOM_RANGE[1]))
    o_na = lax.cond(verified, lambda o, *a: o, lambda o, *a: _na_attention(*a),
                    o_fast, rpb[0], qn, kn, vt)
    score_bound = (jnp.max(jnp.abs(g_q[0])) * jnp.max(jnp.abs(g_k[0]))
                   * (HEAD_DIM * HEAD_DIM ** -0.5 * LOG2E * 1.02))
    o_g = lax.cond(
        score_bound <= GQA_SCORE_LIMIT,
        lambda *a: _gqa_attention_bounded(*a, tq=GQA_QUERIES, unroll=GQA_UNROLL_BOUNDED),
        lambda *a: _gqa_attention(*a, tq=GQA_QUERIES, kc=GQA_KEYS, unroll=GQA_UNROLL_ONLINE),
        qg, ka, kb, vt)
    return _post(x, o_na, o_g, mod, w_o[0].astype(BF16), g_ffn[0].reshape(1, D_MODEL),
                 w_gate[0].astype(BF16), w_up[0].astype(BF16), w_down[0].astype(BF16),
                 g_final.reshape(1, D_MODEL), tm=POST_ROWS)
```

```python
import functools

import jax
import jax.numpy as jnp
import numpy as np
from jax import lax
from jax.experimental import pallas as pl
from jax.experimental.pallas import tpu as pltpu

F32 = jnp.float32
BF16 = jnp.bfloat16

D_MODEL = 1024
HEAD_DIM = 64
NA_HEADS = 8
GQA_HEADS = 8
GQA_KV_HEADS = 2
GQA_GROUP = GQA_HEADS // GQA_KV_HEADS
NA_WIDTH = NA_HEADS * HEAD_DIM
GQA_WIDTH = GQA_HEADS * HEAD_DIM
KV_WIDTH = GQA_KV_HEADS * HEAD_DIM
IN_WIDTH = 3 * NA_WIDTH + GQA_WIDTH + 2 * KV_WIDTH
V_WIDTH = NA_WIDTH + KV_WIDTH
GRID_W = 64
NA_WIN_H = 8
NA_WIN_W = 16
ROPE_THETA = 10000.0
N_MOD = 6
EPS = 1e-6

LANES = 128
SUBLANES = 8
BF16_SUBLANES = 16
ROWBLOCK_ROWS = 4
ROWBLOCK = ROWBLOCK_ROWS * GRID_W
BAND_BLOCKS = 3
BAND = BAND_BLOCKS * ROWBLOCK
V_AUG = HEAD_DIM + BF16_SUBLANES
MASKED = -1e30
LOG2E = 1.4426950408889634
DENOM_RANGE = (2.0 ** -60, 2.0 ** 100)
VMEM_LIMIT = 56 * 1024 * 1024

IN_ROWS = 1024
POST_ROWS = 1024
GQA_QUERIES = 512
GQA_KEYS = ROWBLOCK
GQA_UNROLL_ONLINE = 4
GQA_UNROLL_BOUNDED = 16
GQA_LOOKAHEAD = 4
NA_LOOKAHEAD = 2

COL_QG, COL_KG, COL_VG, COL_VN, COL_QN, COL_KN = 0, 512, 640, 768, 1280, 1792
IN_GROUPS = ((0, COL_VN), (COL_VN, COL_QN), (COL_QN, IN_WIDTH))


def _silu(x):
    return x / (1.0 + jnp.exp(-x))


def _with_ones(vt):
    return jnp.concatenate([vt, jnp.ones((BF16_SUBLANES, vt.shape[1]), vt.dtype)], axis=0)


def _ada_kernel(c_ref, w_ref, b_ref, o_ref):
    ca = _silu(c_ref[...])
    o_ref[...] = jnp.dot(ca.astype(BF16), w_ref[...].astype(BF16),
                         preferred_element_type=F32) + b_ref[...]


def _ada(c, w_ada, b_ada):
    B = c.shape[0]
    n = w_ada.shape[1]
    return pl.pallas_call(
        _ada_kernel,
        out_shape=jax.ShapeDtypeStruct((B, n), F32),
        grid=(n // D_MODEL,),
        in_specs=[pl.BlockSpec((B, D_MODEL), lambda j: (0, 0)),
                  pl.BlockSpec((D_MODEL, D_MODEL), lambda j: (0, j)),
                  pl.BlockSpec((1, D_MODEL), lambda j: (0, j))],
        out_specs=pl.BlockSpec((B, D_MODEL), lambda j: (0, j)),
        compiler_params=pltpu.CompilerParams(dimension_semantics=("arbitrary",)),
        name="ada",
    )(c, w_ada, b_ada.reshape(1, n))


def _in_kernel(x_ref, mod_ref, g_ref, w_ref, cos_ref, sin_ref, gq_ref, gk_ref,
               qn_ref, kn_ref, qg_ref, ka_ref, kb_ref, vt_ref, *, tm):
    scale = HEAD_DIM ** -0.5 * LOG2E
    lane = lax.broadcasted_iota(jnp.int32, (ROWBLOCK, LANES), 1)
    low_head = lane < HEAD_DIM
    first_half = (lane & (HEAD_DIM - 1)) < HEAD_DIM // 2
    shift, gain = mod_ref[0:1, :], 1.0 + mod_ref[1:2, :]

    def hidden(rows):
        x = x_ref[rows, :]
        ms = jnp.mean(x * x, axis=-1, keepdims=True)
        return ((x * lax.rsqrt(ms + EPS) * g_ref[...]) * gain + shift).astype(BF16)

    def norm_rope(xg, head_gain, cos, sin):
        sq = xg * xg
        s_lo = jnp.sum(jnp.where(low_head, sq, 0.0), axis=-1, keepdims=True)
        s_hi = jnp.sum(jnp.where(low_head, 0.0, sq), axis=-1, keepdims=True)
        inv = lax.rsqrt(jnp.where(low_head, s_lo, s_hi) * (1.0 / HEAD_DIM) + EPS)
        yg = xg * inv * head_gain
        rot = jnp.where(first_half,
                        pltpu.roll(yg, LANES - HEAD_DIM // 2, 1),
                        pltpu.roll(yg, HEAD_DIM // 2, 1))
        return yg * cos + rot * sin

    subs = [pl.ds(r * ROWBLOCK, ROWBLOCK) for r in range(tm // ROWBLOCK)]
    hs = [hidden(rows) for rows in subs]
    accs = [[jnp.dot(hb, w_ref[:, c0:c1], preferred_element_type=F32) for c0, c1 in IN_GROUPS]
            for hb in hs]
    for r, rows in enumerate(subs):
        a_gqa, a_vn, a_na = accs[r]
        cos, sin = cos_ref[rows, :], sin_ref[rows, :]
        for g in range(GQA_WIDTH // LANES):
            cols = slice(COL_QG + g * LANES, COL_QG + (g + 1) * LANES)
            qg_ref[rows, g * LANES:(g + 1) * LANES] = (
                norm_rope(a_gqa[:, cols], gq_ref[...], cos, sin) * scale).astype(BF16)
        kr = norm_rope(a_gqa[:, COL_KG:COL_KG + KV_WIDTH], gk_ref[...], cos, sin)
        ka_ref[rows, :] = kr.astype(BF16)
        kb_ref[rows, :] = pltpu.roll(kr, HEAD_DIM, 1).astype(BF16)
        for j in range(NA_WIDTH // LANES):
            vt_ref[r, j * LANES:(j + 1) * LANES, :] = a_vn[:, j * LANES:(j + 1) * LANES].T.astype(BF16)
        vt_ref[r, NA_WIDTH:V_WIDTH, :] = a_gqa[:, COL_VG:COL_VG + KV_WIDTH].T.astype(BF16)
        qn_ref[rows, :] = (a_na[:, 0:NA_WIDTH] * scale).astype(BF16)
        kn_ref[rows, :] = a_na[:, NA_WIDTH:2 * NA_WIDTH].astype(BF16)


def _in_proj(x, mod, g_attn, w_in, cos_t, sin_t, gq_t, gk_t, tm):
    B, T, _ = x.shape
    nrb = T // ROWBLOCK
    row = lambda b, i: (b, i, 0)
    const2 = lambda b, i: (0, 0)
    out_shape = (
        jax.ShapeDtypeStruct((B, T, NA_WIDTH), BF16),
        jax.ShapeDtypeStruct((B, T, NA_WIDTH), BF16),
        jax.ShapeDtypeStruct((B, T, GQA_WIDTH), BF16),
        jax.ShapeDtypeStruct((B, T, KV_WIDTH), BF16),
        jax.ShapeDtypeStruct((B, T, KV_WIDTH), BF16),
        jax.ShapeDtypeStruct((B, nrb, V_WIDTH, ROWBLOCK), BF16),
    )
    return pl.pallas_call(
        functools.partial(_in_kernel, tm=tm),
        out_shape=out_shape,
        grid=(B, T // tm),
        in_specs=[
            pl.BlockSpec((None, tm, D_MODEL), row),
            pl.BlockSpec((None, N_MOD, D_MODEL), lambda b, i: (b, 0, 0)),
            pl.BlockSpec((1, D_MODEL), const2),
            pl.BlockSpec((D_MODEL, IN_WIDTH), const2, pipeline_mode=pl.Buffered(1)),
            pl.BlockSpec((tm, LANES), lambda b, i: (i, 0)),
            pl.BlockSpec((tm, LANES), lambda b, i: (i, 0)),
            pl.BlockSpec((1, LANES), const2),
            pl.BlockSpec((1, LANES), const2),
        ],
        out_specs=(
            pl.BlockSpec((None, tm, NA_WIDTH), row),
            pl.BlockSpec((None, tm, NA_WIDTH), row),
            pl.BlockSpec((None, tm, GQA_WIDTH), row),
            pl.BlockSpec((None, tm, KV_WIDTH), row),
            pl.BlockSpec((None, tm, KV_WIDTH), row),
            pl.BlockSpec((None, tm // ROWBLOCK, V_WIDTH, ROWBLOCK), lambda b, i: (b, i, 0, 0)),
        ),
        compiler_params=pltpu.CompilerParams(
            dimension_semantics=("arbitrary", "arbitrary"), vmem_limit_bytes=VMEM_LIMIT),
        name="in_proj",
    )(x, mod, g_attn, w_in, cos_t, sin_t, gq_t, gk_t)


def _na_variant(variant, i, qr):
    if variant == 0:
        valid, di = i < NA_WIN_H, i - qr
    elif variant == 1:
        valid, di = 0 <= i - qr < NA_WIN_H, i - qr - ROWBLOCK_ROWS
    else:
        valid, di = ROWBLOCK_ROWS <= i < ROWBLOCK_ROWS + NA_WIN_H, i - qr - 2 * ROWBLOCK_ROWS
    return di + NA_WIN_H - 1 if valid else None


def _na_key_ranges(variant):
    n_rows = BAND_BLOCKS * ROWBLOCK_ROWS
    halves = []
    for e in range(ROWBLOCK_ROWS // 2):
        rows = [i for i in range(n_rows)
                if any(_na_variant(variant, i, 2 * e + d) is not None for d in range(2))]
        halves.append((rows[0] * GRID_W, (rows[-1] + 1) * GRID_W))
    return (min(r0 for r0, _ in halves), max(r1 for _, r1 in halves)), halves


def _na_build_bias(rpb_ref, cols_ref, tbl_ref):
    n_row_off, n_col_off = 2 * NA_WIN_H - 1, 2 * NA_WIN_W - 1

    @pl.when(pl.program_id(1) == 0)
    def _build_bias():
        lane = lax.broadcasted_iota(jnp.int32, (GRID_W, LANES), 1)
        kc = lax.broadcasted_iota(jnp.int32, (GRID_W, LANES), 0)
        qc = lane & (GRID_W - 1)
        win = jnp.clip(qc - NA_WIN_W // 2, 0, GRID_W - NA_WIN_W)
        col_off = jnp.where((kc >= win) & (kc < win + NA_WIN_W), kc - qc + NA_WIN_W - 1, -1)
        hits = [col_off == o for o in range(n_col_off)]
        for hh in range(2):
            head = 2 * pl.program_id(0) + hh
            for r in range(n_row_off):
                t = jnp.full((GRID_W, LANES), MASKED, F32)
                for o in range(n_col_off):
                    t = jnp.where(hits[o], rpb_ref[head, r * n_col_off + o], t)
                cols_ref[hh, r] = t * LOG2E
        left = lane < GRID_W
        masked = jnp.full((GRID_W, LANES), MASKED, F32)
        for variant in range(3):
            for hh in range(2):
                for i in range(BAND_BLOCKS * ROWBLOCK_ROWS):
                    for qp in range(ROWBLOCK_ROWS // 2):
                        ro = [_na_variant(variant, i, 2 * qp + e) for e in range(2)]
                        t = [masked if r is None else cols_ref[hh, r] for r in ro]
                        tbl_ref[variant, hh, i * GRID_W:(i + 1) * GRID_W,
                                qp * LANES:(qp + 1) * LANES] = jnp.where(left, t[0], t[1])


def _na_band_start(rb, n_rb):
    return min(max(rb - 1, 0), n_rb - BAND_BLOCKS)


def _na_variant_of(rb, n_rb):
    return 0 if rb == 0 else (2 if rb == n_rb - 1 else 1)


def _na_products(q_ref, k_ref, rb, n_rb):
    (u0, u1), _ = _na_key_ranges(_na_variant_of(rb, n_rb))
    k0 = _na_band_start(rb, n_rb) * ROWBLOCK
    q = q_ref[rb * ROWBLOCK:(rb + 1) * ROWBLOCK, :]
    kband = k_ref[k0 + u0:k0 + u1, :]
    qt = q.astype(F32).T.astype(BF16)
    other = jnp.zeros((HEAD_DIM, ROWBLOCK), BF16)
    prods = []
    for hh in range(2):
        qh = jnp.concatenate([qt[0:HEAD_DIM], other] if hh == 0 else [other, qt[HEAD_DIM:]], axis=0)
        prods.append(jnp.dot(kband, qh, preferred_element_type=F32))
    return prods


def _na_weighted_values(vt_ref, cols, rb, hh, n_rb):
    (u0, u1), _ = _na_key_ranges(_na_variant_of(rb, n_rb))
    bs = _na_band_start(rb, n_rb)
    padded = []
    for r0, r1, p in cols:
        pieces = [jnp.zeros((r0, LANES), BF16), p, jnp.zeros((BAND - r1, LANES), BF16)]
        padded.append(jnp.concatenate([x for x in pieces if x.shape[0]], axis=0))
    pb = jnp.concatenate(padded, axis=1)
    o_aug = jnp.zeros((V_AUG, ROWBLOCK), F32)
    for j in range(u0 // ROWBLOCK, pl.cdiv(u1, ROWBLOCK)):
        vt = _with_ones(vt_ref[bs + j, hh * HEAD_DIM:(hh + 1) * HEAD_DIM, :])
        o_aug = o_aug + jnp.dot(vt, pb[j * ROWBLOCK:(j + 1) * ROWBLOCK, :],
                                preferred_element_type=F32)
    denom = o_aug[HEAD_DIM:HEAD_DIM + 1]
    return o_aug[0:HEAD_DIM] / denom, denom


def _na_bounded_kernel(rpb_ref, q_ref, k_ref, vt_ref, o_ref, stat_ref, cols_ref, tbl_ref, *, n_rb,
                       lookahead):
    _na_build_bias(rpb_ref, cols_ref, tbl_ref)
    pending = [_na_products(q_ref, k_ref, rb, n_rb) for rb in range(min(lookahead, n_rb))]
    denoms = []
    for rb in range(n_rb):
        if rb + lookahead < n_rb:
            pending.append(_na_products(q_ref, k_ref, rb + lookahead, n_rb))
        variant = _na_variant_of(rb, n_rb)
        (u0, u1), halves = _na_key_ranges(variant)
        outs = []
        for hh, d in enumerate(pending.pop(0)):
            cols = []
            for e, (r0, r1) in enumerate(halves):
                rows, lanes = slice(r0 - u0, r1 - u0), slice(e * LANES, (e + 1) * LANES)
                p = jnp.exp2(d[rows, lanes] + tbl_ref[variant, hh, r0:r1, lanes]).astype(BF16)
                cols.append((r0, r1, p))
            out, denom = _na_weighted_values(vt_ref, cols, rb, hh, n_rb)
            outs.append(out)
            denoms.append(denom)
        o_pair = jnp.concatenate(outs, axis=0)
        o_ref[rb * ROWBLOCK:(rb + 1) * ROWBLOCK, :] = o_pair.T.astype(BF16)
    stat_ref[...] = jnp.concatenate(
        [functools.reduce(jnp.minimum, denoms), functools.reduce(jnp.maximum, denoms),
         jnp.ones((SUBLANES - 2, ROWBLOCK), F32)], axis=0)


def _na_kernel(rpb_ref, q_ref, k_ref, vt_ref, o_ref, cols_ref, tbl_ref, s0_ref, s1_ref, *, n_rb):
    _na_build_bias(rpb_ref, cols_ref, tbl_ref)
    s_refs = (s0_ref, s1_ref)

    def scores(rb, slot):
        variant = _na_variant_of(rb, n_rb)
        (u0, u1), halves = _na_key_ranges(variant)
        for hh, d in enumerate(_na_products(q_ref, k_ref, rb, n_rb)):
            for e, (r0, r1) in enumerate(halves):
                rows, lanes = slice(r0 - u0, r1 - u0), slice(e * LANES, (e + 1) * LANES)
                s_refs[slot][hh, rows, lanes] = d[rows, lanes] + tbl_ref[variant, hh, r0:r1, lanes]

    def softmax_pv(rb, slot):
        (u0, u1), halves = _na_key_ranges(_na_variant_of(rb, n_rb))
        outs = []
        for hh in range(2):
            cols = []
            for e, (r0, r1) in enumerate(halves):
                sl = (hh, slice(r0 - u0, r1 - u0), slice(e * LANES, (e + 1) * LANES))
                m = jnp.max(s_refs[slot][sl], axis=0, keepdims=True)
                cols.append((r0, r1, jnp.exp2(s_refs[slot][sl] - m).astype(BF16)))
            outs.append(_na_weighted_values(vt_ref, cols, rb, hh, n_rb)[0])
        o_pair = jnp.concatenate(outs, axis=0)
        o_ref[rb * ROWBLOCK:(rb + 1) * ROWBLOCK, :] = o_pair.T.astype(BF16)

    scores(0, 0)
    for rb in range(n_rb):
        if rb + 1 < n_rb:
            scores(rb + 1, (rb + 1) % 2)
        softmax_pv(rb, rb % 2)


def _na_call(kernel_fn, with_stats, extra_scratch, name, rpb, qn, kn, vt):
    B, T, _ = qn.shape
    n_rb = T // ROWBLOCK
    n_pairs = NA_HEADS // 2
    col = lambda p, b: (b, 0, p)
    out_shape = [jax.ShapeDtypeStruct((B, T, NA_WIDTH), BF16)]
    out_specs = [pl.BlockSpec((None, T, LANES), col)]
    if with_stats:
        out_shape.append(jax.ShapeDtypeStruct((n_pairs, B, SUBLANES, ROWBLOCK), F32))
        out_specs.append(pl.BlockSpec((None, None, SUBLANES, ROWBLOCK), lambda p, b: (p, b, 0, 0)))
    return pl.pallas_call(
        functools.partial(kernel_fn, n_rb=n_rb),
        out_shape=tuple(out_shape),
        grid=(n_pairs, B),
        in_specs=[
            pl.BlockSpec(memory_space=pltpu.SMEM),
            pl.BlockSpec((None, T, LANES), col),
            pl.BlockSpec((None, T, LANES), col),
            pl.BlockSpec((None, n_rb, LANES, ROWBLOCK), lambda p, b: (b, 0, p, 0)),
        ],
        out_specs=tuple(out_specs),
        scratch_shapes=[pltpu.VMEM((2, 2 * NA_WIN_H - 1, GRID_W, LANES), F32),
                        pltpu.VMEM((3, 2, BAND, ROWBLOCK), F32)] + extra_scratch,
        compiler_params=pltpu.CompilerParams(
            dimension_semantics=("arbitrary", "arbitrary"), vmem_limit_bytes=VMEM_LIMIT),
        name=name,
    )(rpb.reshape(NA_HEADS, -1), qn, kn, vt)


def _na_attention(rpb, qn, kn, vt):
    scores = [pltpu.VMEM((2, BAND, ROWBLOCK), F32), pltpu.VMEM((2, BAND, ROWBLOCK), F32)]
    return _na_call(_na_kernel, False, scores, "na_attn", rpb, qn, kn, vt)[0]


def _na_attention_unshifted(rpb, qn, kn, vt):
    return _na_call(functools.partial(_na_bounded_kernel, lookahead=NA_LOOKAHEAD), True, [],
                    "na_attn_unshifted", rpb, qn, kn, vt)


def _gqa_stage_queries(q_ref, qm_ref, tq):
    other = jnp.zeros((HEAD_DIM, tq), BF16)
    for g in range(GQA_HEADS // 2):
        q2t = q_ref[:, g * LANES:(g + 1) * LANES].astype(F32).T.astype(BF16)
        qm_ref[2 * g] = jnp.concatenate([q2t[0:HEAD_DIM], other], axis=0)
        qm_ref[2 * g + 1] = jnp.concatenate([other, q2t[HEAD_DIM:]], axis=0)


def _gqa_write_output(acc_ref, o_ref):
    for g in range(GQA_HEADS // 2):
        pair = []
        for e in range(2):
            a = acc_ref[2 * g + e]
            pair.append(a[0:HEAD_DIM] / a[HEAD_DIM:HEAD_DIM + 1])
        o_ref[:, g * LANES:(g + 1) * LANES] = jnp.concatenate(pair, axis=0).T.astype(BF16)


def _gqa_bounded_kernel(q_ref, ka_ref, kb_ref, vt_ref, o_ref, stat_ref, qm_ref, acc_ref, den_ref,
                        *, tq, n_chunks, unroll, lookahead):
    _gqa_stage_queries(q_ref, qm_ref, tq)
    acc_ref[...] = jnp.zeros(acc_ref.shape, F32)
    den_ref[...] = jnp.zeros(den_ref.shape, F32)

    def scores(c, h):
        kvh, half = h // GQA_GROUP, h % 2
        k_ref = ka_ref if half == kvh else kb_ref
        rows = pl.ds(pl.multiple_of(c * ROWBLOCK, ROWBLOCK), ROWBLOCK)
        return jnp.dot(k_ref[rows, :], qm_ref[h], preferred_element_type=F32)

    def chunk_group(j, carry):
        accs = [acc_ref[h] for h in range(GQA_HEADS)]
        dens = [den_ref[h] for h in range(GQA_HEADS)]
        items = [(unroll * j + i, h) for i in range(unroll) for h in range(GQA_HEADS)]
        pending = [scores(c, h) for c, h in items[:lookahead]]
        for n, (c, h) in enumerate(items):
            if n + lookahead < len(items):
                pending.append(scores(*items[n + lookahead]))
            kvh = h // GQA_GROUP
            p = jnp.exp2(pending.pop(0))
            dens[h] = dens[h] + jnp.sum(p.reshape(ROWBLOCK // SUBLANES, SUBLANES, tq), axis=0)
            accs[h] = accs[h] + jnp.dot(vt_ref[c, kvh * HEAD_DIM:(kvh + 1) * HEAD_DIM, :],
                                        p.astype(BF16), preferred_element_type=F32)
        for h in range(GQA_HEADS):
            acc_ref[h] = accs[h]
            den_ref[h] = dens[h]
        return carry

    assert n_chunks % unroll == 0
    lax.fori_loop(0, n_chunks // unroll, chunk_group, 0)
    denoms = [jnp.sum(den_ref[h], axis=0, keepdims=True) for h in range(GQA_HEADS)]
    for g in range(GQA_HEADS // 2):
        pair = [acc_ref[h] / denoms[h] for h in (2 * g, 2 * g + 1)]
        o_ref[:, g * LANES:(g + 1) * LANES] = jnp.concatenate(pair, axis=0).T.astype(BF16)
    stat_ref[...] = jnp.concatenate(
        [functools.reduce(jnp.minimum, denoms), functools.reduce(jnp.maximum, denoms),
         jnp.ones((SUBLANES - 2, tq), F32)], axis=0)


def _gqa_kernel(q_ref, ka_ref, kb_ref, vt_ref, o_ref, qm_ref, s0_ref, s1_ref, m_ref, acc_ref, *,
                tq, kc, n_chunks, unroll):
    _gqa_stage_queries(q_ref, qm_ref, tq)
    m_ref[...] = jnp.full(m_ref.shape, MASKED, F32)
    acc_ref[...] = jnp.zeros(acc_ref.shape, F32)

    s_refs = (s0_ref, s1_ref)

    def scores(h, c, slot):
        kvh, half = h // GQA_GROUP, h % 2
        k_ref = ka_ref if half == kvh else kb_ref
        rows = pl.ds(pl.multiple_of(c * kc, kc), kc)
        s_refs[slot][h] = jnp.dot(k_ref[rows, :], qm_ref[h], preferred_element_type=F32)

    def softmax_pv(h, slot, vt_augs):
        m_old = m_ref[h]
        m_new = jnp.maximum(m_old, jnp.max(s_refs[slot][h], axis=0, keepdims=True))
        alpha = jnp.exp2(m_old - m_new)
        pb = jnp.exp2(s_refs[slot][h] - m_new).astype(BF16)
        pv = sum(jnp.dot(vt, pb[j * ROWBLOCK:(j + 1) * ROWBLOCK, :], preferred_element_type=F32)
                 for j, vt in enumerate(vt_augs))
        acc_ref[h] = alpha * acc_ref[h] + pv
        m_ref[h] = m_new

    def step(c, slot, prefetch):
        blocks = kc // ROWBLOCK
        vt_augs = [[_with_ones(vt_ref[c * blocks + j, kvh * HEAD_DIM:(kvh + 1) * HEAD_DIM, :])
                    for j in range(blocks)] for kvh in range(GQA_KV_HEADS)]
        if prefetch:
            scores(0, c + 1, 1 - slot)
        for h in range(GQA_HEADS):
            if prefetch and h + 1 < GQA_HEADS:
                scores(h + 1, c + 1, 1 - slot)
            softmax_pv(h, slot, vt_augs[h // GQA_GROUP])

    for h in range(GQA_HEADS):
        scores(h, 0, 0)

    def chunk_group(j, carry):
        for i in range(unroll):
            step(unroll * j + i, i % 2, True)
        return carry

    assert n_chunks % unroll == 0 and unroll % 2 == 0 and n_chunks >= 2 * unroll
    chunk_group(0, 0)
    lax.fori_loop(1, n_chunks // unroll - 1, chunk_group, 0)
    for c in range(n_chunks - unroll, n_chunks):
        step(c, c % 2, c + 1 < n_chunks)

    _gqa_write_output(acc_ref, o_ref)


def _gqa_call(kernel_fn, with_stats, scratch_shapes, name, qg, ka, kb, vt, tq):
    B, T, _ = qg.shape
    n_rb = T // ROWBLOCK
    kv_block = NA_WIDTH // KV_WIDTH
    out_shape = [jax.ShapeDtypeStruct((B, T, GQA_WIDTH), BF16)]
    out_specs = [pl.BlockSpec((None, tq, GQA_WIDTH), lambda b, i: (b, i, 0))]
    if with_stats:
        out_shape.append(jax.ShapeDtypeStruct((B, T // tq, SUBLANES, tq), F32))
        out_specs.append(pl.BlockSpec((None, None, SUBLANES, tq), lambda b, i: (b, i, 0, 0)))
    return pl.pallas_call(
        kernel_fn,
        out_shape=tuple(out_shape),
        grid=(B, T // tq),
        in_specs=[
            pl.BlockSpec((None, tq, GQA_WIDTH), lambda b, i: (b, i, 0)),
            pl.BlockSpec((None, T, KV_WIDTH), lambda b, i: (b, 0, 0)),
            pl.BlockSpec((None, T, KV_WIDTH), lambda b, i: (b, 0, 0)),
            pl.BlockSpec((None, n_rb, KV_WIDTH, ROWBLOCK), lambda b, i: (b, 0, kv_block, 0)),
        ],
        out_specs=tuple(out_specs),
        scratch_shapes=scratch_shapes,
        compiler_params=pltpu.CompilerParams(
            dimension_semantics=("arbitrary", "arbitrary"), vmem_limit_bytes=VMEM_LIMIT),
        name=name,
    )(qg, ka, kb, vt)


def _gqa_attention(qg, ka, kb, vt, tq, kc, unroll):
    T = qg.shape[1]
    return _gqa_call(
        functools.partial(_gqa_kernel, tq=tq, kc=kc, n_chunks=T // kc, unroll=unroll), False,
        [pltpu.VMEM((GQA_HEADS, LANES, tq), BF16),
         pltpu.VMEM((GQA_HEADS, kc, tq), F32),
         pltpu.VMEM((GQA_HEADS, kc, tq), F32),
         pltpu.VMEM((GQA_HEADS, 1, tq), F32),
         pltpu.VMEM((GQA_HEADS, V_AUG, tq), F32)],
        "gqa_attn", qg, ka, kb, vt, tq)[0]


def _gqa_attention_unshifted(qg, ka, kb, vt, tq, unroll):
    T = qg.shape[1]
    return _gqa_call(
        functools.partial(_gqa_bounded_kernel, tq=tq, n_chunks=T // ROWBLOCK, unroll=unroll,
                          lookahead=GQA_LOOKAHEAD), True,
        [pltpu.VMEM((GQA_HEADS, LANES, tq), BF16),
         pltpu.VMEM((GQA_HEADS, HEAD_DIM, tq), F32),
         pltpu.VMEM((GQA_HEADS, SUBLANES, tq), F32)],
        "gqa_attn_unshifted", qg, ka, kb, vt, tq)


def _post_kernel(x_ref, ona_ref, og_ref, mod_ref, wo_ref, gf_ref, wg_ref, wu_ref, wd_ref,
                 gfin_ref, o_ref, *, tm):
    gate_a, shift_f, gain_f, gate_f = (mod_ref[2:3, :], mod_ref[3:4, :], 1.0 + mod_ref[4:5, :],
                                       mod_ref[5:6, :])

    def rms(x):
        return x * lax.rsqrt(jnp.mean(x * x, axis=-1, keepdims=True) + EPS)

    assert tm % (2 * ROWBLOCK) == 0
    for first in range(0, tm, 2 * ROWBLOCK):
        subs = [pl.ds(first + r * ROWBLOCK, ROWBLOCK) for r in range(2)]
        attn = [jnp.dot(ona_ref[rows, :], wo_ref[0:NA_WIDTH, :], preferred_element_type=F32)
                + jnp.dot(og_ref[rows, :], wo_ref[NA_WIDTH:, :], preferred_element_type=F32)
                for rows in subs]
        x1 = [x_ref[rows, :] + gate_a * a for rows, a in zip(subs, attn)]
        hb = [((rms(v) * gf_ref[...]) * gain_f + shift_f).astype(BF16) for v in x1]
        gu = [(jnp.dot(h, wg_ref[...], preferred_element_type=F32),
               jnp.dot(h, wu_ref[...], preferred_element_type=F32)) for h in hb]
        ff = [(_silu(g) * u).astype(BF16) for g, u in gu]
        x2 = [v + gate_f * jnp.dot(f, wd_ref[...], preferred_element_type=F32)
              for v, f in zip(x1, ff)]
        for rows, v in zip(subs, x2):
            o_ref[rows, :] = rms(v) * gfin_ref[...]


def _post(x, o_na, o_g, mod, w_o, g_ffn, w_gate, w_up, w_down, g_final, tm):
    B, T, _ = x.shape
    d_ff = w_gate.shape[1]
    row = lambda b, i: (b, i, 0)
    const2 = lambda b, i: (0, 0)
    resident = pl.Buffered(1)
    return pl.pallas_call(
        functools.partial(_post_kernel, tm=tm),
        out_shape=jax.ShapeDtypeStruct((B, T, D_MODEL), F32),
        grid=(B, T // tm),
        in_specs=[
            pl.BlockSpec((None, tm, D_MODEL), row),
            pl.BlockSpec((None, tm, NA_WIDTH), row),
            pl.BlockSpec((None, tm, GQA_WIDTH), row),
            pl.BlockSpec((None, N_MOD, D_MODEL), lambda b, i: (b, 0, 0)),
            pl.BlockSpec((D_MODEL, D_MODEL), const2, pipeline_mode=resident),
            pl.BlockSpec((1, D_MODEL), const2),
            pl.BlockSpec((D_MODEL, d_ff), const2, pipeline_mode=resident),
            pl.BlockSpec((D_MODEL, d_ff), const2, pipeline_mode=resident),
            pl.BlockSpec((d_ff, D_MODEL), const2, pipeline_mode=resident),
            pl.BlockSpec((1, D_MODEL), const2),
        ],
        out_specs=pl.BlockSpec((None, tm, D_MODEL), row),
        compiler_params=pltpu.CompilerParams(
            dimension_semantics=("arbitrary", "arbitrary"), vmem_limit_bytes=VMEM_LIMIT),
        name="post",
    )(x, o_na, o_g, mod, w_o, g_ffn, w_gate, w_up, w_down, g_final)


def _rope_tables(n_tokens):
    t = np.arange(n_tokens)
    row = (t // GRID_W).astype(np.float64)
    col = (t % GRID_W).astype(np.float64)

    def angles(pos, dims):
        inv = ROPE_THETA ** (-np.arange(0, dims, 2, dtype=np.float64) / dims)
        return pos[:, None] * inv[None, :]

    ang = np.concatenate([angles(row, HEAD_DIM // 2), angles(col, HEAD_DIM // 2)], axis=-1)
    cos, sin = np.cos(ang), np.sin(ang)
    reps = LANES // HEAD_DIM
    return (jnp.asarray(np.tile(np.concatenate([cos, cos], axis=-1), (1, reps)), F32),
            jnp.asarray(np.tile(np.concatenate([-sin, sin], axis=-1), (1, reps)), F32))


def _permute_in_columns(w):
    q_na, k_na, v_na, q_g, k_g, v_g = jnp.split(
        w, (NA_WIDTH, 2 * NA_WIDTH, 3 * NA_WIDTH, 3 * NA_WIDTH + GQA_WIDTH,
            3 * NA_WIDTH + GQA_WIDTH + KV_WIDTH), axis=-1)
    return jnp.concatenate([q_g, k_g, v_g, v_na, q_na, k_na], axis=-1)


def kernel(x, c, w_ada, b_ada, g_attn, w_in, g_q, g_k, rpb, w_o, g_ffn, w_gate, w_up, w_down, g_final):
    B, T, _ = x.shape
    assert w_ada.shape[0] == 1, "single-layer block: the final norm is fused into the layer"
    cos_t, sin_t = _rope_tables(T)
    reps = LANES // HEAD_DIM
    mod = _ada(c, w_ada[0], b_ada[0]).reshape(B, N_MOD, D_MODEL)
    qn, kn, qg, ka, kb, vt = _in_proj(
        x, mod, g_attn[0].reshape(1, D_MODEL), _permute_in_columns(w_in[0]).astype(BF16),
        cos_t, sin_t,
        jnp.tile(g_q[0], reps).reshape(1, LANES), jnp.tile(g_k[0], reps).reshape(1, LANES),
        tm=IN_ROWS)
    def verified(stats):
        return ((jnp.min(stats[:, :, 0]) >= DENOM_RANGE[0])
                & (jnp.max(stats[:, :, 1]) <= DENOM_RANGE[1]))

    o_fast, na_stats = _na_attention_unshifted(rpb[0], qn, kn, vt)
    o_na = lax.cond(verified(na_stats), lambda o, *a: o, lambda o, *a: _na_attention(*a),
                    o_fast, rpb[0], qn, kn, vt)
    o_fast, gqa_stats = _gqa_attention_unshifted(qg, ka, kb, vt, tq=GQA_QUERIES,
                                                 unroll=GQA_UNROLL_BOUNDED)
    o_g = lax.cond(
        verified(gqa_stats), lambda o, *a: o,
        lambda o, *a: _gqa_attention(*a, tq=GQA_QUERIES, kc=GQA_KEYS, unroll=GQA_UNROLL_ONLINE),
        o_fast, qg, ka, kb, vt)
    return _post(x, o_na, o_g, mod, w_o[0].astype(BF16), g_ffn[0].reshape(1, D_MODEL),
                 w_gate[0].astype(BF16), w_up[0].astype(BF16), w_down[0].astype(BF16),
                 g_final.reshape(1, D_MODEL), tm=POST_ROWS)
```

```python
import functools

import jax
import jax.numpy as jnp
import numpy as np
from jax import lax
from jax.experimental import pallas as pl
from jax.experimental.pallas import tpu as pltpu

F32 = jnp.float32
BF16 = jnp.bfloat16

D_MODEL = 1024
HEAD_DIM = 64
NA_HEADS = 8
GQA_HEADS = 8
GQA_KV_HEADS = 2
GQA_GROUP = GQA_HEADS // GQA_KV_HEADS
NA_WIDTH = NA_HEADS * HEAD_DIM
GQA_WIDTH = GQA_HEADS * HEAD_DIM
KV_WIDTH = GQA_KV_HEADS * HEAD_DIM
IN_WIDTH = 3 * NA_WIDTH + GQA_WIDTH + 2 * KV_WIDTH
V_WIDTH = NA_WIDTH + KV_WIDTH
GRID_W = 64
NA_WIN_H = 8
NA_WIN_W = 16
ROPE_THETA = 10000.0
N_MOD = 6
EPS = 1e-6

LANES = 128
SUBLANES = 8
BF16_SUBLANES = 16
ROWBLOCK_ROWS = 4
ROWBLOCK = ROWBLOCK_ROWS * GRID_W
BAND_BLOCKS = 3
BAND = BAND_BLOCKS * ROWBLOCK
V_AUG = HEAD_DIM + BF16_SUBLANES
MASKED = -1e30
LOG2E = 1.4426950408889634
GQA_SCORE_LIMIT = 60.0
NA_DENOM_RANGE = (2.0 ** -60, 2.0 ** 100)
VMEM_LIMIT = 56 * 1024 * 1024

IN_ROWS = 1024
POST_ROWS = 1024
GQA_QUERIES = 512
GQA_KEYS = ROWBLOCK
GQA_UNROLL_ONLINE = 4
GQA_UNROLL_BOUNDED = 16
GQA_LOOKAHEAD = 4
NA_LOOKAHEAD = 2

COL_QG, COL_KG, COL_VG, COL_VN, COL_QN, COL_KN = 0, 512, 640, 768, 1280, 1792
IN_GROUPS = ((0, COL_VN), (COL_VN, COL_QN), (COL_QN, IN_WIDTH))


def _silu(x):
    return x / (1.0 + jnp.exp(-x))


def _with_ones(vt):
    return jnp.concatenate([vt, jnp.ones((BF16_SUBLANES, vt.shape[1]), vt.dtype)], axis=0)


def _ada_kernel(c_ref, w_ref, b_ref, o_ref):
    ca = _silu(c_ref[...])
    o_ref[...] = jnp.dot(ca.astype(BF16), w_ref[...].astype(BF16),
                         preferred_element_type=F32) + b_ref[...]


def _ada(c, w_ada, b_ada):
    B = c.shape[0]
    n = w_ada.shape[1]
    return pl.pallas_call(
        _ada_kernel,
        out_shape=jax.ShapeDtypeStruct((B, n), F32),
        grid=(n // D_MODEL,),
        in_specs=[pl.BlockSpec((B, D_MODEL), lambda j: (0, 0)),
                  pl.BlockSpec((D_MODEL, D_MODEL), lambda j: (0, j)),
                  pl.BlockSpec((1, D_MODEL), lambda j: (0, j))],
        out_specs=pl.BlockSpec((B, D_MODEL), lambda j: (0, j)),
        compiler_params=pltpu.CompilerParams(dimension_semantics=("arbitrary",)),
        name="ada",
    )(c, w_ada, b_ada.reshape(1, n))


def _in_kernel(x_ref, mod_ref, g_ref, w_ref, cos_ref, sin_ref, gq_ref, gk_ref,
               qn_ref, kn_ref, qg_ref, ka_ref, kb_ref, vt_ref, *, tm):
    scale = HEAD_DIM ** -0.5 * LOG2E
    lane = lax.broadcasted_iota(jnp.int32, (ROWBLOCK, LANES), 1)
    low_head = lane < HEAD_DIM
    first_half = (lane & (HEAD_DIM - 1)) < HEAD_DIM // 2
    shift, gain = mod_ref[0:1, :], 1.0 + mod_ref[1:2, :]

    def hidden(rows):
        x = x_ref[rows, :]
        ms = jnp.mean(x * x, axis=-1, keepdims=True)
        return ((x * lax.rsqrt(ms + EPS) * g_ref[...]) * gain + shift).astype(BF16)

    def norm_rope(xg, head_gain, cos, sin):
        sq = xg * xg
        s_lo = jnp.sum(jnp.where(low_head, sq, 0.0), axis=-1, keepdims=True)
        s_hi = jnp.sum(jnp.where(low_head, 0.0, sq), axis=-1, keepdims=True)
        inv = lax.rsqrt(jnp.where(low_head, s_lo, s_hi) * (1.0 / HEAD_DIM) + EPS)
        yg = xg * inv * head_gain
        rot = jnp.where(first_half,
                        pltpu.roll(yg, LANES - HEAD_DIM // 2, 1),
                        pltpu.roll(yg, HEAD_DIM // 2, 1))
        return yg * cos + rot * sin

    subs = [pl.ds(r * ROWBLOCK, ROWBLOCK) for r in range(tm // ROWBLOCK)]
    hs = [hidden(rows) for rows in subs]
    accs = [[jnp.dot(hb, w_ref[:, c0:c1], preferred_element_type=F32) for c0, c1 in IN_GROUPS]
            for hb in hs]
    for r, rows in enumerate(subs):
        a_gqa, a_vn, a_na = accs[r]
        cos, sin = cos_ref[rows, :], sin_ref[rows, :]
        for g in range(GQA_WIDTH // LANES):
            cols = slice(COL_QG + g * LANES, COL_QG + (g + 1) * LANES)
            qg_ref[rows, g * LANES:(g + 1) * LANES] = (
                norm_rope(a_gqa[:, cols], gq_ref[...], cos, sin) * scale).astype(BF16)
        kr = norm_rope(a_gqa[:, COL_KG:COL_KG + KV_WIDTH], gk_ref[...], cos, sin)
        ka_ref[rows, :] = kr.astype(BF16)
        kb_ref[rows, :] = pltpu.roll(kr, HEAD_DIM, 1).astype(BF16)
        for j in range(NA_WIDTH // LANES):
            vt_ref[r, j * LANES:(j + 1) * LANES, :] = a_vn[:, j * LANES:(j + 1) * LANES].T.astype(BF16)
        vt_ref[r, NA_WIDTH:V_WIDTH, :] = a_gqa[:, COL_VG:COL_VG + KV_WIDTH].T.astype(BF16)
        qn_ref[rows, :] = (a_na[:, 0:NA_WIDTH] * scale).astype(BF16)
        kn_ref[rows, :] = a_na[:, NA_WIDTH:2 * NA_WIDTH].astype(BF16)


def _in_proj(x, mod, g_attn, w_in, cos_t, sin_t, gq_t, gk_t, tm):
    B, T, _ = x.shape
    nrb = T // ROWBLOCK
    row = lambda b, i: (b, i, 0)
    const2 = lambda b, i: (0, 0)
    out_shape = (
        jax.ShapeDtypeStruct((B, T, NA_WIDTH), BF16),
        jax.ShapeDtypeStruct((B, T, NA_WIDTH), BF16),
        jax.ShapeDtypeStruct((B, T, GQA_WIDTH), BF16),
        jax.ShapeDtypeStruct((B, T, KV_WIDTH), BF16),
        jax.ShapeDtypeStruct((B, T, KV_WIDTH), BF16),
        jax.ShapeDtypeStruct((B, nrb, V_WIDTH, ROWBLOCK), BF16),
    )
    return pl.pallas_call(
        functools.partial(_in_kernel, tm=tm),
        out_shape=out_shape,
        grid=(B, T // tm),
        in_specs=[
            pl.BlockSpec((None, tm, D_MODEL), row),
            pl.BlockSpec((None, N_MOD, D_MODEL), lambda b, i: (b, 0, 0)),
            pl.BlockSpec((1, D_MODEL), const2),
            pl.BlockSpec((D_MODEL, IN_WIDTH), const2, pipeline_mode=pl.Buffered(1)),
            pl.BlockSpec((tm, LANES), lambda b, i: (i, 0)),
            pl.BlockSpec((tm, LANES), lambda b, i: (i, 0)),
            pl.BlockSpec((1, LANES), const2),
            pl.BlockSpec((1, LANES), const2),
        ],
        out_specs=(
            pl.BlockSpec((None, tm, NA_WIDTH), row),
            pl.BlockSpec((None, tm, NA_WIDTH), row),
            pl.BlockSpec((None, tm, GQA_WIDTH), row),
            pl.BlockSpec((None, tm, KV_WIDTH), row),
            pl.BlockSpec((None, tm, KV_WIDTH), row),
            pl.BlockSpec((None, tm // ROWBLOCK, V_WIDTH, ROWBLOCK), lambda b, i: (b, i, 0, 0)),
        ),
        compiler_params=pltpu.CompilerParams(
            dimension_semantics=("arbitrary", "arbitrary"), vmem_limit_bytes=VMEM_LIMIT),
        name="in_proj",
    )(x, mod, g_attn, w_in, cos_t, sin_t, gq_t, gk_t)


def _na_variant(variant, i, qr):
    if variant == 0:
        valid, di = i < NA_WIN_H, i - qr
    elif variant == 1:
        valid, di = 0 <= i - qr < NA_WIN_H, i - qr - ROWBLOCK_ROWS
    else:
        valid, di = ROWBLOCK_ROWS <= i < ROWBLOCK_ROWS + NA_WIN_H, i - qr - 2 * ROWBLOCK_ROWS
    return di + NA_WIN_H - 1 if valid else None


def _na_key_ranges(variant):
    n_rows = BAND_BLOCKS * ROWBLOCK_ROWS
    halves = []
    for e in range(ROWBLOCK_ROWS // 2):
        rows = [i for i in range(n_rows)
                if any(_na_variant(variant, i, 2 * e + d) is not None for d in range(2))]
        halves.append((rows[0] * GRID_W, (rows[-1] + 1) * GRID_W))
    return (min(r0 for r0, _ in halves), max(r1 for _, r1 in halves)), halves


def _na_build_bias(rpb_ref, cols_ref, tbl_ref):
    n_row_off, n_col_off = 2 * NA_WIN_H - 1, 2 * NA_WIN_W - 1

    @pl.when(pl.program_id(1) == 0)
    def _build_bias():
        lane = lax.broadcasted_iota(jnp.int32, (GRID_W, LANES), 1)
        kc = lax.broadcasted_iota(jnp.int32, (GRID_W, LANES), 0)
        qc = lane & (GRID_W - 1)
        win = jnp.clip(qc - NA_WIN_W // 2, 0, GRID_W - NA_WIN_W)
        col_off = jnp.where((kc >= win) & (kc < win + NA_WIN_W), kc - qc + NA_WIN_W - 1, -1)
        hits = [col_off == o for o in range(n_col_off)]
        for hh in range(2):
            head = 2 * pl.program_id(0) + hh
            for r in range(n_row_off):
                t = jnp.full((GRID_W, LANES), MASKED, F32)
                for o in range(n_col_off):
                    t = jnp.where(hits[o], rpb_ref[head, r * n_col_off + o], t)
                cols_ref[hh, r] = t * LOG2E
        left = lane < GRID_W
        masked = jnp.full((GRID_W, LANES), MASKED, F32)
        for variant in range(3):
            for hh in range(2):
                for i in range(BAND_BLOCKS * ROWBLOCK_ROWS):
                    for qp in range(ROWBLOCK_ROWS // 2):
                        ro = [_na_variant(variant, i, 2 * qp + e) for e in range(2)]
                        t = [masked if r is None else cols_ref[hh, r] for r in ro]
                        tbl_ref[variant, hh, i * GRID_W:(i + 1) * GRID_W,
                                qp * LANES:(qp + 1) * LANES] = jnp.where(left, t[0], t[1])


def _na_band_start(rb, n_rb):
    return min(max(rb - 1, 0), n_rb - BAND_BLOCKS)


def _na_variant_of(rb, n_rb):
    return 0 if rb == 0 else (2 if rb == n_rb - 1 else 1)


def _na_products(q_ref, k_ref, rb, n_rb):
    (u0, u1), _ = _na_key_ranges(_na_variant_of(rb, n_rb))
    k0 = _na_band_start(rb, n_rb) * ROWBLOCK
    q = q_ref[rb * ROWBLOCK:(rb + 1) * ROWBLOCK, :]
    kband = k_ref[k0 + u0:k0 + u1, :]
    qt = q.astype(F32).T.astype(BF16)
    other = jnp.zeros((HEAD_DIM, ROWBLOCK), BF16)
    prods = []
    for hh in range(2):
        qh = jnp.concatenate([qt[0:HEAD_DIM], other] if hh == 0 else [other, qt[HEAD_DIM:]], axis=0)
        prods.append(jnp.dot(kband, qh, preferred_element_type=F32))
    return prods


def _na_weighted_values(vt_ref, cols, rb, hh, n_rb, denom=None):
    (u0, u1), _ = _na_key_ranges(_na_variant_of(rb, n_rb))
    bs = _na_band_start(rb, n_rb)
    padded = []
    for r0, r1, p in cols:
        pieces = [jnp.zeros((r0, LANES), BF16), p, jnp.zeros((BAND - r1, LANES), BF16)]
        padded.append(jnp.concatenate([x for x in pieces if x.shape[0]], axis=0))
    pb = jnp.concatenate(padded, axis=1)
    o_aug = jnp.zeros((V_AUG if denom is None else HEAD_DIM, ROWBLOCK), F32)
    for j in range(u0 // ROWBLOCK, pl.cdiv(u1, ROWBLOCK)):
        vt = vt_ref[bs + j, hh * HEAD_DIM:(hh + 1) * HEAD_DIM, :]
        o_aug = o_aug + jnp.dot(_with_ones(vt) if denom is None else vt,
                                pb[j * ROWBLOCK:(j + 1) * ROWBLOCK, :],
                                preferred_element_type=F32)
    if denom is None:
        denom = o_aug[HEAD_DIM:HEAD_DIM + 1]
    return o_aug[0:HEAD_DIM] / denom, denom


def _na_bounded_kernel(rpb_ref, q_ref, k_ref, vt_ref, o_ref, stat_ref, cols_ref, tbl_ref, *, n_rb,
                       lookahead):
    _na_build_bias(rpb_ref, cols_ref, tbl_ref)
    pending = [_na_products(q_ref, k_ref, rb, n_rb) for rb in range(min(lookahead, n_rb))]
    denoms = []
    for rb in range(n_rb):
        if rb + lookahead < n_rb:
            pending.append(_na_products(q_ref, k_ref, rb + lookahead, n_rb))
        variant = _na_variant_of(rb, n_rb)
        (u0, u1), halves = _na_key_ranges(variant)
        outs = []
        for hh, d in enumerate(pending.pop(0)):
            cols, sums = [], []
            for e, (r0, r1) in enumerate(halves):
                rows, lanes = slice(r0 - u0, r1 - u0), slice(e * LANES, (e + 1) * LANES)
                p = jnp.exp2(d[rows, lanes] + tbl_ref[variant, hh, r0:r1, lanes])
                sums.append(jnp.sum(p.reshape((r1 - r0) // SUBLANES, SUBLANES, LANES), axis=0))
                cols.append((r0, r1, p.astype(BF16)))
            denom = jnp.sum(jnp.concatenate(sums, axis=1), axis=0, keepdims=True)
            out, denom = _na_weighted_values(vt_ref, cols, rb, hh, n_rb, denom)
            outs.append(out)
            denoms.append(denom)
        o_pair = jnp.concatenate(outs, axis=0)
        o_ref[rb * ROWBLOCK:(rb + 1) * ROWBLOCK, :] = o_pair.T.astype(BF16)
    stat_ref[...] = jnp.concatenate(
        [functools.reduce(jnp.minimum, denoms), functools.reduce(jnp.maximum, denoms),
         jnp.ones((SUBLANES - 2, ROWBLOCK), F32)], axis=0)


def _na_kernel(rpb_ref, q_ref, k_ref, vt_ref, o_ref, cols_ref, tbl_ref, s0_ref, s1_ref, *, n_rb):
    _na_build_bias(rpb_ref, cols_ref, tbl_ref)
    s_refs = (s0_ref, s1_ref)

    def scores(rb, slot):
        variant = _na_variant_of(rb, n_rb)
        (u0, u1), halves = _na_key_ranges(variant)
        for hh, d in enumerate(_na_products(q_ref, k_ref, rb, n_rb)):
            for e, (r0, r1) in enumerate(halves):
                rows, lanes = slice(r0 - u0, r1 - u0), slice(e * LANES, (e + 1) * LANES)
                s_refs[slot][hh, rows, lanes] = d[rows, lanes] + tbl_ref[variant, hh, r0:r1, lanes]

    def softmax_pv(rb, slot):
        (u0, u1), halves = _na_key_ranges(_na_variant_of(rb, n_rb))
        outs = []
        for hh in range(2):
            cols = []
            for e, (r0, r1) in enumerate(halves):
                sl = (hh, slice(r0 - u0, r1 - u0), slice(e * LANES, (e + 1) * LANES))
                m = jnp.max(s_refs[slot][sl], axis=0, keepdims=True)
                cols.append((r0, r1, jnp.exp2(s_refs[slot][sl] - m).astype(BF16)))
            outs.append(_na_weighted_values(vt_ref, cols, rb, hh, n_rb)[0])
        o_pair = jnp.concatenate(outs, axis=0)
        o_ref[rb * ROWBLOCK:(rb + 1) * ROWBLOCK, :] = o_pair.T.astype(BF16)

    scores(0, 0)
    for rb in range(n_rb):
        if rb + 1 < n_rb:
            scores(rb + 1, (rb + 1) % 2)
        softmax_pv(rb, rb % 2)


def _na_call(kernel_fn, with_stats, extra_scratch, name, rpb, qn, kn, vt):
    B, T, _ = qn.shape
    n_rb = T // ROWBLOCK
    n_pairs = NA_HEADS // 2
    col = lambda p, b: (b, 0, p)
    out_shape = [jax.ShapeDtypeStruct((B, T, NA_WIDTH), BF16)]
    out_specs = [pl.BlockSpec((None, T, LANES), col)]
    if with_stats:
        out_shape.append(jax.ShapeDtypeStruct((n_pairs, B, SUBLANES, ROWBLOCK), F32))
        out_specs.append(pl.BlockSpec((None, None, SUBLANES, ROWBLOCK), lambda p, b: (p, b, 0, 0)))
    return pl.pallas_call(
        functools.partial(kernel_fn, n_rb=n_rb),
        out_shape=tuple(out_shape),
        grid=(n_pairs, B),
        in_specs=[
            pl.BlockSpec(memory_space=pltpu.SMEM),
            pl.BlockSpec((None, T, LANES), col),
            pl.BlockSpec((None, T, LANES), col),
            pl.BlockSpec((None, n_rb, LANES, ROWBLOCK), lambda p, b: (b, 0, p, 0)),
        ],
        out_specs=tuple(out_specs),
        scratch_shapes=[pltpu.VMEM((2, 2 * NA_WIN_H - 1, GRID_W, LANES), F32),
                        pltpu.VMEM((3, 2, BAND, ROWBLOCK), F32)] + extra_scratch,
        compiler_params=pltpu.CompilerParams(
            dimension_semantics=("arbitrary", "arbitrary"), vmem_limit_bytes=VMEM_LIMIT),
        name=name,
    )(rpb.reshape(NA_HEADS, -1), qn, kn, vt)


def _na_attention(rpb, qn, kn, vt):
    scores = [pltpu.VMEM((2, BAND, ROWBLOCK), F32), pltpu.VMEM((2, BAND, ROWBLOCK), F32)]
    return _na_call(_na_kernel, False, scores, "na_attn", rpb, qn, kn, vt)[0]


def _na_attention_unshifted(rpb, qn, kn, vt):
    return _na_call(functools.partial(_na_bounded_kernel, lookahead=NA_LOOKAHEAD), True, [],
                    "na_attn_unshifted", rpb, qn, kn, vt)


def _gqa_stage_queries(q_ref, qm_ref, tq):
    other = jnp.zeros((HEAD_DIM, tq), BF16)
    for g in range(GQA_HEADS // 2):
        q2t = q_ref[:, g * LANES:(g + 1) * LANES].astype(F32).T.astype(BF16)
        qm_ref[2 * g] = jnp.concatenate([q2t[0:HEAD_DIM], other], axis=0)
        qm_ref[2 * g + 1] = jnp.concatenate([other, q2t[HEAD_DIM:]], axis=0)


def _gqa_write_output(acc_ref, o_ref):
    for g in range(GQA_HEADS // 2):
        pair = []
        for e in range(2):
            a = acc_ref[2 * g + e]
            pair.append(a[0:HEAD_DIM] / a[HEAD_DIM:HEAD_DIM + 1])
        o_ref[:, g * LANES:(g + 1) * LANES] = jnp.concatenate(pair, axis=0).T.astype(BF16)


def _gqa_bounded_kernel(q_ref, ka_ref, kb_ref, vt_ref, o_ref, qm_ref, acc_ref, den_ref, *, tq,
                        n_chunks, unroll, lookahead):
    _gqa_stage_queries(q_ref, qm_ref, tq)
    acc_ref[...] = jnp.zeros(acc_ref.shape, F32)
    den_ref[...] = jnp.zeros(den_ref.shape, F32)

    def scores(c, h):
        kvh, half = h // GQA_GROUP, h % 2
        k_ref = ka_ref if half == kvh else kb_ref
        rows = pl.ds(pl.multiple_of(c * ROWBLOCK, ROWBLOCK), ROWBLOCK)
        return jnp.dot(k_ref[rows, :], qm_ref[h], preferred_element_type=F32)

    def chunk_group(j, carry):
        accs = [acc_ref[h] for h in range(GQA_HEADS)]
        dens = [den_ref[h] for h in range(GQA_HEADS)]
        items = [(unroll * j + i, h) for i in range(unroll) for h in range(GQA_HEADS)]
        pending = [scores(c, h) for c, h in items[:lookahead]]
        for n, (c, h) in enumerate(items):
            if n + lookahead < len(items):
                pending.append(scores(*items[n + lookahead]))
            kvh = h // GQA_GROUP
            p = jnp.exp2(pending.pop(0))
            dens[h] = dens[h] + jnp.sum(p.reshape(ROWBLOCK // SUBLANES, SUBLANES, tq), axis=0)
            accs[h] = accs[h] + jnp.dot(vt_ref[c, kvh * HEAD_DIM:(kvh + 1) * HEAD_DIM, :],
                                        p.astype(BF16), preferred_element_type=F32)
        for h in range(GQA_HEADS):
            acc_ref[h] = accs[h]
            den_ref[h] = dens[h]
        return carry

    assert n_chunks % unroll == 0
    lax.fori_loop(0, n_chunks // unroll, chunk_group, 0)
    for g in range(GQA_HEADS // 2):
        pair = [acc_ref[h] / jnp.sum(den_ref[h], axis=0, keepdims=True)
                for h in (2 * g, 2 * g + 1)]
        o_ref[:, g * LANES:(g + 1) * LANES] = jnp.concatenate(pair, axis=0).T.astype(BF16)


def _gqa_kernel(q_ref, ka_ref, kb_ref, vt_ref, o_ref, qm_ref, s0_ref, s1_ref, m_ref, acc_ref, *,
                tq, kc, n_chunks, unroll):
    _gqa_stage_queries(q_ref, qm_ref, tq)
    m_ref[...] = jnp.full(m_ref.shape, MASKED, F32)
    acc_ref[...] = jnp.zeros(acc_ref.shape, F32)

    s_refs = (s0_ref, s1_ref)

    def scores(h, c, slot):
        kvh, half = h // GQA_GROUP, h % 2
        k_ref = ka_ref if half == kvh else kb_ref
        rows = pl.ds(pl.multiple_of(c * kc, kc), kc)
        s_refs[slot][h] = jnp.dot(k_ref[rows, :], qm_ref[h], preferred_element_type=F32)

    def softmax_pv(h, slot, vt_augs):
        m_old = m_ref[h]
        m_new = jnp.maximum(m_old, jnp.max(s_refs[slot][h], axis=0, keepdims=True))
        alpha = jnp.exp2(m_old - m_new)
        pb = jnp.exp2(s_refs[slot][h] - m_new).astype(BF16)
        pv = sum(jnp.dot(vt, pb[j * ROWBLOCK:(j + 1) * ROWBLOCK, :], preferred_element_type=F32)
                 for j, vt in enumerate(vt_augs))
        acc_ref[h] = alpha * acc_ref[h] + pv
        m_ref[h] = m_new

    def step(c, slot, prefetch):
        blocks = kc // ROWBLOCK
        vt_augs = [[_with_ones(vt_ref[c * blocks + j, kvh * HEAD_DIM:(kvh + 1) * HEAD_DIM, :])
                    for j in range(blocks)] for kvh in range(GQA_KV_HEADS)]
        if prefetch:
            scores(0, c + 1, 1 - slot)
        for h in range(GQA_HEADS):
            if prefetch and h + 1 < GQA_HEADS:
                scores(h + 1, c + 1, 1 - slot)
            softmax_pv(h, slot, vt_augs[h // GQA_GROUP])

    for h in range(GQA_HEADS):
        scores(h, 0, 0)

    def chunk_group(j, carry):
        for i in range(unroll):
            step(unroll * j + i, i % 2, True)
        return carry

    assert n_chunks % unroll == 0 and unroll % 2 == 0 and n_chunks >= 2 * unroll
    chunk_group(0, 0)
    lax.fori_loop(1, n_chunks // unroll - 1, chunk_group, 0)
    for c in range(n_chunks - unroll, n_chunks):
        step(c, c % 2, c + 1 < n_chunks)

    _gqa_write_output(acc_ref, o_ref)


def _gqa_call(kernel_fn, scratch_shapes, name, qg, ka, kb, vt, tq):
    B, T, _ = qg.shape
    n_rb = T // ROWBLOCK
    kv_block = NA_WIDTH // KV_WIDTH
    return pl.pallas_call(
        kernel_fn,
        out_shape=jax.ShapeDtypeStruct((B, T, GQA_WIDTH), BF16),
        grid=(B, T // tq),
        in_specs=[
            pl.BlockSpec((None, tq, GQA_WIDTH), lambda b, i: (b, i, 0)),
            pl.BlockSpec((None, T, KV_WIDTH), lambda b, i: (b, 0, 0)),
            pl.BlockSpec((None, T, KV_WIDTH), lambda b, i: (b, 0, 0)),
            pl.BlockSpec((None, n_rb, KV_WIDTH, ROWBLOCK), lambda b, i: (b, 0, kv_block, 0)),
        ],
        out_specs=pl.BlockSpec((None, tq, GQA_WIDTH), lambda b, i: (b, i, 0)),
        scratch_shapes=scratch_shapes,
        compiler_params=pltpu.CompilerParams(
            dimension_semantics=("arbitrary", "arbitrary"), vmem_limit_bytes=VMEM_LIMIT),
        name=name,
    )(qg, ka, kb, vt)


def _gqa_attention(qg, ka, kb, vt, tq, kc, unroll):
    T = qg.shape[1]
    return _gqa_call(
        functools.partial(_gqa_kernel, tq=tq, kc=kc, n_chunks=T // kc, unroll=unroll),
        [pltpu.VMEM((GQA_HEADS, LANES, tq), BF16),
         pltpu.VMEM((GQA_HEADS, kc, tq), F32),
         pltpu.VMEM((GQA_HEADS, kc, tq), F32),
         pltpu.VMEM((GQA_HEADS, 1, tq), F32),
         pltpu.VMEM((GQA_HEADS, V_AUG, tq), F32)],
        "gqa_attn", qg, ka, kb, vt, tq)


def _gqa_attention_bounded(qg, ka, kb, vt, tq, unroll):
    T = qg.shape[1]
    return _gqa_call(
        functools.partial(_gqa_bounded_kernel, tq=tq, n_chunks=T // ROWBLOCK, unroll=unroll,
                          lookahead=GQA_LOOKAHEAD),
        [pltpu.VMEM((GQA_HEADS, LANES, tq), BF16),
         pltpu.VMEM((GQA_HEADS, HEAD_DIM, tq), F32),
         pltpu.VMEM((GQA_HEADS, SUBLANES, tq), F32)],
        "gqa_attn_bounded", qg, ka, kb, vt, tq)


def _post_kernel(x_ref, ona_ref, og_ref, mod_ref, wo_ref, gf_ref, wg_ref, wu_ref, wd_ref,
                 gfin_ref, o_ref, *, tm):
    gate_a, shift_f, gain_f, gate_f = (mod_ref[2:3, :], mod_ref[3:4, :], 1.0 + mod_ref[4:5, :],
                                       mod_ref[5:6, :])

    def rms(x):
        return x * lax.rsqrt(jnp.mean(x * x, axis=-1, keepdims=True) + EPS)

    assert tm % (2 * ROWBLOCK) == 0
    for first in range(0, tm, 2 * ROWBLOCK):
        subs = [pl.ds(first + r * ROWBLOCK, ROWBLOCK) for r in range(2)]
        attn = [jnp.dot(ona_ref[rows, :], wo_ref[0:NA_WIDTH, :], preferred_element_type=F32)
                + jnp.dot(og_ref[rows, :], wo_ref[NA_WIDTH:, :], preferred_element_type=F32)
                for rows in subs]
        x1 = [x_ref[rows, :] + gate_a * a for rows, a in zip(subs, attn)]
        hb = [((rms(v) * gf_ref[...]) * gain_f + shift_f).astype(BF16) for v in x1]
        gu = [(jnp.dot(h, wg_ref[...], preferred_element_type=F32),
               jnp.dot(h, wu_ref[...], preferred_element_type=F32)) for h in hb]
        ff = [(_silu(g) * u).astype(BF16) for g, u in gu]
        x2 = [v + gate_f * jnp.dot(f, wd_ref[...], preferred_element_type=F32)
              for v, f in zip(x1, ff)]
        for rows, v in zip(subs, x2):
            o_ref[rows, :] = rms(v) * gfin_ref[...]


def _post(x, o_na, o_g, mod, w_o, g_ffn, w_gate, w_up, w_down, g_final, tm):
    B, T, _ = x.shape
    d_ff = w_gate.shape[1]
    row = lambda b, i: (b, i, 0)
    const2 = lambda b, i: (0, 0)
    resident = pl.Buffered(1)
    return pl.pallas_call(
        functools.partial(_post_kernel, tm=tm),
        out_shape=jax.ShapeDtypeStruct((B, T, D_MODEL), F32),
        grid=(B, T // tm),
        in_specs=[
            pl.BlockSpec((None, tm, D_MODEL), row),
            pl.BlockSpec((None, tm, NA_WIDTH), row),
            pl.BlockSpec((None, tm, GQA_WIDTH), row),
            pl.BlockSpec((None, N_MOD, D_MODEL), lambda b, i: (b, 0, 0)),
            pl.BlockSpec((D_MODEL, D_MODEL), const2, pipeline_mode=resident),
            pl.BlockSpec((1, D_MODEL), const2),
            pl.BlockSpec((D_MODEL, d_ff), const2, pipeline_mode=resident),
            pl.BlockSpec((D_MODEL, d_ff), const2, pipeline_mode=resident),
            pl.BlockSpec((d_ff, D_MODEL), const2, pipeline_mode=resident),
            pl.BlockSpec((1, D_MODEL), const2),
        ],
        out_specs=pl.BlockSpec((None, tm, D_MODEL), row),
        compiler_params=pltpu.CompilerParams(
            dimension_semantics=("arbitrary", "arbitrary"), vmem_limit_bytes=VMEM_LIMIT),
        name="post",
    )(x, o_na, o_g, mod, w_o, g_ffn, w_gate, w_up, w_down, g_final)


def _rope_tables(n_tokens):
    t = np.arange(n_tokens)
    row = (t // GRID_W).astype(np.float64)
    col = (t % GRID_W).astype(np.float64)

    def angles(pos, dims):
        inv = ROPE_THETA ** (-np.arange(0, dims, 2, dtype=np.float64) / dims)
        return pos[:, None] * inv[None, :]

    ang = np.concatenate([angles(row, HEAD_DIM // 2), angles(col, HEAD_DIM // 2)], axis=-1)
    cos, sin = np.cos(ang), np.sin(ang)
    reps = LANES // HEAD_DIM
    return (jnp.asarray(np.tile(np.concatenate([cos, cos], axis=-1), (1, reps)), F32),
            jnp.asarray(np.tile(np.concatenate([-sin, sin], axis=-1), (1, reps)), F32))


def _permute_in_columns(w):
    q_na, k_na, v_na, q_g, k_g, v_g = jnp.split(
        w, (NA_WIDTH, 2 * NA_WIDTH, 3 * NA_WIDTH, 3 * NA_WIDTH + GQA_WIDTH,
            3 * NA_WIDTH + GQA_WIDTH + KV_WIDTH), axis=-1)
    return jnp.concatenate([q_g, k_g, v_g, v_na, q_na, k_na], axis=-1)


def kernel(x, c, w_ada, b_ada, g_attn, w_in, g_q, g_k, rpb, w_o, g_ffn, w_gate, w_up, w_down, g_final):
    B, T, _ = x.shape
    assert w_ada.shape[0] == 1, "single-layer block: the final norm is fused into the layer"
    cos_t, sin_t = _rope_tables(T)
    reps = LANES // HEAD_DIM
    mod = _ada(c, w_ada[0], b_ada[0]).reshape(B, N_MOD, D_MODEL)
    qn, kn, qg, ka, kb, vt = _in_proj(
        x, mod, g_attn[0].reshape(1, D_MODEL), _permute_in_columns(w_in[0]).astype(BF16),
        cos_t, sin_t,
        jnp.tile(g_q[0], reps).reshape(1, LANES), jnp.tile(g_k[0], reps).reshape(1, LANES),
        tm=IN_ROWS)
    o_fast, denom_stats = _na_attention_unshifted(rpb[0], qn, kn, vt)
    verified = ((jnp.min(denom_stats[:, :, 0]) >= NA_DENOM_RANGE[0])
                & (jnp.max(denom_stats[:, :, 1]) <= NA_DENOM_RANGE[1]))
    o_na = lax.cond(verified, lambda o, *a: o, lambda o, *a: _na_attention(*a),
                    o_fast, rpb[0], qn, kn, vt)
    score_bound = (jnp.max(jnp.abs(g_q[0])) * jnp.max(jnp.abs(g_k[0]))
                   * (HEAD_DIM * HEAD_DIM ** -0.5 * LOG2E * 1.02))
    o_g = lax.cond(
        score_bound <= GQA_SCORE_LIMIT,
        lambda *a: _gqa_attention_bounded(*a, tq=GQA_QUERIES, unroll=GQA_UNROLL_BOUNDED),
        lambda *a: _gqa_attention(*a, tq=GQA_QUERIES, kc=GQA_KEYS, unroll=GQA_UNROLL_ONLINE),
        qg, ka, kb, vt)
    return _post(x, o_na, o_g, mod, w_o[0].astype(BF16), g_ffn[0].reshape(1, D_MODEL),
                 w_gate[0].astype(BF16), w_up[0].astype(BF16), w_down[0].astype(BF16),
                 g_final.reshape(1, D_MODEL), tm=POST_ROWS)
```
